```python
import math
import jax, jax.numpy as jnp
from jax import lax
import numpy as np

D_MODEL = 1024
BATCH = 8
SEQ = 2048
DEPTH = 2
DEC_BATCH = 128
DEC_SEQ = 8
PAST_LEN = 16384
PAGE_SIZE = 128

CONV_W = 4
N_BRANCH = 3
EPS = 1e-6
RG_WIDTH = D_MODEL // 2
RG_BLOCKS = 8
RG_BLOCK = RG_WIDTH // RG_BLOCKS
RG_C = 8.0
S5_WIDTH = D_MODEL // 2
S5_GROUP = 16
S5_GROUPS = S5_WIDTH // S5_GROUP
S5_STATE = 64
GDN_HEADS = 4
GDN_DK = 128
GDN_DV = 128
GDN_QK = GDN_HEADS * GDN_DK
GDN_V = GDN_HEADS * GDN_DV
QKV_WIDTH = 2 * GDN_QK + GDN_V
GDN_CHUNK = 64
D_FF = ((8 * D_MODEL + 3 * 256 - 1) // (3 * 256)) * 256
N_IN = 2 * RG_WIDTH + S5_WIDTH + QKV_WIDTH + GDN_V + 2 * GDN_HEADS + N_BRANCH * D_MODEL

kernel_name = "hybrid_rglru_s5_gdn_decode_step"


def split_points():
    sizes = (RG_WIDTH, RG_WIDTH, S5_WIDTH, QKV_WIDTH, GDN_V, GDN_HEADS, GDN_HEADS, N_BRANCH * D_MODEL)
    return [int(v) for v in np.cumsum(sizes)[:-1]]


def rmsnorm(x, g):
    xf = x.astype(jnp.float32)
    y = xf * lax.rsqrt(jnp.mean(xf * xf, axis=-1, keepdims=True) + EPS) * g.astype(jnp.float32)
    return y.astype(x.dtype)


def l2norm(x):
    return x * lax.rsqrt(jnp.sum(x * x, axis=-1, keepdims=True) + EPS)


def causal_conv(x, buf, w):
    T = x.shape[1]
    xp = jnp.concatenate([buf.astype(x.dtype), x], axis=1)
    y = xp[:, 0:T] * w[0]
    for k in range(1, CONV_W):
        y = y + xp[:, k:k + T] * w[k]
    return y, xp[:, -(CONV_W - 1):]


def linear_scan(a, b, h0):
    b = b.at[:, 0].add(a[:, 0] * h0)
    def combine(l, r):
        return l[0] * r[0], r[0] * l[1] + r[1]
    _, h = lax.associative_scan(combine, (a, b), axis=1)
    return h


def rglru_branch(xr, gate_in, conv_buf, h0, conv_w, conv_b, wa, ba, wx, bx, lam, w_o):
    B, T, _ = xr.shape
    xc, new_buf = causal_conv(xr, conv_buf, conv_w)
    xc = xc + conv_b
    xb = xc.reshape(B, T, RG_BLOCKS, RG_BLOCK)
    r = jax.nn.sigmoid(jnp.einsum("btni,nij->btnj", xb, wa).reshape(B, T, RG_WIDTH) + ba).astype(jnp.float32)
    i = jax.nn.sigmoid(jnp.einsum("btni,nij->btnj", xb, wx).reshape(B, T, RG_WIDTH) + bx).astype(jnp.float32)
    log_a = -RG_C * r * jax.nn.softplus(-lam.astype(jnp.float32))
    a = jnp.exp(log_a)
    inp = jnp.sqrt(-jnp.expm1(2.0 * log_a)) * (i * xc.astype(jnp.float32))
    h = linear_scan(a, inp, h0.astype(jnp.float32))
    y = h.astype(xr.dtype) * jax.nn.gelu(gate_in)
    return y @ w_o, new_buf, h[:, -1]


def s5_branch(u, s_re, s_im, a_re, a_im, b_re, b_im, c_re, c_im, d, log_dt, glu_w, glu_b, w_o):
    B, T, _ = u.shape
    f32 = jnp.float32
    lam = lax.complex(a_re.astype(f32), a_im.astype(f32))
    dt = jnp.exp(log_dt.astype(f32))[:, None]
    lam_bar = jnp.exp(lam * dt)
    b_bar = ((lam_bar - 1.0) / lam)[..., None] * lax.complex(b_re.astype(f32), b_im.astype(f32))
    c_c = lax.complex(c_re.astype(f32), c_im.astype(f32))
    ug = u.astype(f32).reshape(B, T, S5_GROUPS, S5_GROUP)
    bu = jnp.einsum("gph,btgh->btgp", b_bar, ug)
    a = jnp.broadcast_to(lam_bar, bu.shape)
    s = linear_scan(a, bu, lax.complex(s_re.astype(f32), s_im.astype(f32)))
    y = jnp.einsum("ghp,btgp->btgh", c_c, s).real + d.astype(f32).reshape(S5_GROUPS, S5_GROUP) * ug
    y = jax.nn.gelu(y.reshape(B, T, S5_WIDTH)).astype(u.dtype)
    y = y * jax.nn.sigmoid(y @ glu_w + glu_b)
    s_last = s[:, -1]
    return y @ w_o, s_last.real, s_last.imag


def chunk_gated_delta(q, k, v, beta, g, S0):
    B, T, H, _ = q.shape
    C = min(GDN_CHUNK, T)
    pad = (-T) % C
    if pad:
        def padt(t):
            return jnp.pad(t, [(0, 0), (0, pad)] + [(0, 0)] * (t.ndim - 2))
        q, k, v, beta, g = padt(q), padt(k), padt(v), padt(beta), padt(g)
    N = (T + pad) // C
    def chunks(t):
        t = t.reshape((B, N, C, H) + t.shape[3:])
        return jnp.swapaxes(jnp.moveaxis(t, 1, 0), 2, 3)
    q, k, v, beta, g = chunks(q), chunks(k), chunks(v), chunks(beta), chunks(g)
    gc = jnp.cumsum(g, axis=-1)
    kb = k * beta[..., None]
    vb = v * beta[..., None]
    causal = jnp.tril(jnp.ones((C, C), bool))
    strict = jnp.tril(jnp.ones((C, C), bool), -1)
    diff = gc[..., :, None] - gc[..., None, :]
    decay = jnp.where(causal, jnp.exp(jnp.where(causal, diff, 0.0)), 0.0)
    lmat = jnp.where(strict, jnp.einsum("nbhid,nbhjd->nbhij", kb, k) * decay, 0.0)
    eye = jnp.eye(C, dtype=jnp.float32)
    tmat = lax.linalg.triangular_solve(eye + lmat, jnp.broadcast_to(eye, lmat.shape), left_side=True, lower=True)
    u = tmat @ vb
    w = tmat @ (kb * jnp.exp(gc)[..., None])
    qk = jnp.where(causal, jnp.einsum("nbhid,nbhjd->nbhij", q, k) * decay, 0.0)

    def step(S, xs):
        q_i, k_i, u_i, w_i, qk_i, g_i = xs
        v_new = u_i - w_i @ S
        o_i = (q_i * jnp.exp(g_i)[..., None]) @ S + qk_i @ v_new
        g_last = g_i[..., -1]
        k_dec = k_i * jnp.exp(g_last[..., None] - g_i)[..., None]
        S = S * jnp.exp(g_last)[..., None, None] + jnp.einsum("bhcd,bhce->bhde", k_dec, v_new)
        return S, o_i

    S_last, o = lax.scan(step, S0, (q, k, u, w, qk, gc))
    o = jnp.transpose(o, (1, 0, 3, 2, 4)).reshape(B, N * C, H, GDN_DV)[:, :T]
    return o, S_last


def gdn_branch(qkv, z, a_in, b_in, conv_buf, S0, conv_w, a_log, dt_bias, norm_w, w_o):
    B, T, _ = qkv.shape
    f32 = jnp.float32
    qkv_c, new_buf = causal_conv(qkv, conv_buf, conv_w)
    qkv_c = jax.nn.silu(qkv_c).astype(f32)
    q, k, v = jnp.split(qkv_c, [GDN_QK, 2 * GDN_QK], axis=-1)
    q = l2norm(q.reshape(B, T, GDN_HEADS, GDN_DK)) * (GDN_DK ** -0.5)
    k = l2norm(k.reshape(B, T, GDN_HEADS, GDN_DK))
    v = v.reshape(B, T, GDN_HEADS, GDN_DV)
    beta = jax.nn.sigmoid(b_in.astype(f32))
    g = -jnp.exp(a_log.astype(f32)) * jax.nn.softplus(a_in.astype(f32) + dt_bias.astype(f32))
    o, S_last = chunk_gated_delta(q, k, v, beta, g, S0.astype(f32))
    o = rmsnorm(o, norm_w) * jax.nn.silu(z.astype(f32).reshape(B, T, GDN_HEADS, GDN_DV))
    return o.reshape(B, T, GDN_V).astype(qkv.dtype) @ w_o, new_buf, S_last


def decoder_layer(x, layer_state, l, p):
    conv_rg, h_rg, s5_re, s5_im, conv_gd, s_gd = layer_state
    h = rmsnorm(x, p["norm_pre_mix"][l])
    proj = h @ p["w_in"][l]
    rg_x, rg_gate, s5_u, qkv, z, a_in, b_in, gate_logits = jnp.split(proj, split_points(), axis=-1)
    y_rg, conv_rg, h_rg = rglru_branch(rg_x, rg_gate, conv_rg, h_rg, p["rg_conv_w"][l], p["rg_conv_b"][l],
                                       p["rg_wa"][l], p["rg_ba"][l], p["rg_wx"][l], p["rg_bx"][l],
                                       p["rg_lambda"][l], p["rg_w_out"][l])
    y_s5, s5_re, s5_im = s5_branch(s5_u, s5_re, s5_im, p["s5_a_re"][l], p["s5_a_im"][l], p["s5_b_re"][l],
                                   p["s5_b_im"][l], p["s5_c_re"][l], p["s5_c_im"][l], p["s5_d"][l],
                                   p["s5_log_dt"][l], p["s5_glu_w"][l], p["s5_glu_b"][l], p["s5_w_out"][l])
    y_gd, conv_gd, s_gd = gdn_branch(qkv, z, a_in, b_in, conv_gd, s_gd, p["gdn_conv_w"][l], p["gdn_a_log"][l],
                                     p["gdn_dt_bias"][l], p["gdn_norm_w"][l], p["gdn_w_out"][l])
    g_rg, g_s5, g_gd = jnp.split(jax.nn.sigmoid(gate_logits), N_BRANCH, axis=-1)
    mixed = ((g_rg * y_rg + g_s5 * y_s5 + g_gd * y_gd).astype(x.dtype)) @ p["w_out"][l]
    x = x + rmsnorm(mixed, p["norm_post_mix"][l])
    hf = rmsnorm(x, p["norm_pre_ffn"][l])
    f = (jax.nn.silu(hf @ p["ffn_w_gate"][l]) * (hf @ p["ffn_w_up"][l])) @ p["ffn_w_down"][l]
    x = x + rmsnorm(f, p["norm_post_ffn"][l])
    return x, (conv_rg, h_rg, s5_re, s5_im, conv_gd, s_gd)


def run_stack(x, states, p):
    new = [[] for _ in states]
    for l in range(DEPTH):
        x, layer_state = decoder_layer(x, tuple(s[l] for s in states), l, p)
        for lst, s in zip(new, layer_state):
            lst.append(s)
    return x, [jnp.stack(lst) for lst in new]


def zero_states(b, dtype):
    f32 = jnp.float32
    return (jnp.zeros((DEPTH, b, CONV_W - 1, RG_WIDTH), dtype),
            jnp.zeros((DEPTH, b, RG_WIDTH), f32),
            jnp.zeros((DEPTH, b, S5_GROUPS, S5_STATE), f32),
            jnp.zeros((DEPTH, b, S5_GROUPS, S5_STATE), f32),
            jnp.zeros((DEPTH, b, CONV_W - 1, QKV_WIDTH), dtype),
            jnp.zeros((DEPTH, b, GDN_HEADS, GDN_DK, GDN_DV), f32))


def setup_inputs(seed: int = 0) -> dict:
    key = jax.random.key(seed)
    ks = iter(jax.random.split(key, 64))
    f32 = jnp.float32
    L, D = DEPTH, D_MODEL

    def nrm(shape, scale):
        return jax.random.normal(next(ks), shape, f32) * scale

    def unif(shape, lo, hi):
        return jax.random.uniform(next(ks), shape, f32, lo, hi)

    x_prompt = nrm((BATCH, SEQ, D), 1.0)
    x_sample = nrm((DEC_BATCH, DEC_SEQ, D), 1.0)
    state_rg_conv = nrm((L, DEC_BATCH, CONV_W - 1, RG_WIDTH), 1.0)
    state_rg_h = nrm((L, DEC_BATCH, RG_WIDTH), 0.5)
    state_s5_re = nrm((L, DEC_BATCH, S5_GROUPS, S5_STATE), 0.1)
    state_s5_im = nrm((L, DEC_BATCH, S5_GROUPS, S5_STATE), 0.1)
    state_gdn_conv = nrm((L, DEC_BATCH, CONV_W - 1, QKV_WIDTH), 1.0)
    state_gdn_s = nrm((L, DEC_BATCH, GDN_HEADS, GDN_DK, GDN_DV), 0.5)

    norm_pre_mix = 1.0 + nrm((L, D), 0.01)
    norm_post_mix = 1.0 + nrm((L, D), 0.01)
    norm_pre_ffn = 1.0 + nrm((L, D), 0.01)
    norm_post_ffn = 1.0 + nrm((L, D), 0.01)
    w_in = nrm((L, D, N_IN), D ** -0.5)

    rg_conv_w = nrm((L, CONV_W, RG_WIDTH), CONV_W ** -0.5)
    rg_conv_b = nrm((L, RG_WIDTH), 0.01)
    rg_wa = nrm((L, RG_BLOCKS, RG_BLOCK, RG_BLOCK), RG_BLOCK ** -0.5)
    rg_ba = nrm((L, RG_WIDTH), 0.01)
    rg_wx = nrm((L, RG_BLOCKS, RG_BLOCK, RG_BLOCK), RG_BLOCK ** -0.5)
    rg_bx = nrm((L, RG_WIDTH), 0.01)
    a_pow = unif((L, RG_WIDTH), 0.9, 0.999)
    base = a_pow ** (1.0 / RG_C)
    rg_lambda = jnp.log(base) - jnp.log1p(-base)
    rg_w_out = nrm((L, RG_WIDTH, D), RG_WIDTH ** -0.5)

    n = jnp.arange(S5_STATE, dtype=f32)
    s5_a_re = -0.5 + nrm((L, S5_GROUPS, S5_STATE), 0.01)
    s5_a_im = jnp.pi * n + nrm((L, S5_GROUPS, S5_STATE), 0.01)
    s5_b_re = nrm((L, S5_GROUPS, S5_STATE, S5_GROUP), (2 * S5_GROUP) ** -0.5)
    s5_b_im = nrm((L, S5_GROUPS, S5_STATE, S5_GROUP), (2 * S5_GROUP) ** -0.5)
    s5_c_re = nrm((L, S5_GROUPS, S5_GROUP, S5_STATE), S5_STATE ** -0.5)
    s5_c_im = nrm((L, S5_GROUPS, S5_GROUP, S5_STATE), S5_STATE ** -0.5)
    s5_d = nrm((L, S5_WIDTH), 1.0)
    s5_log_dt = unif((L, S5_GROUPS), math.log(1e-3), math.log(1e-1))
    s5_glu_w = nrm((L, S5_WIDTH, S5_WIDTH), S5_WIDTH ** -0.5)
    s5_glu_b = nrm((L, S5_WIDTH), 0.01)
    s5_w_out = nrm((L, S5_WIDTH, D), S5_WIDTH ** -0.5)

    gdn_conv_w = nrm((L, CONV_W, QKV_WIDTH), CONV_W ** -0.5)
    gdn_a_log = jnp.log(unif((L, GDN_HEADS), 1.0, 16.0))
    dt0 = jnp.exp(unif((L, GDN_HEADS), math.log(1e-3), math.log(1e-1)))
    gdn_dt_bias = dt0 + jnp.log(-jnp.expm1(-dt0))
    gdn_norm_w = 1.0 + nrm((L, GDN_DV), 0.01)
    gdn_w_out = nrm((L, GDN_V, D), GDN_V ** -0.5)

    w_out = nrm((L, D, D), D ** -0.5)
    ffn_w_gate = nrm((L, D, D_FF), D ** -0.5)
    ffn_w_up = nrm((L, D, D_FF), D ** -0.5)
    ffn_w_down = nrm((L, D_FF, D), D_FF ** -0.5)

    return {
        "x_prompt": x_prompt, "x_sample": x_sample,
        "state_rg_conv": state_rg_conv, "state_rg_h": state_rg_h,
        "state_s5_re": state_s5_re, "state_s5_im": state_s5_im,
        "state_gdn_conv": state_gdn_conv, "state_gdn_s": state_gdn_s,
        "norm_pre_mix": norm_pre_mix, "norm_post_mix": norm_post_mix,
        "norm_pre_ffn": norm_pre_ffn, "norm_post_ffn": norm_post_ffn, "w_in": w_in,
        "rg_conv_w": rg_conv_w, "rg_conv_b": rg_conv_b, "rg_wa": rg_wa, "rg_ba": rg_ba,
        "rg_wx": rg_wx, "rg_bx": rg_bx, "rg_lambda": rg_lambda, "rg_w_out": rg_w_out,
        "s5_a_re": s5_a_re, "s5_a_im": s5_a_im, "s5_b_re": s5_b_re, "s5_b_im": s5_b_im,
        "s5_c_re": s5_c_re, "s5_c_im": s5_c_im, "s5_d": s5_d, "s5_log_dt": s5_log_dt,
        "s5_glu_w": s5_glu_w, "s5_glu_b": s5_glu_b, "s5_w_out": s5_w_out,
        "gdn_conv_w": gdn_conv_w, "gdn_a_log": gdn_a_log, "gdn_dt_bias": gdn_dt_bias,
        "gdn_norm_w": gdn_norm_w, "gdn_w_out": gdn_w_out,
        "w_out": w_out, "ffn_w_gate": ffn_w_gate, "ffn_w_up": ffn_w_up, "ffn_w_down": ffn_w_down,
    }


def reference(x_prompt, x_sample, state_rg_conv, state_rg_h, state_s5_re, state_s5_im, state_gdn_conv, state_gdn_s,
              norm_pre_mix, norm_post_mix, norm_pre_ffn, norm_post_ffn, w_in,
              rg_conv_w, rg_conv_b, rg_wa, rg_ba, rg_wx, rg_bx, rg_lambda, rg_w_out,
              s5_a_re, s5_a_im, s5_b_re, s5_b_im, s5_c_re, s5_c_im, s5_d, s5_log_dt, s5_glu_w, s5_glu_b, s5_w_out,
              gdn_conv_w, gdn_a_log, gdn_dt_bias, gdn_norm_w, gdn_w_out,
              w_out, ffn_w_gate, ffn_w_up, ffn_w_down):
    p = {
        "norm_pre_mix": norm_pre_mix, "norm_post_mix": norm_post_mix,
        "norm_pre_ffn": norm_pre_ffn, "norm_post_ffn": norm_post_ffn, "w_in": w_in,
        "rg_conv_w": rg_conv_w, "rg_conv_b": rg_conv_b, "rg_wa": rg_wa, "rg_ba": rg_ba,
        "rg_wx": rg_wx, "rg_bx": rg_bx, "rg_lambda": rg_lambda, "rg_w_out": rg_w_out,
        "s5_a_re": s5_a_re, "s5_a_im": s5_a_im, "s5_b_re": s5_b_re, "s5_b_im": s5_b_im,
        "s5_c_re": s5_c_re, "s5_c_im": s5_c_im, "s5_d": s5_d, "s5_log_dt": s5_log_dt,
        "s5_glu_w": s5_glu_w, "s5_glu_b": s5_glu_b, "s5_w_out": s5_w_out,
        "gdn_conv_w": gdn_conv_w, "gdn_a_log": gdn_a_log, "gdn_dt_bias": gdn_dt_bias,
        "gdn_norm_w": gdn_norm_w, "gdn_w_out": gdn_w_out,
        "w_out": w_out, "ffn_w_gate": ffn_w_gate, "ffn_w_up": ffn_w_up, "ffn_w_down": ffn_w_down,
    }
    y_prompt, (p_rg_conv, p_rg_h, p_s5_re, p_s5_im, p_gdn_conv, p_gdn_s) = run_stack(
        x_prompt, zero_states(x_prompt.shape[0], x_prompt.dtype), p)
    y_sample, (s_rg_conv, s_rg_h, s_s5_re, s_s5_im, s_gdn_conv, s_gdn_s) = run_stack(
        x_sample, (state_rg_conv, state_rg_h, state_s5_re, state_s5_im, state_gdn_conv, state_gdn_s), p)
    return (y_prompt, y_sample,
            p_rg_conv, p_rg_h, p_s5_re, p_s5_im, p_gdn_conv, p_gdn_s,
            s_rg_conv, s_rg_h, s_s5_re, s_s5_im, s_gdn_conv, s_gdn_s)
```

```python
import functools

import jax
import jax.numpy as jnp
from jax import lax
from jax.experimental import pallas as pl
from jax.experimental.pallas import tpu as pltpu

F32 = jnp.float32
BF16 = jnp.bfloat16
HIGHEST = lax.Precision.HIGHEST

D_MODEL = 1024
CONV_W = 4
EPS = 1e-6
RG_WIDTH = 512
RG_BLOCKS = 8
RG_BLOCK = 64
RG_C = 8.0
S5_WIDTH = 512
S5_GROUP = 16
S5_GROUPS = 32
S5_STATE = 64
S5_FLAT = S5_GROUPS * S5_STATE
GDN_HEADS = 4
GDN_DK = 128
GDN_DV = 128
GDN_QK = 512
GDN_V = 512
QKV_WIDTH = 1536
D_FF = 2816

LANE = 128
SUBLANE = 8

COL_GATES = 0
COL_RG = 3072
COL_S5 = 4096
COL_QKV = 4608
COL_Z = 6144
N_MAIN = 6656
N_AB = LANE

VMEM_LIMIT = 52 * 1024 * 1024


def _cparams(sem):
    return pltpu.CompilerParams(dimension_semantics=sem, vmem_limit_bytes=VMEM_LIMIT)


def _rms(x, g):
    return x * lax.rsqrt(jnp.mean(x * x, axis=-1, keepdims=True) + EPS) * g


def _dot(a, b, precision=None):
    return jnp.dot(a, b, preferred_element_type=F32, precision=precision)


def _dot_nt(a, b, precision=None):
    return lax.dot_general(a, b, (((1,), (1,)), ((), ())), preferred_element_type=F32, precision=precision)


def _dot_tn(a, b, precision=None):
    return lax.dot_general(a, b, (((0,), (0,)), ((), ())), preferred_element_type=F32, precision=precision)


def _in_proj_kernel(x_ref, g_ref, w_ref, wab_ref, o_ref, oab_ref, h_ref):
    @pl.when(pl.program_id(1) == 0)
    def _():
        h = _rms(x_ref[...], g_ref[...]).astype(BF16)
        h_ref[...] = h
        oab_ref[...] = _dot(h, wab_ref[...])

    o_ref[...] = _dot(h_ref[...], w_ref[...])


def _in_proj(x, g, w, wab, *, tm=1024, tn=1664):
    m = x.shape[0]
    tm = min(tm, m)
    return pl.pallas_call(
        _in_proj_kernel,
        grid=(m // tm, N_MAIN // tn),
        in_specs=[
            pl.BlockSpec((tm, D_MODEL), lambda i, j: (i, 0)),
            pl.BlockSpec((1, D_MODEL), lambda i, j: (0, 0)),
            pl.BlockSpec((D_MODEL, tn), lambda i, j: (0, j)),
            pl.BlockSpec((D_MODEL, N_AB), lambda i, j: (0, 0)),
        ],
        out_specs=[
            pl.BlockSpec((tm, tn), lambda i, j: (i, j)),
            pl.BlockSpec((tm, N_AB), lambda i, j: (i, 0)),
        ],
        out_shape=[jax.ShapeDtypeStruct((m, N_MAIN), F32), jax.ShapeDtypeStruct((m, N_AB), F32)],
        scratch_shapes=[pltpu.VMEM((tm, D_MODEL), BF16)],
        compiler_params=_cparams(("parallel", "arbitrary")),
        name="in_proj",
    )(x, g, w, wab)


def _rglru_kernel(p_ref, conv0_ref, h0_ref, cw_ref, cb_ref, wg_ref, ba_ref, bx_ref, lam_ref,
                  y_ref, hl_ref, xp_ref, a_ref, h_ref, hc_ref, *, nb, tc):
    rows = tc * nb
    halo = (CONV_W - 1) * nb

    @pl.when(pl.program_id(0) == 0)
    def _():
        xp_ref[0:halo, :] = conv0_ref[...]
        hc_ref[...] = h0_ref[...]

    xp_ref[halo:halo + rows, :] = p_ref[:, 0:RG_WIDTH]
    cw = cw_ref[...]
    xc = cb_ref[...] + xp_ref[0:rows, :] * cw[0:1, :]
    for k in range(1, CONV_W):
        xc = xc + xp_ref[k * nb:k * nb + rows, :] * cw[k:k + 1, :]
    xp_ref[0:halo, :] = xp_ref[rows:rows + halo, :]

    xcb = xc.astype(BF16)
    for p in range(RG_WIDTH // LANE):
        cols = slice(p * LANE, (p + 1) * LANE)
        gm = _dot(xcb[:, cols], wg_ref[p])
        r = jax.nn.sigmoid(gm[:, 0:LANE] + ba_ref[:, cols])
        i = jax.nn.sigmoid(gm[:, LANE:2 * LANE] + bx_ref[:, cols])
        log_a = -RG_C * r * jax.nn.softplus(-lam_ref[:, cols])
        a_ref[:, cols] = jnp.exp(log_a)
        h_ref[:, cols] = jnp.sqrt(1.0 - jnp.exp(2.0 * log_a)) * (i * xc[:, cols])

    def step(t, h):
        rs = pl.ds(pl.multiple_of(t * nb, nb), nb)
        h = a_ref[rs, :] * h + h_ref[rs, :]
        h_ref[rs, :] = h
        return h

    h = lax.fori_loop(0, tc, step, hc_ref[...])
    hc_ref[...] = h
    hl_ref[...] = h
    y_ref[...] = (h_ref[...] * jax.nn.gelu(p_ref[:, RG_WIDTH:2 * RG_WIDTH])).astype(BF16)


def _rglru(proj, conv0, h0, cw, cb, wg, ba, bx, lam, *, nb, nt, tc):
    rows = tc * nb
    halo = (CONV_W - 1) * nb
    const = lambda c: (0, 0)
    return pl.pallas_call(
        functools.partial(_rglru_kernel, nb=nb, tc=tc),
        grid=(nt // tc,),
        in_specs=[
            pl.BlockSpec((rows, 2 * RG_WIDTH), lambda c: (c, COL_RG // (2 * RG_WIDTH))),
            pl.BlockSpec((halo, RG_WIDTH), const),
            pl.BlockSpec((nb, RG_WIDTH), const),
            pl.BlockSpec((CONV_W, RG_WIDTH), const),
            pl.BlockSpec((1, RG_WIDTH), const),
            pl.BlockSpec((RG_WIDTH // LANE, LANE, 2 * LANE), lambda c: (0, 0, 0)),
            pl.BlockSpec((1, RG_WIDTH), const),
            pl.BlockSpec((1, RG_WIDTH), const),
            pl.BlockSpec((1, RG_WIDTH), const),
        ],
        out_specs=[
            pl.BlockSpec((rows, RG_WIDTH), lambda c: (c, 0)),
            pl.BlockSpec((nb, RG_WIDTH), const),
        ],
        out_shape=[jax.ShapeDtypeStruct((nt * nb, RG_WIDTH), BF16), jax.ShapeDtypeStruct((nb, RG_WIDTH), F32)],
        scratch_shapes=[
            pltpu.VMEM((rows + halo, RG_WIDTH), F32),
            pltpu.VMEM((rows, RG_WIDTH), F32),
            pltpu.VMEM((rows, RG_WIDTH), F32),
            pltpu.VMEM((nb, RG_WIDTH), F32),
        ],
        compiler_params=_cparams(("arbitrary",)),
        name="rglru",
    )(proj, conv0, h0, cw, cb, wg, ba, bx, lam)


def _s5_kernel(u_ref, s0r_ref, s0i_ref, lr_ref, li_ref, bre_ref, bim_ref, cre_ref, cim_ref, d_ref, gw_ref, gb_ref,
               y_ref, sr_out, si_out, sre_ref, sim_ref, src_ref, sic_ref, *, nb, tc, lane_blk):
    nq = S5_WIDTH // LANE
    qs = S5_FLAT // nq

    @pl.when(pl.program_id(0) == 0)
    def _():
        src_ref[...] = s0r_ref[...]
        sic_ref[...] = s0i_ref[...]

    u = u_ref[...]
    ub = u.astype(BF16)
    for q in range(nq):
        uq = ub[:, q * LANE:(q + 1) * LANE]
        sre_ref[:, q * qs:(q + 1) * qs] = _dot(uq, bre_ref[q])
        sim_ref[:, q * qs:(q + 1) * qs] = _dot(uq, bim_ref[q])

    for lb in range(S5_FLAT // lane_blk):
        cols = slice(lb * lane_blk, (lb + 1) * lane_blk)
        lr = jnp.broadcast_to(lr_ref[:, cols], (nb, lane_blk))
        li = jnp.broadcast_to(li_ref[:, cols], (nb, lane_blk))

        def step(t, carry, cols=cols, lr=lr, li=li):
            sr, si = carry
            rs = pl.ds(pl.multiple_of(t * nb, nb), nb)
            nr = lr * sr - li * si + sre_ref[rs, cols]
            ni = lr * si + li * sr + sim_ref[rs, cols]
            sre_ref[rs, cols] = nr
            sim_ref[rs, cols] = ni
            return nr, ni

        sr, si = lax.fori_loop(0, tc, step, (src_ref[:, cols], sic_ref[:, cols]))
        src_ref[:, cols] = sr
        sic_ref[:, cols] = si

    sr_out[...] = src_ref[...]
    si_out[...] = sic_ref[...]

    ys = []
    for q in range(nq):
        sr_q = sre_ref[:, q * qs:(q + 1) * qs].astype(BF16)
        si_q = sim_ref[:, q * qs:(q + 1) * qs].astype(BF16)
        ys.append(_dot(sr_q, cre_ref[q]) - _dot(si_q, cim_ref[q]))
    y = jnp.concatenate(ys, axis=1) + d_ref[...] * u
    y = jax.nn.gelu(y)
    y = y * jax.nn.sigmoid(_dot(y.astype(BF16), gw_ref[...]) + gb_ref[...])
    y_ref[...] = y.astype(BF16)


def _s5(proj, s0r, s0i, lr, li, bre, bim, cre, cim, d, gw, gb, *, nb, nt, tc, lane_blk):
    rows = tc * nb
    nq = S5_WIDTH // LANE
    qs = S5_FLAT // nq
    const = lambda c: (0, 0)
    const3 = lambda c: (0, 0, 0)
    return pl.pallas_call(
        functools.partial(_s5_kernel, nb=nb, tc=tc, lane_blk=lane_blk),
        grid=(nt // tc,),
        in_specs=[
            pl.BlockSpec((rows, S5_WIDTH), lambda c: (c, COL_S5 // S5_WIDTH)),
            pl.BlockSpec((nb, S5_FLAT), const),
            pl.BlockSpec((nb, S5_FLAT), const),
            pl.BlockSpec((1, S5_FLAT), const),
            pl.BlockSpec((1, S5_FLAT), const),
            pl.BlockSpec((nq, LANE, qs), const3),
            pl.BlockSpec((nq, LANE, qs), const3),
            pl.BlockSpec((nq, qs, LANE), const3),
            pl.BlockSpec((nq, qs, LANE), const3),
            pl.BlockSpec((1, S5_WIDTH), const),
            pl.BlockSpec((S5_WIDTH, S5_WIDTH), const),
            pl.BlockSpec((1, S5_WIDTH), const),
        ],
        out_specs=[
            pl.BlockSpec((rows, S5_WIDTH), lambda c: (c, 0)),
            pl.BlockSpec((nb, S5_FLAT), const),
            pl.BlockSpec((nb, S5_FLAT), const),
        ],
        out_shape=[
            jax.ShapeDtypeStruct((nt * nb, S5_WIDTH), BF16),
            jax.ShapeDtypeStruct((nb, S5_FLAT), F32),
            jax.ShapeDtypeStruct((nb, S5_FLAT), F32),
        ],
        scratch_shapes=[
            pltpu.VMEM((rows, S5_FLAT), F32),
            pltpu.VMEM((rows, S5_FLAT), F32),
            pltpu.VMEM((nb, S5_FLAT), F32),
            pltpu.VMEM((nb, S5_FLAT), F32),
        ],
        compiler_params=_cparams(("arbitrary",)),
        name="s5",
    )(proj, s0r, s0i, lr, li, bre, bim, cre, cim, d, gw, gb)


def _unit_lower_inverse(lmat, eye, ck):
    x = eye - lmat
    p = lmat
    n = 2
    while n < ck:
        p = _dot(p, p, HIGHEST)
        x = x + _dot(x, p, HIGHEST)
        n *= 2
    return x


def _gdn_kernel(q_ref, k_ref, v_ref, z_ref, ab_ref, conv0_ref, s0_ref, cw_ref, alog_ref, dtb_ref, nw_ref,
                o_ref, sout_ref, xb_ref, s_ref, *, ck):
    c = pl.program_id(1)

    @pl.when(c == 0)
    def _():
        xb_ref[0:SUBLANE, :] = conv0_ref[...]
        s_ref[...] = s0_ref[...]

    xb_ref[SUBLANE:SUBLANE + ck, 0:GDN_QK] = q_ref[...]
    xb_ref[SUBLANE:SUBLANE + ck, GDN_QK:2 * GDN_QK] = k_ref[...]
    xb_ref[SUBLANE:SUBLANE + ck, 2 * GDN_QK:QKV_WIDTH] = v_ref[...]
    cw = cw_ref[...]
    first = SUBLANE - (CONV_W - 1)
    xc = xb_ref[first:first + ck, :] * cw[0:1, :]
    for j in range(1, CONV_W):
        xc = xc + xb_ref[first + j:first + j + ck, :] * cw[j:j + 1, :]
    xb_ref[0:SUBLANE, :] = xb_ref[ck:ck + SUBLANE, :]
    xc = xc * jax.nn.sigmoid(xc)

    ab = ab_ref[...]
    g_all = -jnp.exp(alog_ref[...]) * jax.nn.softplus(ab + dtb_ref[...])
    beta_all = jax.nn.sigmoid(ab)
    row = lax.broadcasted_iota(jnp.int32, (ck, ck), 0)
    col = lax.broadcasted_iota(jnp.int32, (ck, ck), 1)
    causal = row >= col
    strict = row > col
    eye = jnp.where(row == col, 1.0, 0.0).astype(F32)
    gc_all = _dot(jnp.where(causal, 1.0, 0.0).astype(F32), g_all, HIGHEST)
    avg = jnp.full((ck, LANE), 1.0 / LANE, F32)
    z = z_ref[...]
    nw = nw_ref[...]

    for h in range(GDN_HEADS):
        cols = slice(h * GDN_DK, (h + 1) * GDN_DK)
        gc = gc_all[:, h:h + 1]
        beta = beta_all[:, GDN_HEADS + h:GDN_HEADS + h + 1]
        q = xc[:, cols]
        k = xc[:, GDN_QK + h * GDN_DK:GDN_QK + (h + 1) * GDN_DK]
        v = xc[:, 2 * GDN_QK + h * GDN_DV:2 * GDN_QK + (h + 1) * GDN_DV]
        q = q * lax.rsqrt(jnp.sum(q * q, axis=-1, keepdims=True) + EPS) * (GDN_DK ** -0.5)
        k = k * lax.rsqrt(jnp.sum(k * k, axis=-1, keepdims=True) + EPS)
        kb = k * beta
        vb = v * beta
        gcb = jnp.broadcast_to(gc, (ck, LANE))
        gc_row = _dot_nt(avg, gcb, HIGHEST)
        diff = jnp.broadcast_to(gc, (ck, ck)) - gc_row
        decay = jnp.where(causal, jnp.exp(jnp.where(causal, diff, 0.0)), 0.0)
        kbf = k.astype(BF16)
        lmat = jnp.where(strict, _dot_nt(kb.astype(BF16), kbf) * decay, 0.0)
        tmat = _unit_lower_inverse(lmat, eye, ck).astype(BF16)
        egc = jnp.exp(gcb)
        u = _dot(tmat, vb.astype(BF16))
        w = _dot(tmat, (kb * egc).astype(BF16))
        qk = jnp.where(causal, _dot_nt(q.astype(BF16), kbf) * decay, 0.0)

        s = s_ref[h]
        sb = s.astype(BF16)
        v_new = u - _dot(w.astype(BF16), sb)
        vnb = v_new.astype(BF16)
        o = _dot((q * egc).astype(BF16), sb) + _dot(qk.astype(BF16), vnb)
        g_last = gcb[ck - 1:ck, :]
        k_dec = k * jnp.exp(g_last - gcb)
        s_ref[h] = s * jnp.exp(g_last) + _dot_tn(k_dec.astype(BF16), vnb)

        zh = z[:, h * GDN_DV:(h + 1) * GDN_DV]
        o = _rms(o, nw) * (zh * jax.nn.sigmoid(zh))
        o_ref[:, h * GDN_DV:(h + 1) * GDN_DV] = o.astype(BF16)

    @pl.when(c == pl.num_programs(1) - 1)
    def _():
        sout_ref[...] = s_ref[...]


def _gdn(proj_b, ab_b, conv0, s0, cw, alog, dtb, nw, *, nb, nt, ck):
    per_seq = N_MAIN // GDN_QK
    qblk = COL_QKV // GDN_QK
    const = lambda b, c: (0, 0)
    return pl.pallas_call(
        functools.partial(_gdn_kernel, ck=ck),
        grid=(nb, nt // ck),
        in_specs=[
            pl.BlockSpec((ck, GDN_QK), lambda b, c: (c, b * per_seq + qblk)),
            pl.BlockSpec((ck, GDN_QK), lambda b, c: (c, b * per_seq + qblk + 1)),
            pl.BlockSpec((ck, GDN_V), lambda b, c: (c, b * per_seq + qblk + 2)),
            pl.BlockSpec((ck, GDN_V), lambda b, c: (c, b * per_seq + COL_Z // GDN_V)),
            pl.BlockSpec((ck, N_AB), lambda b, c: (c, b)),
            pl.BlockSpec((None, SUBLANE, QKV_WIDTH), lambda b, c: (b, 0, 0)),
            pl.BlockSpec((None, GDN_HEADS, GDN_DK, GDN_DV), lambda b, c: (b, 0, 0, 0)),
            pl.BlockSpec((CONV_W, QKV_WIDTH), const),
            pl.BlockSpec((1, LANE), const),
            pl.BlockSpec((1, LANE), const),
            pl.BlockSpec((1, GDN_DV), const),
        ],
        out_specs=[
            pl.BlockSpec((ck, GDN_V), lambda b, c: (c, b)),
            pl.BlockSpec((None, GDN_HEADS, GDN_DK, GDN_DV), lambda b, c: (b, 0, 0, 0)),
        ],
        out_shape=[
            jax.ShapeDtypeStruct((nt, nb * GDN_V), BF16),
            jax.ShapeDtypeStruct((nb, GDN_HEADS, GDN_DK, GDN_DV), F32),
        ],
        scratch_shapes=[
            pltpu.VMEM((ck + SUBLANE, QKV_WIDTH), F32),
            pltpu.VMEM((GDN_HEADS, GDN_DK, GDN_DV), F32),
        ],
        compiler_params=_cparams(("parallel", "arbitrary")),
        name="gdn",
    )(proj_b, proj_b, proj_b, proj_b, ab_b, conv0, s0, cw, alog, dtb, nw)


def _merge_kernel(x_ref, yrg_ref, ys5_ref, ygd_ref, grg_ref, gs5_ref, ggd_ref, wrg_ref, ws5_ref, wgd_ref,
                  wo_ref, g_ref, o_ref):
    mixed = jax.nn.sigmoid(grg_ref[...]) * _dot(yrg_ref[...], wrg_ref[...])
    mixed = mixed + jax.nn.sigmoid(gs5_ref[...]) * _dot(ys5_ref[...], ws5_ref[...])
    mixed = mixed + jax.nn.sigmoid(ggd_ref[...]) * _dot(ygd_ref[...], wgd_ref[...])
    out = _dot(mixed.astype(BF16), wo_ref[...])
    o_ref[...] = x_ref[...] + _rms(out, g_ref[...])


def _merge(x, yrg, ys5, ygd, proj, wrg, ws5, wgd, wo, g, *, tm=512):
    m = x.shape[0]
    tm = min(tm, m)
    const = lambda i: (0, 0)
    half = D_MODEL // 2
    return pl.pallas_call(
        _merge_kernel,
        grid=(m // tm,),
        in_specs=[
            pl.BlockSpec((tm, D_MODEL), lambda i: (i, 0)),
            pl.BlockSpec((tm, half), lambda i: (i, 0)),
            pl.BlockSpec((tm, half), lambda i: (i, 0)),
            pl.BlockSpec((tm, half), lambda i: (i, 0)),
            pl.BlockSpec((tm, D_MODEL), lambda i: (i, 0)),
            pl.BlockSpec((tm, D_MODEL), lambda i: (i, 1)),
            pl.BlockSpec((tm, D_MODEL), lambda i: (i, 2)),
            pl.BlockSpec((half, D_MODEL), const),
            pl.BlockSpec((half, D_MODEL), const),
            pl.BlockSpec((half, D_MODEL), const),
            pl.BlockSpec((D_MODEL, D_MODEL), const),
            pl.BlockSpec((1, D_MODEL), const),
        ],
        out_specs=pl.BlockSpec((tm, D_MODEL), lambda i: (i, 0)),
        out_shape=jax.ShapeDtypeStruct((m, D_MODEL), F32),
        compiler_params=_cparams(("parallel",)),
        name="merge",
    )(x, yrg, ys5, ygd, proj, proj, proj, wrg, ws5, wgd, wo, g)


def _ffn_kernel(x_ref, gpre_ref, wg_ref, wu_ref, wd_ref, gpost_ref, o_ref, h_ref, acc_ref):
    j = pl.program_id(1)

    @pl.when(j == 0)
    def _():
        h_ref[...] = _rms(x_ref[...], gpre_ref[...]).astype(BF16)
        acc_ref[...] = jnp.zeros_like(acc_ref)

    h = h_ref[...]
    gate = _dot(h, wg_ref[...])
    act = (gate * jax.nn.sigmoid(gate)) * _dot(h, wu_ref[...])
    acc_ref[...] += _dot(act.astype(BF16), wd_ref[...])

    @pl.when(j == pl.num_programs(1) - 1)
    def _():
        o_ref[...] = x_ref[...] + _rms(acc_ref[...], gpost_ref[...])


def _ffn(x, gpre, wg, wu, wd, gpost, *, tm=512, tf=1408):
    m = x.shape[0]
    tm = min(tm, m)
    const = lambda i, j: (0, 0)
    return pl.pallas_call(
        _ffn_kernel,
        grid=(m // tm, D_FF // tf),
        in_specs=[
            pl.BlockSpec((tm, D_MODEL), lambda i, j: (i, 0)),
            pl.BlockSpec((1, D_MODEL), const),
            pl.BlockSpec((D_MODEL, tf), lambda i, j: (0, j)),
            pl.BlockSpec((D_MODEL, tf), lambda i, j: (0, j)),
            pl.BlockSpec((tf, D_MODEL), lambda i, j: (j, 0)),
            pl.BlockSpec((1, D_MODEL), const),
        ],
        out_specs=pl.BlockSpec((tm, D_MODEL), lambda i, j: (i, 0)),
        out_shape=jax.ShapeDtypeStruct((m, D_MODEL), F32),
        scratch_shapes=[pltpu.VMEM((tm, D_MODEL), BF16), pltpu.VMEM((tm, D_MODEL), F32)],
        compiler_params=_cparams(("parallel", "arbitrary")),
        name="ffn",
    )(x, gpre, wg, wu, wd, gpost)


def _block_diag(blocks):
    n, r, c = blocks.shape
    eye = jnp.eye(n, dtype=blocks.dtype)
    return jnp.einsum("nrc,nm->nrmc", blocks, eye).reshape(n * r, n * c)


def _prep_layer(l, p):
    f32 = F32
    w_in = p["w_in"][l]
    sizes = (RG_WIDTH, RG_WIDTH, S5_WIDTH, QKV_WIDTH, GDN_V, GDN_HEADS, GDN_HEADS, 3 * D_MODEL)
    offs = [0]
    for s in sizes:
        offs.append(offs[-1] + s)
    seg = lambda i: w_in[:, offs[i]:offs[i + 1]]
    w_main = jnp.concatenate([seg(7), seg(0), seg(1), seg(2), seg(3), seg(4)], axis=1).astype(BF16)
    w_ab = jnp.concatenate(
        [seg(5), seg(6), jnp.zeros((D_MODEL, N_AB - 2 * GDN_HEADS), f32)], axis=1).astype(BF16)

    pair = LANE // RG_BLOCK
    wa = p["rg_wa"][l].reshape(RG_WIDTH // LANE, pair, RG_BLOCK, RG_BLOCK)
    wx = p["rg_wx"][l].reshape(RG_WIDTH // LANE, pair, RG_BLOCK, RG_BLOCK)
    wg = jnp.stack([jnp.concatenate([_block_diag(wa[i]), _block_diag(wx[i])], axis=1)
                    for i in range(RG_WIDTH // LANE)]).astype(BF16)

    lam = lax.complex(p["s5_a_re"][l].astype(f32), p["s5_a_im"][l].astype(f32))
    dt = jnp.exp(p["s5_log_dt"][l].astype(f32))[:, None]
    lam_bar = jnp.exp(lam * dt)
    b_bar = ((lam_bar - 1.0) / lam)[..., None] * lax.complex(p["s5_b_re"][l].astype(f32), p["s5_b_im"][l].astype(f32))
    nq = S5_WIDTH // LANE
    gq = S5_GROUPS // nq
    bt = jnp.swapaxes(b_bar, 1, 2).reshape(nq, gq, S5_GROUP, S5_STATE)
    bre = jnp.stack([_block_diag(jnp.real(bt[i])) for i in range(nq)]).astype(BF16)
    bim = jnp.stack([_block_diag(jnp.imag(bt[i])) for i in range(nq)]).astype(BF16)
    cre_g = jnp.swapaxes(p["s5_c_re"][l].astype(f32), 1, 2).reshape(nq, gq, S5_STATE, S5_GROUP)
    cim_g = jnp.swapaxes(p["s5_c_im"][l].astype(f32), 1, 2).reshape(nq, gq, S5_STATE, S5_GROUP)
    cre = jnp.stack([_block_diag(cre_g[i]) for i in range(nq)]).astype(BF16)
    cim = jnp.stack([_block_diag(cim_g[i]) for i in range(nq)]).astype(BF16)

    pad4 = lambda a: jnp.pad(a.astype(f32), (0, LANE - GDN_HEADS)).reshape(1, LANE)
    row = lambda a: a.astype(f32).reshape(1, -1)
    return dict(
        g_pre_mix=row(p["norm_pre_mix"][l]), g_post_mix=row(p["norm_post_mix"][l]),
        g_pre_ffn=row(p["norm_pre_ffn"][l]), g_post_ffn=row(p["norm_post_ffn"][l]),
        w_main=w_main, w_ab=w_ab,
        rg_cw=p["rg_conv_w"][l].astype(f32), rg_cb=row(p["rg_conv_b"][l]), rg_wg=wg,
        rg_ba=row(p["rg_ba"][l]), rg_bx=row(p["rg_bx"][l]), rg_lam=row(p["rg_lambda"][l]),
        rg_wo=p["rg_w_out"][l].astype(BF16),
        s5_lr=jnp.real(lam_bar).reshape(1, S5_FLAT), s5_li=jnp.imag(lam_bar).reshape(1, S5_FLAT),
        s5_bre=bre, s5_bim=bim, s5_cre=cre, s5_cim=cim, s5_d=row(p["s5_d"][l]),
        s5_gw=p["s5_glu_w"][l].astype(BF16), s5_gb=row(p["s5_glu_b"][l]), s5_wo=p["s5_w_out"][l].astype(BF16),
        gd_cw=p["gdn_conv_w"][l].astype(f32), gd_alog=pad4(p["gdn_a_log"][l]), gd_dtb=pad4(p["gdn_dt_bias"][l]),
        gd_nw=row(p["gdn_norm_w"][l]), gd_wo=p["gdn_w_out"][l].astype(BF16),
        w_out=p["w_out"][l].astype(BF16),
        ffn_wg=p["ffn_w_gate"][l].astype(BF16), ffn_wu=p["ffn_w_up"][l].astype(BF16),
        ffn_wd=p["ffn_w_down"][l].astype(BF16),
    )


def _tail_rows(a, nb):
    keep = CONV_W - 1
    return jnp.swapaxes(a[-keep * nb:].reshape(keep, nb, a.shape[-1]), 0, 1)


def _run_group(x, states, layers, *, tc, ck, lane_blk):
    nb, nt, _ = x.shape
    assert nt >= CONV_W - 1 and nb % SUBLANE == 0
    st_rg_conv, st_rg_h, st_s5_re, st_s5_im, st_gd_conv, st_gd_s = states
    xf = jnp.swapaxes(x, 0, 1).reshape(nt * nb, D_MODEL)
    outs = [[] for _ in range(6)]
    keep = CONV_W - 1
    for l, w in enumerate(layers):
        proj, ab = _in_proj(xf, w["g_pre_mix"], w["w_main"], w["w_ab"])

        conv0 = jnp.swapaxes(st_rg_conv[l], 0, 1).reshape(keep * nb, RG_WIDTH)
        y_rg, h_last = _rglru(proj, conv0, st_rg_h[l], w["rg_cw"], w["rg_cb"], w["rg_wg"], w["rg_ba"], w["rg_bx"],
                              w["rg_lam"], nb=nb, nt=nt, tc=tc)

        y_s5, s_re, s_im = _s5(proj, st_s5_re[l].reshape(nb, S5_FLAT), st_s5_im[l].reshape(nb, S5_FLAT),
                               w["s5_lr"], w["s5_li"], w["s5_bre"], w["s5_bim"], w["s5_cre"], w["s5_cim"],
                               w["s5_d"], w["s5_gw"], w["s5_gb"], nb=nb, nt=nt, tc=tc, lane_blk=lane_blk)

        gconv0 = jnp.pad(st_gd_conv[l], ((0, 0), (SUBLANE - keep, 0), (0, 0)))
        y_gd, s_gd = _gdn(proj.reshape(nt, nb * N_MAIN), ab.reshape(nt, nb * N_AB), gconv0, st_gd_s[l],
                          w["gd_cw"], w["gd_alog"], w["gd_dtb"], w["gd_nw"], nb=nb, nt=nt, ck=ck)

        x1 = _merge(xf, y_rg, y_s5, y_gd.reshape(nt * nb, GDN_V), proj, w["rg_wo"], w["s5_wo"], w["gd_wo"],
                    w["w_out"], w["g_post_mix"])
        xf = _ffn(x1, w["g_pre_ffn"], w["ffn_wg"], w["ffn_wu"], w["ffn_wd"], w["g_post_ffn"])

        outs[0].append(_tail_rows(proj[:, COL_RG:COL_RG + RG_WIDTH], nb))
        outs[1].append(h_last)
        outs[2].append(s_re.reshape(nb, S5_GROUPS, S5_STATE))
        outs[3].append(s_im.reshape(nb, S5_GROUPS, S5_STATE))
        outs[4].append(_tail_rows(proj[:, COL_QKV:COL_QKV + QKV_WIDTH], nb))
        outs[5].append(s_gd)
    y = jnp.swapaxes(xf.reshape(nt, nb, D_MODEL), 0, 1)
    return y, [jnp.stack(o) for o in outs]


def _zero_states(depth, nb):
    return (jnp.zeros((depth, nb, CONV_W - 1, RG_WIDTH), F32),
            jnp.zeros((depth, nb, RG_WIDTH), F32),
            jnp.zeros((depth, nb, S5_GROUPS, S5_STATE), F32),
            jnp.zeros((depth, nb, S5_GROUPS, S5_STATE), F32),
            jnp.zeros((depth, nb, CONV_W - 1, QKV_WIDTH), F32),
            jnp.zeros((depth, nb, GDN_HEADS, GDN_DK, GDN_DV), F32))


def kernel(x_prompt, x_sample, state_rg_conv, state_rg_h, state_s5_re, state_s5_im, state_gdn_conv, state_gdn_s, norm_pre_mix, norm_post_mix, norm_pre_ffn, norm_post_ffn, w_in, rg_conv_w, rg_conv_b, rg_wa, rg_ba, rg_wx, rg_bx, rg_lambda, rg_w_out, s5_a_re, s5_a_im, s5_b_re, s5_b_im, s5_c_re, s5_c_im, s5_d, s5_log_dt, s5_glu_w, s5_glu_b, s5_w_out, gdn_conv_w, gdn_a_log, gdn_dt_bias, gdn_norm_w, gdn_w_out, w_out, ffn_w_gate, ffn_w_up, ffn_w_down):
    p = dict(norm_pre_mix=norm_pre_mix, norm_post_mix=norm_post_mix, norm_pre_ffn=norm_pre_ffn,
             norm_post_ffn=norm_post_ffn, w_in=w_in, rg_conv_w=rg_conv_w, rg_conv_b=rg_conv_b, rg_wa=rg_wa,
             rg_ba=rg_ba, rg_wx=rg_wx, rg_bx=rg_bx, rg_lambda=rg_lambda, rg_w_out=rg_w_out, s5_a_re=s5_a_re,
             s5_a_im=s5_a_im, s5_b_re=s5_b_re, s5_b_im=s5_b_im, s5_c_re=s5_c_re, s5_c_im=s5_c_im, s5_d=s5_d,
             s5_log_dt=s5_log_dt, s5_glu_w=s5_glu_w, s5_glu_b=s5_glu_b, s5_w_out=s5_w_out,
             gdn_conv_w=gdn_conv_w, gdn_a_log=gdn_a_log, gdn_dt_bias=gdn_dt_bias, gdn_norm_w=gdn_norm_w,
             gdn_w_out=gdn_w_out, w_out=w_out, ffn_w_gate=ffn_w_gate, ffn_w_up=ffn_w_up, ffn_w_down=ffn_w_down)
    depth = w_in.shape[0]
    layers = [_prep_layer(l, p) for l in range(depth)]
    y_p, st_p = _run_group(x_prompt, _zero_states(depth, x_prompt.shape[0]), layers, tc=32, ck=64, lane_blk=512)
    y_s, st_s = _run_group(
        x_sample, (state_rg_conv, state_rg_h, state_s5_re, state_s5_im, state_gdn_conv, state_gdn_s), layers,
        tc=x_sample.shape[1], ck=x_sample.shape[1], lane_blk=LANE)
    return (y_p, y_s, *st_p, *st_s)
```

```python
import functools

import jax
import jax.numpy as jnp
from jax import lax
from jax.experimental import pallas as pl
from jax.experimental.pallas import tpu as pltpu

F32 = jnp.float32
BF16 = jnp.bfloat16

D_MODEL = 1024
CONV_W = 4
EPS = 1e-6
RG_WIDTH = 512
RG_BLOCK = 64
RG_C = 8.0
S5_WIDTH = 512
S5_GROUP = 16
S5_GROUPS = 32
S5_STATE = 64
S5_FLAT = S5_GROUPS * S5_STATE
GDN_HEADS = 4
GDN_DK = 128
GDN_DV = 128
GDN_QK = 512
GDN_V = 512
QKV_WIDTH = 1536
D_FF = 2816

LANE = 128
SUBLANE = 8
BF16_ROWS = 16
STACK = 256

COL_GATES = 0
COL_RG = 3072
COL_S5 = 4096
COL_QKV = 4608
COL_Z = 6144
N_MAIN = 6656
N_AB = LANE

VMEM_LIMIT = 52 * 1024 * 1024


def _cparams(sem):
    return pltpu.CompilerParams(dimension_semantics=sem, vmem_limit_bytes=VMEM_LIMIT)


def _rms(x, g):
    return x * lax.rsqrt(jnp.mean(x * x, axis=-1, keepdims=True) + EPS) * g


def _dot(a, b):
    return jnp.dot(a, b, preferred_element_type=F32)


def _dot_nt(a, b):
    return lax.dot_general(a, b, (((1,), (1,)), ((), ())), preferred_element_type=F32)


def _dot_tn(a, b):
    return lax.dot_general(a, b, (((0,), (0,)), ((), ())), preferred_element_type=F32)


def _split2(a):
    hi = a.astype(BF16)
    return hi, (a - hi.astype(F32)).astype(BF16)


def _dot_exact_lhs(a, b):
    b1 = b.astype(BF16)
    r = b - b1.astype(F32)
    b2 = r.astype(BF16)
    b3 = (r - b2.astype(F32)).astype(BF16)
    return _dot(a, b1) + (_dot(a, b2) + _dot(a, b3))


def _dot_split(a, b):
    a1, a2 = _split2(a)
    b1, b2 = _split2(b)
    return _dot(a1, b1) + (_dot(a1, b2) + _dot(a2, b1))


def _step_pitch(nt_chunk):
    return nt_chunk if nt_chunk <= SUBLANE else nt_chunk + SUBLANE


def _put_rows(ref, tile, val, nb, tc, pitch):
    if pitch == tc:
        ref[tile, 0:nb * tc, :] = val
    else:
        for b in range(nb):
            ref[tile, b * pitch:b * pitch + tc, :] = val[b * tc:(b + 1) * tc, :]


def _get_rows(ref, tile, nb, tc, pitch):
    if pitch == tc:
        return ref[tile, 0:nb * tc, :]
    return jnp.concatenate([ref[tile, b * pitch:b * pitch + tc, :] for b in range(nb)], axis=0)


def _in_proj_kernel(x_ref, g_ref, w_ref, wab_ref, o_ref, oab_ref, h_ref):
    @pl.when(pl.program_id(1) == 0)
    def _():
        h = _rms(x_ref[...], g_ref[...]).astype(BF16)
        h_ref[...] = h
        oab_ref[...] = _dot(h, wab_ref[...])

    o_ref[...] = _dot(h_ref[...], w_ref[...])


def _in_proj(x, g, w, wab, *, tm=1024, tn=1664):
    m = x.shape[0]
    tm = min(tm, m)
    return pl.pallas_call(
        _in_proj_kernel,
        grid=(m // tm, N_MAIN // tn),
        in_specs=[
            pl.BlockSpec((tm, D_MODEL), lambda i, j: (i, 0)),
            pl.BlockSpec((1, D_MODEL), lambda i, j: (0, 0)),
            pl.BlockSpec((D_MODEL, tn), lambda i, j: (0, j)),
            pl.BlockSpec((D_MODEL, N_AB), lambda i, j: (0, 0)),
        ],
        out_specs=[
            pl.BlockSpec((tm, tn), lambda i, j: (i, j)),
            pl.BlockSpec((tm, N_AB), lambda i, j: (i, 0)),
        ],
        out_shape=[jax.ShapeDtypeStruct((m, N_MAIN), F32), jax.ShapeDtypeStruct((m, N_AB), F32)],
        scratch_shapes=[pltpu.VMEM((tm, D_MODEL), BF16)],
        compiler_params=_cparams(("parallel", "arbitrary")),
        name="in_proj",
    )(x, g, w, wab)


def _rglru_kernel(p_ref, conv0_ref, h0_ref, cw_ref, cb_ref, wg_ref, ba_ref, bx_ref, lam_ref,
                  y_ref, hl_ref, xp_ref, a_ref, h_ref, hc_ref, *, nb, tc, pitch):
    ntile = RG_WIDTH // LANE
    first = SUBLANE - (CONV_W - 1)

    @pl.when(pl.program_id(0) == 0)
    def _():
        xp_ref[:, 0:SUBLANE, :] = conv0_ref[...]
        hc_ref[...] = h0_ref[...]

    xp_ref[:, SUBLANE:SUBLANE + tc, :] = p_ref[:, :, 0:RG_WIDTH]
    cw = cw_ref[...]
    xc = cb_ref[...] + xp_ref[:, first:first + tc, :] * cw[0:1, :]
    for k in range(1, CONV_W):
        xc = xc + xp_ref[:, first + k:first + k + tc, :] * cw[k:k + 1, :]
    xp_ref[:, 0:SUBLANE, :] = xp_ref[:, tc:tc + SUBLANE, :]

    xc = xc.reshape(nb * tc, RG_WIDTH)
    xcb = xc.astype(BF16)
    for p in range(ntile):
        cols = slice(p * LANE, (p + 1) * LANE)
        gm = _dot(xcb[:, cols], wg_ref[p])
        r = jax.nn.sigmoid(gm[:, 0:LANE] + ba_ref[:, cols])
        i = jax.nn.sigmoid(gm[:, LANE:2 * LANE] + bx_ref[:, cols])
        log_a = -RG_C * r * jax.nn.softplus(-lam_ref[:, cols])
        _put_rows(a_ref, p, jnp.exp(log_a), nb, tc, pitch)
        _put_rows(h_ref, p, jnp.sqrt(1.0 - jnp.exp(2.0 * log_a)) * (i * xc[:, cols]), nb, tc, pitch)

    def step(t, hs):
        rs = pl.ds(t, nb, stride=pitch)
        out = []
        for p in range(ntile):
            h = a_ref[p, rs, :] * hs[p] + h_ref[p, rs, :]
            h_ref[p, rs, :] = h
            out.append(h)
        return tuple(out)

    hc = hc_ref[...]
    hs = lax.fori_loop(0, tc, step, tuple(hc[:, p * LANE:(p + 1) * LANE] for p in range(ntile)))
    for p in range(ntile):
        cols = slice(p * LANE, (p + 1) * LANE)
        hc_ref[:, cols] = hs[p]
        hl_ref[:, cols] = hs[p]
        h = _get_rows(h_ref, p, nb, tc, pitch).reshape(nb, tc, LANE)
        gate = p_ref[:, :, RG_WIDTH + p * LANE:RG_WIDTH + (p + 1) * LANE]
        y_ref[:, :, cols] = (h * jax.nn.gelu(gate)).astype(y_ref.dtype)


def _rglru(proj3, conv0, h0, cw, cb, wg, ba, bx, lam, *, tc, ydtype):
    nb, nt, _ = proj3.shape
    pitch = _step_pitch(tc)
    ntile = RG_WIDTH // LANE
    const = lambda c: (0, 0)
    const3 = lambda c: (0, 0, 0)
    return pl.pallas_call(
        functools.partial(_rglru_kernel, nb=nb, tc=tc, pitch=pitch),
        grid=(nt // tc,),
        in_specs=[
            pl.BlockSpec((nb, tc, 2 * RG_WIDTH), lambda c: (0, c, COL_RG // (2 * RG_WIDTH))),
            pl.BlockSpec((nb, SUBLANE, RG_WIDTH), const3),
            pl.BlockSpec((nb, RG_WIDTH), const),
            pl.BlockSpec((CONV_W, RG_WIDTH), const),
            pl.BlockSpec((1, RG_WIDTH), const),
            pl.BlockSpec((ntile, LANE, 2 * LANE), const3),
            pl.BlockSpec((1, RG_WIDTH), const),
            pl.BlockSpec((1, RG_WIDTH), const),
            pl.BlockSpec((1, RG_WIDTH), const),
        ],
        out_specs=[
            pl.BlockSpec((nb, tc, RG_WIDTH), lambda c: (0, c, 0)),
            pl.BlockSpec((nb, RG_WIDTH), const),
        ],
        out_shape=[jax.ShapeDtypeStruct((nb, nt, RG_WIDTH), ydtype), jax.ShapeDtypeStruct((nb, RG_WIDTH), F32)],
        scratch_shapes=[
            pltpu.VMEM((nb, tc + SUBLANE, RG_WIDTH), F32),
            pltpu.VMEM((ntile, nb * pitch, LANE), F32),
            pltpu.VMEM((ntile, nb * pitch, LANE), F32),
            pltpu.VMEM((nb, RG_WIDTH), F32),
        ],
        compiler_params=_cparams(("arbitrary",)),
        name="rglru",
    )(proj3, conv0, h0, cw, cb, wg, ba, bx, lam)


def _s5_kernel(u_ref, s0r_ref, s0i_ref, lr_ref, li_ref, bre_ref, bim_ref, cre_ref, cim_ref, d_ref, gw_ref, gb_ref,
               y_ref, sr_out, si_out, sre_ref, sim_ref, src_ref, sic_ref, *, nb, tc, pitch, tiles_per_pass):
    nq = S5_WIDTH // LANE
    qt = S5_FLAT // nq // LANE
    ntile = S5_FLAT // LANE

    @pl.when(pl.program_id(0) == 0)
    def _():
        src_ref[...] = s0r_ref[...]
        sic_ref[...] = s0i_ref[...]

    u = u_ref[...].reshape(nb * tc, S5_WIDTH)
    ub = u.astype(BF16)
    for q in range(nq):
        uq = ub[:, q * LANE:(q + 1) * LANE]
        bur = _dot(uq, bre_ref[q])
        bui = _dot(uq, bim_ref[q])
        for j in range(qt):
            _put_rows(sre_ref, q * qt + j, bur[:, j * LANE:(j + 1) * LANE], nb, tc, pitch)
            _put_rows(sim_ref, q * qt + j, bui[:, j * LANE:(j + 1) * LANE], nb, tc, pitch)

    for t0 in range(0, ntile, tiles_per_pass):
        tiles = list(range(t0, t0 + tiles_per_pass))
        lrs = [jnp.broadcast_to(lr_ref[:, tl * LANE:(tl + 1) * LANE], (nb, LANE)) for tl in tiles]
        lis = [jnp.broadcast_to(li_ref[:, tl * LANE:(tl + 1) * LANE], (nb, LANE)) for tl in tiles]

        def step(t, carry, tiles=tiles, lrs=lrs, lis=lis):
            rs = pl.ds(t, nb, stride=pitch)
            out = []
            for n, tl in enumerate(tiles):
                sr, si = carry[2 * n], carry[2 * n + 1]
                nr = lrs[n] * sr - lis[n] * si + sre_ref[tl, rs, :]
                ni = lrs[n] * si + lis[n] * sr + sim_ref[tl, rs, :]
                sre_ref[tl, rs, :] = nr
                sim_ref[tl, rs, :] = ni
                out += [nr, ni]
            return tuple(out)

        init = []
        for tl in tiles:
            init += [src_ref[:, tl * LANE:(tl + 1) * LANE], sic_ref[:, tl * LANE:(tl + 1) * LANE]]
        fin = lax.fori_loop(0, tc, step, tuple(init))
        for n, tl in enumerate(tiles):
            src_ref[:, tl * LANE:(tl + 1) * LANE] = fin[2 * n]
            sic_ref[:, tl * LANE:(tl + 1) * LANE] = fin[2 * n + 1]

    sr_out[...] = src_ref[...]
    si_out[...] = sic_ref[...]

    ys = []
    for q in range(nq):
        sr_q = jnp.concatenate([_get_rows(sre_ref, q * qt + j, nb, tc, pitch) for j in range(qt)], axis=1)
        si_q = jnp.concatenate([_get_rows(sim_ref, q * qt + j, nb, tc, pitch) for j in range(qt)], axis=1)
        ys.append(_dot(sr_q.astype(BF16), cre_ref[q]) - _dot(si_q.astype(BF16), cim_ref[q]))
    y = jnp.concatenate(ys, axis=1) + d_ref[...] * u
    y = jax.nn.gelu(y)
    y = y * jax.nn.sigmoid(_dot(y.astype(BF16), gw_ref[...]) + gb_ref[...])
    y_ref[...] = y.reshape(nb, tc, S5_WIDTH).astype(y_ref.dtype)


def _s5(proj3, s0r, s0i, lr, li, bre, bim, cre, cim, d, gw, gb, *, tc, tiles_per_pass, ydtype):
    nb, nt, _ = proj3.shape
    pitch = _step_pitch(tc)
    nq = S5_WIDTH // LANE
    qs = S5_FLAT // nq
    ntile = S5_FLAT // LANE
    const = lambda c: (0, 0)
    const3 = lambda c: (0, 0, 0)
    return pl.pallas_call(
        functools.partial(_s5_kernel, nb=nb, tc=tc, pitch=pitch, tiles_per_pass=tiles_per_pass),
        grid=(nt // tc,),
        in_specs=[
            pl.BlockSpec((nb, tc, S5_WIDTH), lambda c: (0, c, COL_S5 // S5_WIDTH)),
            pl.BlockSpec((nb, S5_FLAT), const),
            pl.BlockSpec((nb, S5_FLAT), const),
            pl.BlockSpec((1, S5_FLAT), const),
            pl.BlockSpec((1, S5_FLAT), const),
            pl.BlockSpec((nq, LANE, qs), const3),
            pl.BlockSpec((nq, LANE, qs), const3),
            pl.BlockSpec((nq, qs, LANE), const3),
            pl.BlockSpec((nq, qs, LANE), const3),
            pl.BlockSpec((1, S5_WIDTH), const),
            pl.BlockSpec((S5_WIDTH, S5_WIDTH), const),
            pl.BlockSpec((1, S5_WIDTH), const),
        ],
        out_specs=[
            pl.BlockSpec((nb, tc, S5_WIDTH), lambda c: (0, c, 0)),
            pl.BlockSpec((nb, S5_FLAT), const),
            pl.BlockSpec((nb, S5_FLAT), const),
        ],
        out_shape=[
            jax.ShapeDtypeStruct((nb, nt, S5_WIDTH), ydtype),
            jax.ShapeDtypeStruct((nb, S5_FLAT), F32),
            jax.ShapeDtypeStruct((nb, S5_FLAT), F32),
        ],
        scratch_shapes=[
            pltpu.VMEM((ntile, nb * pitch, LANE), F32),
            pltpu.VMEM((ntile, nb * pitch, LANE), F32),
            pltpu.VMEM((nb, S5_FLAT), F32),
            pltpu.VMEM((nb, S5_FLAT), F32),
        ],
        compiler_params=_cparams(("arbitrary",)),
        name="s5",
    )(proj3, s0r, s0i, lr, li, bre, bim, cre, cim, d, gw, gb)


def _unit_lower_inverse_minus_eye(lmat, block):
    x = -lmat
    p = _dot_split(lmat, lmat)
    x = x + p + _dot(x.astype(BF16), p.astype(BF16))
    n = 4
    while n < block:
        pb = p.astype(BF16)
        p = _dot(pb, pb)
        x = x + p + _dot(x.astype(BF16), p.astype(BF16))
        n *= 2
    return x


def _gdn_stack(seqs, ck, xc, g_all, beta_all, z, nw, s_ref, o_ref):
    nblk = len(seqs) * GDN_HEADS
    blocks = [(g, h) for g in seqs for h in range(GDN_HEADS)]

    def stack(fn):
        return jnp.concatenate([fn(g, h) for g, h in blocks], axis=0)

    q = stack(lambda g, h: xc[g, :, h * GDN_DK:(h + 1) * GDN_DK])
    k = stack(lambda g, h: xc[g, :, GDN_QK + h * GDN_DK:GDN_QK + (h + 1) * GDN_DK])
    v = stack(lambda g, h: xc[g, :, 2 * GDN_QK + h * GDN_DV:2 * GDN_QK + (h + 1) * GDN_DV])
    zs = stack(lambda g, h: z[g, :, h * GDN_DV:(h + 1) * GDN_DV])
    gbig = stack(lambda g, h: jnp.broadcast_to(g_all[g, :, h:h + 1], (ck, STACK)))
    beta = stack(lambda g, h: jnp.broadcast_to(beta_all[g, :, GDN_HEADS + h:GDN_HEADS + h + 1], (ck, LANE)))

    q = q * lax.rsqrt(jnp.sum(q * q, axis=-1, keepdims=True) + EPS) * (GDN_DK ** -0.5)
    k = k * lax.rsqrt(jnp.sum(k * k, axis=-1, keepdims=True) + EPS)
    kb = k * beta
    vb = v * beta

    shift = ck.bit_length() - 1
    row = lax.broadcasted_iota(jnp.int32, (STACK, STACK), 0)
    col = lax.broadcasted_iota(jnp.int32, (STACK, STACK), 1)
    same = (row >> shift) == (col >> shift)
    causal = same & (row >= col)
    strict = same & (row > col)
    tril = jnp.where(causal, 1.0, 0.0).astype(BF16)

    gc = _dot_exact_lhs(tril, gbig[:, 0:LANE])
    seg = _dot_exact_lhs(tril, jnp.where(strict, gbig, 0.0))
    decay = jnp.where(causal, jnp.exp(jnp.where(causal, seg, 0.0)), 0.0)

    kbf = k.astype(BF16)
    lmat = jnp.where(strict, _dot_nt(kb.astype(BF16), kbf) * decay, 0.0)
    eye = jnp.where(row == col, 1.0, 0.0).astype(F32)
    tmat = (eye + _unit_lower_inverse_minus_eye(lmat, ck)).astype(BF16)
    egc = jnp.exp(gc)
    uw = _dot(tmat, jnp.concatenate([vb, kb * egc], axis=1).astype(BF16))
    u = uw[:, 0:GDN_DV]
    w = uw[:, GDN_DV:GDN_DV + GDN_DK]
    qk = jnp.where(causal, _dot_nt(q.astype(BF16), kbf) * decay, 0.0).astype(BF16)
    qg = q * egc
    g_last = jnp.concatenate(
        [jnp.broadcast_to(gc[(n + 1) * ck - 1:(n + 1) * ck, :], (ck, LANE)) for n in range(nblk)], axis=0)
    k_dec = (k * jnp.exp(g_last - gc)).astype(BF16)

    v_new, qs = [], []
    for n, (g, h) in enumerate(blocks):
        rs = slice(n * ck, (n + 1) * ck)
        sb = s_ref[g, h].astype(BF16)
        r = _dot(jnp.concatenate([w[rs], qg[rs]], axis=0).astype(BF16), sb)
        v_new.append(u[rs] - r[0:ck])
        qs.append(r[ck:2 * ck])
    v_new = jnp.concatenate(v_new, axis=0)
    vnb = v_new.astype(BF16)
    o = jnp.concatenate(qs, axis=0) + _dot(qk, vnb)
    for n, (g, h) in enumerate(blocks):
        rs = slice(n * ck, (n + 1) * ck)
        s_ref[g, h] = s_ref[g, h] * jnp.exp(gc[(n + 1) * ck - 1:(n + 1) * ck, :]) + _dot_tn(k_dec[rs], vnb[rs])

    o = _rms(o, nw) * (zs * jax.nn.sigmoid(zs))
    for n, (g, h) in enumerate(blocks):
        o_ref[g, :, h * GDN_DV:(h + 1) * GDN_DV] = o[n * ck:(n + 1) * ck].astype(o_ref.dtype)


def _gdn_kernel(q_ref, k_ref, v_ref, z_ref, ab_ref, conv0_ref, s0_ref, cw_ref, alog_ref, dtb_ref, nw_ref,
                o_ref, sout_ref, xb_ref, s_ref, *, ck, nseq, per_stack):
    c = pl.program_id(1)

    @pl.when(c == 0)
    def _():
        xb_ref[:, 0:SUBLANE, :] = conv0_ref[...]
        s_ref[...] = s0_ref[...]

    xb_ref[:, SUBLANE:SUBLANE + ck, 0:GDN_QK] = q_ref[...]
    xb_ref[:, SUBLANE:SUBLANE + ck, GDN_QK:2 * GDN_QK] = k_ref[...]
    xb_ref[:, SUBLANE:SUBLANE + ck, 2 * GDN_QK:QKV_WIDTH] = v_ref[...]
    cw = cw_ref[...]
    first = SUBLANE - (CONV_W - 1)
    xc = xb_ref[:, first:first + ck, :] * cw[0:1, :]
    for j in range(1, CONV_W):
        xc = xc + xb_ref[:, first + j:first + j + ck, :] * cw[j:j + 1, :]
    xb_ref[:, 0:SUBLANE, :] = xb_ref[:, ck:ck + SUBLANE, :]
    xc = xc * jax.nn.sigmoid(xc)

    ab = ab_ref[...]
    g_all = -jnp.exp(alog_ref[...]) * jax.nn.softplus(ab + dtb_ref[...])
    beta_all = jax.nn.sigmoid(ab)
    z = z_ref[...]
    nw = nw_ref[...]
    for s0 in range(0, nseq, per_stack):
        _gdn_stack(list(range(s0, s0 + per_stack)), ck, xc, g_all, beta_all, z, nw, s_ref, o_ref)

    @pl.when(c == pl.num_programs(1) - 1)
    def _():
        sout_ref[...] = s_ref[...]


def _gdn(proj3, ab3, conv0, s0, cw, alog, dtb, nw, *, ck, nstack, odtype):
    nb, nt, _ = proj3.shape
    per_stack = STACK // (GDN_HEADS * ck)
    nseq = nstack * per_stack
    qblk = COL_QKV // GDN_QK
    const = lambda i, c: (0, 0)
    return pl.pallas_call(
        functools.partial(_gdn_kernel, ck=ck, nseq=nseq, per_stack=per_stack),
        grid=(nb // nseq, nt // ck),
        in_specs=[
            pl.BlockSpec((nseq, ck, GDN_QK), lambda i, c: (i, c, qblk)),
            pl.BlockSpec((nseq, ck, GDN_QK), lambda i, c: (i, c, qblk + 1)),
            pl.BlockSpec((nseq, ck, GDN_V), lambda i, c: (i, c, qblk + 2)),
            pl.BlockSpec((nseq, ck, GDN_V), lambda i, c: (i, c, COL_Z // GDN_V)),
            pl.BlockSpec((nseq, ck, N_AB), lambda i, c: (i, c, 0)),
            pl.BlockSpec((nseq, SUBLANE, QKV_WIDTH), lambda i, c: (i, 0, 0)),
            pl.BlockSpec((nseq, GDN_HEADS, GDN_DK, GDN_DV), lambda i, c: (i, 0, 0, 0)),
            pl.BlockSpec((CONV_W, QKV_WIDTH), const),
            pl.BlockSpec((1, LANE), const),
            pl.BlockSpec((1, LANE), const),
            pl.BlockSpec((1, GDN_DV), const),
        ],
        out_specs=[
            pl.BlockSpec((nseq, ck, GDN_V), lambda i, c: (i, c, 0)),
            pl.BlockSpec((nseq, GDN_HEADS, GDN_DK, GDN_DV), lambda i, c: (i, 0, 0, 0)),
        ],
        out_shape=[
            jax.ShapeDtypeStruct((nb, nt, GDN_V), odtype),
            jax.ShapeDtypeStruct((nb, GDN_HEADS, GDN_DK, GDN_DV), F32),
        ],
        scratch_shapes=[
            pltpu.VMEM((nseq, ck + SUBLANE, QKV_WIDTH), F32),
            pltpu.VMEM((nseq, GDN_HEADS, GDN_DK, GDN_DV), F32),
        ],
        compiler_params=_cparams(("parallel", "arbitrary")),
        name="gdn",
    )(proj3, proj3, proj3, proj3, ab3, conv0, s0, cw, alog, dtb, nw)


def _merge_kernel(x_ref, yrg_ref, ys5_ref, ygd_ref, grg_ref, gs5_ref, ggd_ref, wrg_ref, ws5_ref, wgd_ref,
                  wo_ref, g_ref, o_ref):
    mixed = jax.nn.sigmoid(grg_ref[...]) * _dot(yrg_ref[...].astype(BF16), wrg_ref[...])
    mixed = mixed + jax.nn.sigmoid(gs5_ref[...]) * _dot(ys5_ref[...].astype(BF16), ws5_ref[...])
    mixed = mixed + jax.nn.sigmoid(ggd_ref[...]) * _dot(ygd_ref[...].astype(BF16), wgd_ref[...])
    out = _dot(mixed.astype(BF16), wo_ref[...])
    o_ref[...] = x_ref[...] + _rms(out, g_ref[...])


def _merge(x, yrg, ys5, ygd, proj, wrg, ws5, wgd, wo, g, *, tm=512):
    m = x.shape[0]
    tm = min(tm, m)
    const = lambda i: (0, 0)
    half = D_MODEL // 2
    return pl.pallas_call(
        _merge_kernel,
        grid=(m // tm,),
        in_specs=[
            pl.BlockSpec((tm, D_MODEL), lambda i: (i, 0)),
            pl.BlockSpec((tm, half), lambda i: (i, 0)),
            pl.BlockSpec((tm, half), lambda i: (i, 0)),
            pl.BlockSpec((tm, half), lambda i: (i, 0)),
            pl.BlockSpec((tm, D_MODEL), lambda i: (i, 0)),
            pl.BlockSpec((tm, D_MODEL), lambda i: (i, 1)),
            pl.BlockSpec((tm, D_MODEL), lambda i: (i, 2)),
            pl.BlockSpec((half, D_MODEL), const),
            pl.BlockSpec((half, D_MODEL), const),
            pl.BlockSpec((half, D_MODEL), const),
            pl.BlockSpec((D_MODEL, D_MODEL), const),
            pl.BlockSpec((1, D_MODEL), const),
        ],
        out_specs=pl.BlockSpec((tm, D_MODEL), lambda i: (i, 0)),
        out_shape=jax.ShapeDtypeStruct((m, D_MODEL), F32),
        compiler_params=_cparams(("parallel",)),
        name="merge",
    )(x, yrg, ys5, ygd, proj, proj, proj, wrg, ws5, wgd, wo, g)


def _ffn_kernel(x_ref, gpre_ref, wg_ref, wu_ref, wd_ref, gpost_ref, o_ref, h_ref, acc_ref):
    j = pl.program_id(1)

    @pl.when(j == 0)
    def _():
        h_ref[...] = _rms(x_ref[...], gpre_ref[...]).astype(BF16)
        acc_ref[...] = jnp.zeros_like(acc_ref)

    h = h_ref[...]
    gate = _dot(h, wg_ref[...])
    act = (gate * jax.nn.sigmoid(gate)) * _dot(h, wu_ref[...])
    acc_ref[...] += _dot(act.astype(BF16), wd_ref[...])

    @pl.when(j == pl.num_programs(1) - 1)
    def _():
        o_ref[...] = x_ref[...] + _rms(acc_ref[...], gpost_ref[...])


def _ffn(x, gpre, wg, wu, wd, gpost, *, tm=512, tf=1408):
    m = x.shape[0]
    tm = min(tm, m)
    const = lambda i, j: (0, 0)
    return pl.pallas_call(
        _ffn_kernel,
        grid=(m // tm, D_FF // tf),
        in_specs=[
            pl.BlockSpec((tm, D_MODEL), lambda i, j: (i, 0)),
            pl.BlockSpec((1, D_MODEL), const),
            pl.BlockSpec((D_MODEL, tf), lambda i, j: (0, j)),
            pl.BlockSpec((D_MODEL, tf), lambda i, j: (0, j)),
            pl.BlockSpec((tf, D_MODEL), lambda i, j: (j, 0)),
            pl.BlockSpec((1, D_MODEL), const),
        ],
        out_specs=pl.BlockSpec((tm, D_MODEL), lambda i, j: (i, 0)),
        out_shape=jax.ShapeDtypeStruct((m, D_MODEL), F32),
        scratch_shapes=[pltpu.VMEM((tm, D_MODEL), BF16), pltpu.VMEM((tm, D_MODEL), F32)],
        compiler_params=_cparams(("parallel", "arbitrary")),
        name="ffn",
    )(x, gpre, wg, wu, wd, gpost)


def _block_diag(blocks):
    n, r, c = blocks.shape
    eye = jnp.eye(n, dtype=blocks.dtype)
    return jnp.einsum("nrc,nm->nrmc", blocks, eye).reshape(n * r, n * c)


def _prep_layer(l, p):
    f32 = F32
    w_in = p["w_in"][l]
    sizes = (RG_WIDTH, RG_WIDTH, S5_WIDTH, QKV_WIDTH, GDN_V, GDN_HEADS, GDN_HEADS, 3 * D_MODEL)
    offs = [0]
    for s in sizes:
        offs.append(offs[-1] + s)
    seg = lambda i: w_in[:, offs[i]:offs[i + 1]]
    w_main = jnp.concatenate([seg(7), seg(0), seg(1), seg(2), seg(3), seg(4)], axis=1).astype(BF16)
    w_ab = jnp.concatenate(
        [seg(5), seg(6), jnp.zeros((D_MODEL, N_AB - 2 * GDN_HEADS), f32)], axis=1).astype(BF16)

    pair = LANE // RG_BLOCK
    wa = p["rg_wa"][l].reshape(RG_WIDTH // LANE, pair, RG_BLOCK, RG_BLOCK)
    wx = p["rg_wx"][l].reshape(RG_WIDTH // LANE, pair, RG_BLOCK, RG_BLOCK)
    wg = jnp.stack([jnp.concatenate([_block_diag(wa[i]), _block_diag(wx[i])], axis=1)
                    for i in range(RG_WIDTH // LANE)]).astype(BF16)

    a_re = p["s5_a_re"][l].astype(f32)
    a_im = p["s5_a_im"][l].astype(f32)
    dt = jnp.exp(p["s5_log_dt"][l].astype(f32))[:, None]
    mag = jnp.exp(a_re * dt)
    lr = mag * jnp.cos(a_im * dt)
    li = mag * jnp.sin(a_im * dt)
    den = a_re * a_re + a_im * a_im
    cr = ((lr - 1.0) * a_re + li * a_im) / den
    ci = (li * a_re - (lr - 1.0) * a_im) / den
    b_re = p["s5_b_re"][l].astype(f32)
    b_im = p["s5_b_im"][l].astype(f32)
    bb_re = cr[..., None] * b_re - ci[..., None] * b_im
    bb_im = cr[..., None] * b_im + ci[..., None] * b_re
    nq = S5_WIDTH // LANE
    gq = S5_GROUPS // nq
    to_in = lambda a: jnp.swapaxes(a, 1, 2).reshape(nq, gq, S5_GROUP, S5_STATE)
    bre = jnp.stack([_block_diag(to_in(bb_re)[i]) for i in range(nq)]).astype(BF16)
    bim = jnp.stack([_block_diag(to_in(bb_im)[i]) for i in range(nq)]).astype(BF16)
    to_out = lambda a: jnp.swapaxes(a.astype(f32), 1, 2).reshape(nq, gq, S5_STATE, S5_GROUP)
    cre = jnp.stack([_block_diag(to_out(p["s5_c_re"][l])[i]) for i in range(nq)]).astype(BF16)
    cim = jnp.stack([_block_diag(to_out(p["s5_c_im"][l])[i]) for i in range(nq)]).astype(BF16)

    pad4 = lambda a: jnp.pad(a.astype(f32), (0, LANE - GDN_HEADS)).reshape(1, LANE)
    row = lambda a: a.astype(f32).reshape(1, -1)
    return dict(
        g_pre_mix=row(p["norm_pre_mix"][l]), g_post_mix=row(p["norm_post_mix"][l]),
        g_pre_ffn=row(p["norm_pre_ffn"][l]), g_post_ffn=row(p["norm_post_ffn"][l]),
        w_main=w_main, w_ab=w_ab,
        rg_cw=p["rg_conv_w"][l].astype(f32), rg_cb=row(p["rg_conv_b"][l]), rg_wg=wg,
        rg_ba=row(p["rg_ba"][l]), rg_bx=row(p["rg_bx"][l]), rg_lam=row(p["rg_lambda"][l]),
        rg_wo=p["rg_w_out"][l].astype(BF16),
        s5_lr=lr.reshape(1, S5_FLAT), s5_li=li.reshape(1, S5_FLAT),
        s5_bre=bre, s5_bim=bim, s5_cre=cre, s5_cim=cim, s5_d=row(p["s5_d"][l]),
        s5_gw=p["s5_glu_w"][l].astype(BF16), s5_gb=row(p["s5_glu_b"][l]), s5_wo=p["s5_w_out"][l].astype(BF16),
        gd_cw=p["gdn_conv_w"][l].astype(f32), gd_alog=pad4(p["gdn_a_log"][l]), gd_dtb=pad4(p["gdn_dt_bias"][l]),
        gd_nw=row(p["gdn_norm_w"][l]), gd_wo=p["gdn_w_out"][l].astype(BF16),
        w_out=p["w_out"][l].astype(BF16),
        ffn_wg=p["ffn_w_gate"][l].astype(BF16), ffn_wu=p["ffn_w_up"][l].astype(BF16),
        ffn_wd=p["ffn_w_down"][l].astype(BF16),
    )


def _pad_conv_state(s):
    return jnp.pad(s, ((0, 0), (SUBLANE - (CONV_W - 1), 0), (0, 0)))


def _run_group(x, states, layers, *, tc, ck, nstack, tiles_per_pass):
    nb, nt, _ = x.shape
    keep = CONV_W - 1
    assert nt >= keep and nt % SUBLANE == 0 and nb % SUBLANE == 0
    st_rg_conv, st_rg_h, st_s5_re, st_s5_im, st_gd_conv, st_gd_s = states
    ydtype = BF16 if (tc % BF16_ROWS == 0 and ck % BF16_ROWS == 0) else F32
    xf = x.reshape(nb * nt, D_MODEL)
    outs = [[] for _ in range(6)]
    for l, w in enumerate(layers):
        proj, ab = _in_proj(xf, w["g_pre_mix"], w["w_main"], w["w_ab"])
        proj3 = proj.reshape(nb, nt, N_MAIN)

        y_rg, h_last = _rglru(proj3, _pad_conv_state(st_rg_conv[l]), st_rg_h[l], w["rg_cw"], w["rg_cb"],
                              w["rg_wg"], w["rg_ba"], w["rg_bx"], w["rg_lam"], tc=tc, ydtype=ydtype)

        y_s5, s_re, s_im = _s5(proj3, st_s5_re[l].reshape(nb, S5_FLAT), st_s5_im[l].reshape(nb, S5_FLAT),
                               w["s5_lr"], w["s5_li"], w["s5_bre"], w["s5_bim"], w["s5_cre"], w["s5_cim"],
                               w["s5_d"], w["s5_gw"], w["s5_gb"], tc=tc, tiles_per_pass=tiles_per_pass,
                               ydtype=ydtype)

        y_gd, s_gd = _gdn(proj3, ab.reshape(nb, nt, N_AB), _pad_conv_state(st_gd_conv[l]), st_gd_s[l],
                          w["gd_cw"], w["gd_alog"], w["gd_dtb"], w["gd_nw"], ck=ck, nstack=nstack, odtype=ydtype)

        flat = lambda a: a.reshape(nb * nt, a.shape[-1])
        x1 = _merge(xf, flat(y_rg), flat(y_s5), flat(y_gd), proj, w["rg_wo"], w["s5_wo"], w["gd_wo"],
                    w["w_out"], w["g_post_mix"])
        xf = _ffn(x1, w["g_pre_ffn"], w["ffn_wg"], w["ffn_wu"], w["ffn_wd"], w["g_post_ffn"])

        outs[0].append(proj3[:, nt - keep:, COL_RG:COL_RG + RG_WIDTH])
        outs[1].append(h_last)
        outs[2].append(s_re.reshape(nb, S5_GROUPS, S5_STATE))
        outs[3].append(s_im.reshape(nb, S5_GROUPS, S5_STATE))
        outs[4].append(proj3[:, nt - keep:, COL_QKV:COL_QKV + QKV_WIDTH])
        outs[5].append(s_gd)
    return xf.reshape(nb, nt, D_MODEL), [jnp.stack(o) for o in outs]


def _zero_states(depth, nb):
    return (jnp.zeros((depth, nb, CONV_W - 1, RG_WIDTH), F32),
            jnp.zeros((depth, nb, RG_WIDTH), F32),
            jnp.zeros((depth, nb, S5_GROUPS, S5_STATE), F32),
            jnp.zeros((depth, nb, S5_GROUPS, S5_STATE), F32),
            jnp.zeros((depth, nb, CONV_W - 1, QKV_WIDTH), F32),
            jnp.zeros((depth, nb, GDN_HEADS, GDN_DK, GDN_DV), F32))


def kernel(x_prompt, x_sample, state_rg_conv, state_rg_h, state_s5_re, state_s5_im, state_gdn_conv, state_gdn_s, norm_pre_mix, norm_post_mix, norm_pre_ffn, norm_post_ffn, w_in, rg_conv_w, rg_conv_b, rg_wa, rg_ba, rg_wx, rg_bx, rg_lambda, rg_w_out, s5_a_re, s5_a_im, s5_b_re, s5_b_im, s5_c_re, s5_c_im, s5_d, s5_log_dt, s5_glu_w, s5_glu_b, s5_w_out, gdn_conv_w, gdn_a_log, gdn_dt_bias, gdn_norm_w, gdn_w_out, w_out, ffn_w_gate, ffn_w_up, ffn_w_down):
    p = dict(norm_pre_mix=norm_pre_mix, norm_post_mix=norm_post_mix, norm_pre_ffn=norm_pre_ffn,
             norm_post_ffn=norm_post_ffn, w_in=w_in, rg_conv_w=rg_conv_w, rg_conv_b=rg_conv_b, rg_wa=rg_wa,
             rg_ba=rg_ba, rg_wx=rg_wx, rg_bx=rg_bx, rg_lambda=rg_lambda, rg_w_out=rg_w_out, s5_a_re=s5_a_re,
             s5_a_im=s5_a_im, s5_b_re=s5_b_re, s5_b_im=s5_b_im, s5_c_re=s5_c_re, s5_c_im=s5_c_im, s5_d=s5_d,
             s5_log_dt=s5_log_dt, s5_glu_w=s5_glu_w, s5_glu_b=s5_glu_b, s5_w_out=s5_w_out,
             gdn_conv_w=gdn_conv_w, gdn_a_log=gdn_a_log, gdn_dt_bias=gdn_dt_bias, gdn_norm_w=gdn_norm_w,
             gdn_w_out=gdn_w_out, w_out=w_out, ffn_w_gate=ffn_w_gate, ffn_w_up=ffn_w_up, ffn_w_down=ffn_w_down)
    depth = w_in.shape[0]
    layers = [_prep_layer(l, p) for l in range(depth)]
    y_p, st_p = _run_group(x_prompt, _zero_states(depth, x_prompt.shape[0]), layers,
                           tc=32, ck=64, nstack=4, tiles_per_pass=8)
    y_s, st_s = _run_group(
        x_sample, (state_rg_conv, state_rg_h, state_s5_re, state_s5_im, state_gdn_conv, state_gdn_s), layers,
        tc=x_sample.shape[1], ck=x_sample.shape[1], nstack=2, tiles_per_pass=1)
    return (y_p, y_s, *st_p, *st_s)
```

```python
import functools

import jax
import jax.numpy as jnp
from jax import lax
from jax.experimental import pallas as pl
from jax.experimental.pallas import tpu as pltpu

F32 = jnp.float32
BF16 = jnp.bfloat16

D_MODEL = 1024
CONV_W = 4
EPS = 1e-6
RG_WIDTH = 512
RG_BLOCK = 64
RG_C = 8.0
S5_WIDTH = 512
S5_GROUP = 16
S5_GROUPS = 32
S5_STATE = 64
S5_FLAT = S5_GROUPS * S5_STATE
GDN_HEADS = 4
GDN_DK = 128
GDN_DV = 128
GDN_QK = 512
GDN_V = 512
QKV_WIDTH = 1536
D_FF = 2816

LANE = 128
SUBLANE = 8
BF16_ROWS = 16
MXU_EDGE = 256
MASKED_LOG = -1e30

COL_GATES = 0
COL_RG = 3072
COL_S5 = 4096
COL_QKV = 4608
COL_Z = 6144
N_MAIN = 6656
N_AB = LANE

VMEM_LIMIT = 52 * 1024 * 1024


def _cparams(sem):
    return pltpu.CompilerParams(dimension_semantics=sem, vmem_limit_bytes=VMEM_LIMIT)


def _rms(x, g):
    return x * lax.rsqrt(jnp.mean(x * x, axis=-1, keepdims=True) + EPS) * g


def _dot(a, b):
    return jnp.dot(a, b, preferred_element_type=F32)


def _dot_nt(a, b):
    return lax.dot_general(a, b, (((1,), (1,)), ((), ())), preferred_element_type=F32)


def _dot_tn(a, b):
    return lax.dot_general(a, b, (((0,), (0,)), ((), ())), preferred_element_type=F32)


def _split2(a):
    hi = a.astype(BF16)
    return hi, (a - hi.astype(F32)).astype(BF16)


def _dot_exact_lhs(a, b):
    b1 = b.astype(BF16)
    r = b - b1.astype(F32)
    b2 = r.astype(BF16)
    b3 = (r - b2.astype(F32)).astype(BF16)
    return _dot(a, b1) + (_dot(a, b2) + _dot(a, b3))


def _dot_split(a, b):
    a1, a2 = _split2(a)
    b1, b2 = _split2(b)
    return _dot(a1, b1) + (_dot(a1, b2) + _dot(a2, b1))


def _step_pitch(nt_chunk):
    return nt_chunk if nt_chunk <= SUBLANE else nt_chunk + SUBLANE


def _put_rows(ref, tile, val, nb, tc, pitch):
    if pitch == tc:
        ref[tile, 0:nb * tc, :] = val
    else:
        for b in range(nb):
            ref[tile, b * pitch:b * pitch + tc, :] = val[b * tc:(b + 1) * tc, :]


def _get_rows(ref, tile, nb, tc, pitch):
    if pitch == tc:
        return ref[tile, 0:nb * tc, :]
    return jnp.concatenate([ref[tile, b * pitch:b * pitch + tc, :] for b in range(nb)], axis=0)


def _in_proj_kernel(x_ref, g_ref, w_ref, wab_ref, o_ref, oab_ref, h_ref):
    @pl.when(pl.program_id(1) == 0)
    def _():
        h = _rms(x_ref[...], g_ref[...]).astype(BF16)
        h_ref[...] = h
        oab_ref[...] = _dot(h, wab_ref[...])

    o_ref[...] = _dot(h_ref[...], w_ref[...])


def _in_proj(x, g, w, wab, *, tm=1024, tn=1664):
    m = x.shape[0]
    tm = min(tm, m)
    return pl.pallas_call(
        _in_proj_kernel,
        grid=(m // tm, N_MAIN // tn),
        in_specs=[
            pl.BlockSpec((tm, D_MODEL), lambda i, j: (i, 0)),
            pl.BlockSpec((1, D_MODEL), lambda i, j: (0, 0)),
            pl.BlockSpec((D_MODEL, tn), lambda i, j: (0, j)),
            pl.BlockSpec((D_MODEL, N_AB), lambda i, j: (0, 0)),
        ],
        out_specs=[
            pl.BlockSpec((tm, tn), lambda i, j: (i, j)),
            pl.BlockSpec((tm, N_AB), lambda i, j: (i, 0)),
        ],
        out_shape=[jax.ShapeDtypeStruct((m, N_MAIN), F32), jax.ShapeDtypeStruct((m, N_AB), F32)],
        scratch_shapes=[pltpu.VMEM((tm, D_MODEL), BF16)],
        compiler_params=_cparams(("parallel", "arbitrary")),
        name="in_proj",
    )(x, g, w, wab)


def _rglru_kernel(p_ref, conv0_ref, h0_ref, cw_ref, cb_ref, wg_ref, ba_ref, bx_ref, lam_ref,
                  y_ref, hl_ref, xp_ref, a_ref, h_ref, hc_ref, *, nb, tc, pitch):
    ntile = RG_WIDTH // LANE
    first = SUBLANE - (CONV_W - 1)

    @pl.when(pl.program_id(0) == 0)
    def _():
        xp_ref[:, 0:SUBLANE, :] = conv0_ref[...]
        hc_ref[...] = h0_ref[...]

    xp_ref[:, SUBLANE:SUBLANE + tc, :] = p_ref[:, :, 0:RG_WIDTH]
    cw = cw_ref[...]
    xc = cb_ref[...] + xp_ref[:, first:first + tc, :] * cw[0:1, :]
    for k in range(1, CONV_W):
        xc = xc + xp_ref[:, first + k:first + k + tc, :] * cw[k:k + 1, :]
    xp_ref[:, 0:SUBLANE, :] = xp_ref[:, tc:tc + SUBLANE, :]

    xc = xc.reshape(nb * tc, RG_WIDTH)
    xcb = xc.astype(BF16)
    for p in range(ntile):
        cols = slice(p * LANE, (p + 1) * LANE)
        gm = _dot(xcb[:, cols], wg_ref[p])
        r = jax.nn.sigmoid(gm[:, 0:LANE] + ba_ref[:, cols])
        i = jax.nn.sigmoid(gm[:, LANE:2 * LANE] + bx_ref[:, cols])
        log_a = -RG_C * r * jax.nn.softplus(-lam_ref[:, cols])
        _put_rows(a_ref, p, jnp.exp(log_a), nb, tc, pitch)
        _put_rows(h_ref, p, jnp.sqrt(1.0 - jnp.exp(2.0 * log_a)) * (i * xc[:, cols]), nb, tc, pitch)

    def step(t, hs):
        rs = pl.ds(t, nb, stride=pitch)
        out = []
        for p in range(ntile):
            h = a_ref[p, rs, :] * hs[p] + h_ref[p, rs, :]
            h_ref[p, rs, :] = h
            out.append(h)
        return tuple(out)

    hc = hc_ref[...]
    hs = lax.fori_loop(0, tc, step, tuple(hc[:, p * LANE:(p + 1) * LANE] for p in range(ntile)), unroll=2)
    for p in range(ntile):
        cols = slice(p * LANE, (p + 1) * LANE)
        hc_ref[:, cols] = hs[p]
        hl_ref[:, cols] = hs[p]
        h = _get_rows(h_ref, p, nb, tc, pitch).reshape(nb, tc, LANE)
        gate = p_ref[:, :, RG_WIDTH + p * LANE:RG_WIDTH + (p + 1) * LANE]
        y_ref[:, :, cols] = (h * jax.nn.gelu(gate)).astype(y_ref.dtype)


def _rglru(proj3, conv0, h0, cw, cb, wg, ba, bx, lam, *, tc, ydtype):
    nb, nt, _ = proj3.shape
    pitch = _step_pitch(tc)
    ntile = RG_WIDTH // LANE
    const = lambda c: (0, 0)
    const3 = lambda c: (0, 0, 0)
    return pl.pallas_call(
        functools.partial(_rglru_kernel, nb=nb, tc=tc, pitch=pitch),
        grid=(nt // tc,),
        in_specs=[
            pl.BlockSpec((nb, tc, 2 * RG_WIDTH), lambda c: (0, c, COL_RG // (2 * RG_WIDTH))),
            pl.BlockSpec((nb, SUBLANE, RG_WIDTH), const3),
            pl.BlockSpec((nb, RG_WIDTH), const),
            pl.BlockSpec((CONV_W, RG_WIDTH), const),
            pl.BlockSpec((1, RG_WIDTH), const),
            pl.BlockSpec((ntile, LANE, 2 * LANE), const3),
            pl.BlockSpec((1, RG_WIDTH), const),
            pl.BlockSpec((1, RG_WIDTH), const),
            pl.BlockSpec((1, RG_WIDTH), const),
        ],
        out_specs=[
            pl.BlockSpec((nb, tc, RG_WIDTH), lambda c: (0, c, 0)),
            pl.BlockSpec((nb, RG_WIDTH), const),
        ],
        out_shape=[jax.ShapeDtypeStruct((nb, nt, RG_WIDTH), ydtype), jax.ShapeDtypeStruct((nb, RG_WIDTH), F32)],
        scratch_shapes=[
            pltpu.VMEM((nb, tc + SUBLANE, RG_WIDTH), F32),
            pltpu.VMEM((ntile, nb * pitch, LANE), F32),
            pltpu.VMEM((ntile, nb * pitch, LANE), F32),
            pltpu.VMEM((nb, RG_WIDTH), F32),
        ],
        compiler_params=_cparams(("arbitrary",)),
        name="rglru",
    )(proj3, conv0, h0, cw, cb, wg, ba, bx, lam)


def _s5_kernel(u_ref, s0r_ref, s0i_ref, lr_ref, li_ref, bre_ref, bim_ref, cre_ref, cim_ref, d_ref, gw_ref, gb_ref,
               y_ref, sr_out, si_out, sre_ref, sim_ref, src_ref, sic_ref, *, nb, tc, pitch, tiles_per_pass):
    nq = S5_WIDTH // LANE
    qt = S5_FLAT // nq // LANE
    ntile = S5_FLAT // LANE

    @pl.when(pl.program_id(0) == 0)
    def _():
        src_ref[...] = s0r_ref[...]
        sic_ref[...] = s0i_ref[...]

    u = u_ref[...].reshape(nb * tc, S5_WIDTH)
    ub = u.astype(BF16)
    for q in range(nq):
        uq = ub[:, q * LANE:(q + 1) * LANE]
        bur = _dot(uq, bre_ref[q])
        bui = _dot(uq, bim_ref[q])
        for j in range(qt):
            _put_rows(sre_ref, q * qt + j, bur[:, j * LANE:(j + 1) * LANE], nb, tc, pitch)
            _put_rows(sim_ref, q * qt + j, bui[:, j * LANE:(j + 1) * LANE], nb, tc, pitch)

    for t0 in range(0, ntile, tiles_per_pass):
        tiles = list(range(t0, t0 + tiles_per_pass))
        lrs = [jnp.broadcast_to(lr_ref[:, tl * LANE:(tl + 1) * LANE], (nb, LANE)) for tl in tiles]
        lis = [jnp.broadcast_to(li_ref[:, tl * LANE:(tl + 1) * LANE], (nb, LANE)) for tl in tiles]

        def step(t, carry, tiles=tiles, lrs=lrs, lis=lis):
            rs = pl.ds(t, nb, stride=pitch)
            out = []
            for n, tl in enumerate(tiles):
                sr, si = carry[2 * n], carry[2 * n + 1]
                nr = lrs[n] * sr - lis[n] * si + sre_ref[tl, rs, :]
                ni = lrs[n] * si + lis[n] * sr + sim_ref[tl, rs, :]
                sre_ref[tl, rs, :] = nr
                sim_ref[tl, rs, :] = ni
                out += [nr, ni]
            return tuple(out)

        init = []
        for tl in tiles:
            init += [src_ref[:, tl * LANE:(tl + 1) * LANE], sic_ref[:, tl * LANE:(tl + 1) * LANE]]
        fin = lax.fori_loop(0, tc, step, tuple(init), unroll=2)
        for n, tl in enumerate(tiles):
            src_ref[:, tl * LANE:(tl + 1) * LANE] = fin[2 * n]
            sic_ref[:, tl * LANE:(tl + 1) * LANE] = fin[2 * n + 1]

    sr_out[...] = src_ref[...]
    si_out[...] = sic_ref[...]

    ys = []
    for q in range(nq):
        sr_q = jnp.concatenate([_get_rows(sre_ref, q * qt + j, nb, tc, pitch) for j in range(qt)], axis=1)
        si_q = jnp.concatenate([_get_rows(sim_ref, q * qt + j, nb, tc, pitch) for j in range(qt)], axis=1)
        ys.append(_dot(sr_q.astype(BF16), cre_ref[q]) - _dot(si_q.astype(BF16), cim_ref[q]))
    y = jnp.concatenate(ys, axis=1) + d_ref[...] * u
    y = jax.nn.gelu(y)
    y = y * jax.nn.sigmoid(_dot(y.astype(BF16), gw_ref[...]) + gb_ref[...])
    y_ref[...] = y.reshape(nb, tc, S5_WIDTH).astype(y_ref.dtype)


def _s5(proj3, s0r, s0i, lr, li, bre, bim, cre, cim, d, gw, gb, *, tc, tiles_per_pass, ydtype):
    nb, nt, _ = proj3.shape
    pitch = _step_pitch(tc)
    nq = S5_WIDTH // LANE
    qs = S5_FLAT // nq
    ntile = S5_FLAT // LANE
    const = lambda c: (0, 0)
    const3 = lambda c: (0, 0, 0)
    return pl.pallas_call(
        functools.partial(_s5_kernel, nb=nb, tc=tc, pitch=pitch, tiles_per_pass=tiles_per_pass),
        grid=(nt // tc,),
        in_specs=[
            pl.BlockSpec((nb, tc, S5_WIDTH), lambda c: (0, c, COL_S5 // S5_WIDTH)),
            pl.BlockSpec((nb, S5_FLAT), const),
            pl.BlockSpec((nb, S5_FLAT), const),
            pl.BlockSpec((1, S5_FLAT), const),
            pl.BlockSpec((1, S5_FLAT), const),
            pl.BlockSpec((nq, LANE, qs), const3),
            pl.BlockSpec((nq, LANE, qs), const3),
            pl.BlockSpec((nq, qs, LANE), const3),
            pl.BlockSpec((nq, qs, LANE), const3),
            pl.BlockSpec((1, S5_WIDTH), const),
            pl.BlockSpec((S5_WIDTH, S5_WIDTH), const),
            pl.BlockSpec((1, S5_WIDTH), const),
        ],
        out_specs=[
            pl.BlockSpec((nb, tc, S5_WIDTH), lambda c: (0, c, 0)),
            pl.BlockSpec((nb, S5_FLAT), const),
            pl.BlockSpec((nb, S5_FLAT), const),
        ],
        out_shape=[
            jax.ShapeDtypeStruct((nb, nt, S5_WIDTH), ydtype),
            jax.ShapeDtypeStruct((nb, S5_FLAT), F32),
            jax.ShapeDtypeStruct((nb, S5_FLAT), F32),
        ],
        scratch_shapes=[
            pltpu.VMEM((ntile, nb * pitch, LANE), F32),
            pltpu.VMEM((ntile, nb * pitch, LANE), F32),
            pltpu.VMEM((nb, S5_FLAT), F32),
            pltpu.VMEM((nb, S5_FLAT), F32),
        ],
        compiler_params=_cparams(("arbitrary",)),
        name="s5",
    )(proj3, s0r, s0i, lr, li, bre, bim, cre, cim, d, gw, gb)


def _gdn_masks(sr, ck):
    shift = ck.bit_length() - 1
    row = lax.broadcasted_iota(jnp.int32, (sr, sr), 0)
    col = lax.broadcasted_iota(jnp.int32, (sr, sr), 1)
    same = (row >> shift) == (col >> shift)
    causal = same & (row >= col)
    strict = same & (row > col)
    tril = jnp.where(causal, 1.0, 0.0).astype(BF16)
    eye = jnp.where(row == col, 1.0, 0.0).astype(F32)
    return causal, strict, tril, eye


def _gdn_stack(blocks, ck, masks, xc, g_all, beta_all, z, nw, s_ref, o_ref):
    nblk = len(blocks)
    sr = nblk * ck
    causal, strict, tril, eye = masks

    def stack(fn):
        return jnp.concatenate([fn(g, h) for g, h in blocks], axis=0)

    q = stack(lambda g, h: xc[g, :, h * GDN_DK:(h + 1) * GDN_DK])
    k = stack(lambda g, h: xc[g, :, GDN_QK + h * GDN_DK:GDN_QK + (h + 1) * GDN_DK])
    v = stack(lambda g, h: xc[g, :, 2 * GDN_QK + h * GDN_DV:2 * GDN_QK + (h + 1) * GDN_DV])
    zs = stack(lambda g, h: z[g, :, h * GDN_DV:(h + 1) * GDN_DV])
    gcol = stack(lambda g, h: jnp.broadcast_to(g_all[g, :, h:h + 1], (ck, LANE)))
    beta = stack(lambda g, h: jnp.broadcast_to(beta_all[g, :, GDN_HEADS + h:GDN_HEADS + h + 1], (ck, LANE)))

    q = q * lax.rsqrt(jnp.sum(q * q, axis=-1, keepdims=True) + EPS) * (GDN_DK ** -0.5)
    k = k * lax.rsqrt(jnp.sum(k * k, axis=-1, keepdims=True) + EPS)
    kb = k * beta
    vb = v * beta
    yield

    gc = _dot_exact_lhs(tril, gcol)
    kbf = k.astype(BF16)
    kk = _dot_nt(kb.astype(BF16), kbf)
    qk = _dot_nt(q.astype(BF16), kbf)
    yield
    gc_lanes = jnp.transpose(gc)[0:SUBLANE, :]
    gc_rows = jnp.concatenate([gc] * (sr // LANE), axis=1) if sr > LANE else gc[:, 0:sr]
    seg = gc_rows - jnp.concatenate([gc_lanes] * (sr // SUBLANE), axis=0)
    decay = jnp.exp(jnp.where(causal, seg, MASKED_LOG))

    lmat = jnp.where(strict, kk * decay, 0.0)
    qk = (qk * decay).astype(BF16)

    x = -lmat
    p = _dot_split(lmat, lmat)
    yield
    x = x + p + _dot(x.astype(BF16), p.astype(BF16))
    n = 4
    while n < ck:
        pb = p.astype(BF16)
        p = _dot(pb, pb)
        yield
        x = x + p + _dot(x.astype(BF16), p.astype(BF16))
        yield
        n *= 2
    tmat = (eye + x).astype(BF16)
    egc = jnp.exp(gc)
    uw = _dot(tmat, jnp.concatenate([vb, kb * egc], axis=1).astype(BF16))
    yield
    u = uw[:, 0:GDN_DV]
    w = uw[:, GDN_DV:GDN_DV + GDN_DK]
    qg = q * egc
    g_last = jnp.concatenate(
        [jnp.broadcast_to(gc[(n + 1) * ck - 1:(n + 1) * ck, :], (ck, LANE)) for n in range(nblk)], axis=0)
    k_dec = (k * jnp.exp(g_last - gc)).astype(BF16)

    v_new, qs = [], []
    for n, (g, h) in enumerate(blocks):
        rs = slice(n * ck, (n + 1) * ck)
        sb = s_ref[g, h].astype(BF16)
        r = _dot(jnp.concatenate([w[rs], qg[rs]], axis=0).astype(BF16), sb)
        v_new.append(u[rs] - r[0:ck])
        qs.append(r[ck:2 * ck])
    yield
    v_new = jnp.concatenate(v_new, axis=0)
    vnb = v_new.astype(BF16)
    o = jnp.concatenate(qs, axis=0) + _dot(qk, vnb)
    for n, (g, h) in enumerate(blocks):
        rs = slice(n * ck, (n + 1) * ck)
        s_ref[g, h] = s_ref[g, h] * jnp.exp(gc[(n + 1) * ck - 1:(n + 1) * ck, :]) + _dot_tn(k_dec[rs], vnb[rs])
    yield

    o = _rms(o, nw) * (zs * jax.nn.sigmoid(zs))
    for n, (g, h) in enumerate(blocks):
        o_ref[g, :, h * GDN_DV:(h + 1) * GDN_DV] = o[n * ck:(n + 1) * ck].astype(o_ref.dtype)


def _gdn_kernel(q_ref, k_ref, v_ref, z_ref, ab_ref, conv0_ref, s0_ref, cw_ref, alog_ref, dtb_ref, nw_ref,
                o_ref, sout_ref, xb_ref, s_ref, *, ck, nseq, stack_rows):
    c = pl.program_id(1)

    @pl.when(c == 0)
    def _():
        xb_ref[:, 0:SUBLANE, :] = conv0_ref[...]
        s_ref[...] = s0_ref[...]

    xb_ref[:, SUBLANE:SUBLANE + ck, 0:GDN_QK] = q_ref[...]
    xb_ref[:, SUBLANE:SUBLANE + ck, GDN_QK:2 * GDN_QK] = k_ref[...]
    xb_ref[:, SUBLANE:SUBLANE + ck, 2 * GDN_QK:QKV_WIDTH] = v_ref[...]
    cw = cw_ref[...]
    first = SUBLANE - (CONV_W - 1)
    xc = xb_ref[:, first:first + ck, :] * cw[0:1, :]
    for j in range(1, CONV_W):
        xc = xc + xb_ref[:, first + j:first + j + ck, :] * cw[j:j + 1, :]
    xb_ref[:, 0:SUBLANE, :] = xb_ref[:, ck:ck + SUBLANE, :]
    xc = xc * jax.nn.sigmoid(xc)

    ab = ab_ref[...]
    g_all = -jnp.exp(alog_ref[...]) * jax.nn.softplus(ab + dtb_ref[...])
    beta_all = jax.nn.sigmoid(ab)
    z = z_ref[...]
    nw = nw_ref[...]
    masks = _gdn_masks(stack_rows, ck)
    blocks = [(g, h) for g in range(nseq) for h in range(GDN_HEADS)]
    per = stack_rows // ck
    stacks = [_gdn_stack(blocks[b0:b0 + per], ck, masks, xc, g_all, beta_all, z, nw, s_ref, o_ref)
              for b0 in range(0, len(blocks), per)]
    while stacks:
        stacks = [s for s in stacks if next(s, True) is None]

    @pl.when(c == pl.num_programs(1) - 1)
    def _():
        sout_ref[...] = s_ref[...]


def _gdn(proj3, ab3, conv0, s0, cw, alog, dtb, nw, *, ck, nseq, stack_rows, odtype):
    nb, nt, _ = proj3.shape
    assert (nseq * GDN_HEADS * ck) % stack_rows == 0 and stack_rows % ck == 0
    qblk = COL_QKV // GDN_QK
    const = lambda i, c: (0, 0)
    return pl.pallas_call(
        functools.partial(_gdn_kernel, ck=ck, nseq=nseq, stack_rows=stack_rows),
        grid=(nb // nseq, nt // ck),
        in_specs=[
            pl.BlockSpec((nseq, ck, GDN_QK), lambda i, c: (i, c, qblk)),
            pl.BlockSpec((nseq, ck, GDN_QK), lambda i, c: (i, c, qblk + 1)),
            pl.BlockSpec((nseq, ck, GDN_V), lambda i, c: (i, c, qblk + 2)),
            pl.BlockSpec((nseq, ck, GDN_V), lambda i, c: (i, c, COL_Z // GDN_V)),
            pl.BlockSpec((nseq, ck, N_AB), lambda i, c: (i, c, 0)),
            pl.BlockSpec((nseq, SUBLANE, QKV_WIDTH), lambda i, c: (i, 0, 0)),
            pl.BlockSpec((nseq, GDN_HEADS, GDN_DK, GDN_DV), lambda i, c: (i, 0, 0, 0)),
            pl.BlockSpec((CONV_W, QKV_WIDTH), const),
            pl.BlockSpec((1, LANE), const),
            pl.BlockSpec((1, LANE), const),
            pl.BlockSpec((1, GDN_DV), const),
        ],
        out_specs=[
            pl.BlockSpec((nseq, ck, GDN_V), lambda i, c: (i, c, 0)),
            pl.BlockSpec((nseq, GDN_HEADS, GDN_DK, GDN_DV), lambda i, c: (i, 0, 0, 0)),
        ],
        out_shape=[
            jax.ShapeDtypeStruct((nb, nt, GDN_V), odtype),
            jax.ShapeDtypeStruct((nb, GDN_HEADS, GDN_DK, GDN_DV), F32),
        ],
        scratch_shapes=[
            pltpu.VMEM((nseq, ck + SUBLANE, QKV_WIDTH), F32),
            pltpu.VMEM((nseq, GDN_HEADS, GDN_DK, GDN_DV), F32),
        ],
        compiler_params=_cparams(("parallel", "arbitrary")),
        name="gdn",
    )(proj3, proj3, proj3, proj3, ab3, conv0, s0, cw, alog, dtb, nw)


def _merge_kernel(x_ref, yrg_ref, ys5_ref, ygd_ref, grg_ref, gs5_ref, ggd_ref, wrg_ref, ws5_ref, wgd_ref,
                  wo_ref, g_ref, o_ref):
    mixed = jax.nn.sigmoid(grg_ref[...]) * _dot(yrg_ref[...].astype(BF16), wrg_ref[...])
    mixed = mixed + jax.nn.sigmoid(gs5_ref[...]) * _dot(ys5_ref[...].astype(BF16), ws5_ref[...])
    mixed = mixed + jax.nn.sigmoid(ggd_ref[...]) * _dot(ygd_ref[...].astype(BF16), wgd_ref[...])
    out = _dot(mixed.astype(BF16), wo_ref[...])
    o_ref[...] = x_ref[...] + _rms(out, g_ref[...])


def _merge(x, yrg, ys5, ygd, proj, wrg, ws5, wgd, wo, g, *, tm=512):
    m = x.shape[0]
    tm = min(tm, m)
    const = lambda i: (0, 0)
    half = D_MODEL // 2
    return pl.pallas_call(
        _merge_kernel,
        grid=(m // tm,),
        in_specs=[
            pl.BlockSpec((tm, D_MODEL), lambda i: (i, 0)),
            pl.BlockSpec((tm, half), lambda i: (i, 0)),
            pl.BlockSpec((tm, half), lambda i: (i, 0)),
            pl.BlockSpec((tm, half), lambda i: (i, 0)),
            pl.BlockSpec((tm, D_MODEL), lambda i: (i, 0)),
            pl.BlockSpec((tm, D_MODEL), lambda i: (i, 1)),
            pl.BlockSpec((tm, D_MODEL), lambda i: (i, 2)),
            pl.BlockSpec((half, D_MODEL), const),
            pl.BlockSpec((half, D_MODEL), const),
            pl.BlockSpec((half, D_MODEL), const),
            pl.BlockSpec((D_MODEL, D_MODEL), const),
            pl.BlockSpec((1, D_MODEL), const),
        ],
        out_specs=pl.BlockSpec((tm, D_MODEL), lambda i: (i, 0)),
        out_shape=jax.ShapeDtypeStruct((m, D_MODEL), F32),
        compiler_params=_cparams(("parallel",)),
        name="merge",
    )(x, yrg, ys5, ygd, proj, proj, proj, wrg, ws5, wgd, wo, g)


def _ffn_kernel(x_ref, gpre_ref, wg_ref, wu_ref, wd_ref, gpost_ref, o_ref, h_ref, acc_ref):
    j = pl.program_id(1)

    @pl.when(j == 0)
    def _():
        h_ref[...] = _rms(x_ref[...], gpre_ref[...]).astype(BF16)
        acc_ref[...] = jnp.zeros_like(acc_ref)

    h = h_ref[...]
    gate = _dot(h, wg_ref[...])
    act = (gate * jax.nn.sigmoid(gate)) * _dot(h, wu_ref[...])
    acc_ref[...] += _dot(act.astype(BF16), wd_ref[...])

    @pl.when(j == pl.num_programs(1) - 1)
    def _():
        o_ref[...] = x_ref[...] + _rms(acc_ref[...], gpost_ref[...])


def _ffn(x, gpre, wg, wu, wd, gpost, *, tm=512, tf=1408):
    m = x.shape[0]
    tm = min(tm, m)
    const = lambda i, j: (0, 0)
    return pl.pallas_call(
        _ffn_kernel,
        grid=(m // tm, D_FF // tf),
        in_specs=[
            pl.BlockSpec((tm, D_MODEL), lambda i, j: (i, 0)),
            pl.BlockSpec((1, D_MODEL), const),
            pl.BlockSpec((D_MODEL, tf), lambda i, j: (0, j)),
            pl.BlockSpec((D_MODEL, tf), lambda i, j: (0, j)),
            pl.BlockSpec((tf, D_MODEL), lambda i, j: (j, 0)),
            pl.BlockSpec((1, D_MODEL), const),
        ],
        out_specs=pl.BlockSpec((tm, D_MODEL), lambda i, j: (i, 0)),
        out_shape=jax.ShapeDtypeStruct((m, D_MODEL), F32),
        scratch_shapes=[pltpu.VMEM((tm, D_MODEL), BF16), pltpu.VMEM((tm, D_MODEL), F32)],
        compiler_params=_cparams(("parallel", "arbitrary")),
        name="ffn",
    )(x, gpre, wg, wu, wd, gpost)


def _block_diag(blocks):
    n, r, c = blocks.shape
    eye = jnp.eye(n, dtype=blocks.dtype)
    return jnp.einsum("nrc,nm->nrmc", blocks, eye).reshape(n * r, n * c)


def _prep_layer(l, p):
    f32 = F32
    w_in = p["w_in"][l]
    sizes = (RG_WIDTH, RG_WIDTH, S5_WIDTH, QKV_WIDTH, GDN_V, GDN_HEADS, GDN_HEADS, 3 * D_MODEL)
    offs = [0]
    for s in sizes:
        offs.append(offs[-1] + s)
    seg = lambda i: w_in[:, offs[i]:offs[i + 1]]
    w_main = jnp.concatenate([seg(7), seg(0), seg(1), seg(2), seg(3), seg(4)], axis=1).astype(BF16)
    w_ab = jnp.concatenate(
        [seg(5), seg(6), jnp.zeros((D_MODEL, N_AB - 2 * GDN_HEADS), f32)], axis=1).astype(BF16)

    pair = LANE // RG_BLOCK
    wa = p["rg_wa"][l].reshape(RG_WIDTH // LANE, pair, RG_BLOCK, RG_BLOCK)
    wx = p["rg_wx"][l].reshape(RG_WIDTH // LANE, pair, RG_BLOCK, RG_BLOCK)
    wg = jnp.stack([jnp.concatenate([_block_diag(wa[i]), _block_diag(wx[i])], axis=1)
                    for i in range(RG_WIDTH // LANE)]).astype(BF16)

    a_re = p["s5_a_re"][l].astype(f32)
    a_im = p["s5_a_im"][l].astype(f32)
    dt = jnp.exp(p["s5_log_dt"][l].astype(f32))[:, None]
    mag = jnp.exp(a_re * dt)
    lr = mag * jnp.cos(a_im * dt)
    li = mag * jnp.sin(a_im * dt)
    den = a_re * a_re + a_im * a_im
    cr = ((lr - 1.0) * a_re + li * a_im) / den
    ci = (li * a_re - (lr - 1.0) * a_im) / den
    b_re = p["s5_b_re"][l].astype(f32)
    b_im = p["s5_b_im"][l].astype(f32)
    bb_re = cr[..., None] * b_re - ci[..., None] * b_im
    bb_im = cr[..., None] * b_im + ci[..., None] * b_re
    nq = S5_WIDTH // LANE
    gq = S5_GROUPS // nq
    to_in = lambda a: jnp.swapaxes(a, 1, 2).reshape(nq, gq, S5_GROUP, S5_STATE)
    bre = jnp.stack([_block_diag(to_in(bb_re)[i]) for i in range(nq)]).astype(BF16)
    bim = jnp.stack([_block_diag(to_in(bb_im)[i]) for i in range(nq)]).astype(BF16)
    to_out = lambda a: jnp.swapaxes(a.astype(f32), 1, 2).reshape(nq, gq, S5_STATE, S5_GROUP)
    cre = jnp.stack([_block_diag(to_out(p["s5_c_re"][l])[i]) for i in range(nq)]).astype(BF16)
    cim = jnp.stack([_block_diag(to_out(p["s5_c_im"][l])[i]) for i in range(nq)]).astype(BF16)

    pad4 = lambda a: jnp.pad(a.astype(f32), (0, LANE - GDN_HEADS)).reshape(1, LANE)
    row = lambda a: a.astype(f32).reshape(1, -1)
    return dict(
        g_pre_mix=row(p["norm_pre_mix"][l]), g_post_mix=row(p["norm_post_mix"][l]),
        g_pre_ffn=row(p["norm_pre_ffn"][l]), g_post_ffn=row(p["norm_post_ffn"][l]),
        w_main=w_main, w_ab=w_ab,
        rg_cw=p["rg_conv_w"][l].astype(f32), rg_cb=row(p["rg_conv_b"][l]), rg_wg=wg,
        rg_ba=row(p["rg_ba"][l]), rg_bx=row(p["rg_bx"][l]), rg_lam=row(p["rg_lambda"][l]),
        rg_wo=p["rg_w_out"][l].astype(BF16),
        s5_lr=lr.reshape(1, S5_FLAT), s5_li=li.reshape(1, S5_FLAT),
        s5_bre=bre, s5_bim=bim, s5_cre=cre, s5_cim=cim, s5_d=row(p["s5_d"][l]),
        s5_gw=p["s5_glu_w"][l].astype(BF16), s5_gb=row(p["s5_glu_b"][l]), s5_wo=p["s5_w_out"][l].astype(BF16),
        gd_cw=p["gdn_conv_w"][l].astype(f32), gd_alog=pad4(p["gdn_a_log"][l]), gd_dtb=pad4(p["gdn_dt_bias"][l]),
        gd_nw=row(p["gdn_norm_w"][l]), gd_wo=p["gdn_w_out"][l].astype(BF16),
        w_out=p["w_out"][l].astype(BF16),
        ffn_wg=p["ffn_w_gate"][l].astype(BF16), ffn_wu=p["ffn_w_up"][l].astype(BF16),
        ffn_wd=p["ffn_w_down"][l].astype(BF16),
    )


def _pad_conv_state(s):
    return jnp.pad(s, ((0, 0), (SUBLANE - (CONV_W - 1), 0), (0, 0)))


def _run_group(x, states, layers, *, tc, ck, gdn_nseq, gdn_stack_rows, tiles_per_pass):
    nb, nt, _ = x.shape
    keep = CONV_W - 1
    assert nt >= keep and nt % SUBLANE == 0 and nb % SUBLANE == 0
    st_rg_conv, st_rg_h, st_s5_re, st_s5_im, st_gd_conv, st_gd_s = states
    ydtype = BF16 if (tc % BF16_ROWS == 0 and ck % BF16_ROWS == 0) else F32
    xf = x.reshape(nb * nt, D_MODEL)
    outs = [[] for _ in range(6)]
    for l, w in enumerate(layers):
        proj, ab = _in_proj(xf, w["g_pre_mix"], w["w_main"], w["w_ab"])
        proj3 = proj.reshape(nb, nt, N_MAIN)

        y_rg, h_last = _rglru(proj3, _pad_conv_state(st_rg_conv[l]), st_rg_h[l], w["rg_cw"], w["rg_cb"],
                              w["rg_wg"], w["rg_ba"], w["rg_bx"], w["rg_lam"], tc=tc, ydtype=ydtype)

        y_s5, s_re, s_im = _s5(proj3, st_s5_re[l].reshape(nb, S5_FLAT), st_s5_im[l].reshape(nb, S5_FLAT),
                               w["s5_lr"], w["s5_li"], w["s5_bre"], w["s5_bim"], w["s5_cre"], w["s5_cim"],
                               w["s5_d"], w["s5_gw"], w["s5_gb"], tc=tc, tiles_per_pass=tiles_per_pass,
                               ydtype=ydtype)

        y_gd, s_gd = _gdn(proj3, ab.reshape(nb, nt, N_AB), _pad_conv_state(st_gd_conv[l]), st_gd_s[l],
                          w["gd_cw"], w["gd_alog"], w["gd_dtb"], w["gd_nw"], ck=ck, nseq=gdn_nseq,
                          stack_rows=gdn_stack_rows, odtype=ydtype)

        flat = lambda a: a.reshape(nb * nt, a.shape[-1])
        x1 = _merge(xf, flat(y_rg), flat(y_s5), flat(y_gd), proj, w["rg_wo"], w["s5_wo"], w["gd_wo"],
                    w["w_out"], w["g_post_mix"])
        xf = _ffn(x1, w["g_pre_ffn"], w["ffn_wg"], w["ffn_wu"], w["ffn_wd"], w["g_post_ffn"])

        outs[0].append(proj3[:, nt - keep:, COL_RG:COL_RG + RG_WIDTH])
        outs[1].append(h_last)
        outs[2].append(s_re.reshape(nb, S5_GROUPS, S5_STATE))
        outs[3].append(s_im.reshape(nb, S5_GROUPS, S5_STATE))
        outs[4].append(proj3[:, nt - keep:, COL_QKV:COL_QKV + QKV_WIDTH])
        outs[5].append(s_gd)
    return xf.reshape(nb, nt, D_MODEL), [jnp.stack(o) for o in outs]


def _zero_states(depth, nb):
    return (jnp.zeros((depth, nb, CONV_W - 1, RG_WIDTH), F32),
            jnp.zeros((depth, nb, RG_WIDTH), F32),
            jnp.zeros((depth, nb, S5_GROUPS, S5_STATE), F32),
            jnp.zeros((depth, nb, S5_GROUPS, S5_STATE), F32),
            jnp.zeros((depth, nb, CONV_W - 1, QKV_WIDTH), F32),
            jnp.zeros((depth, nb, GDN_HEADS, GDN_DK, GDN_DV), F32))


def kernel(x_prompt, x_sample, state_rg_conv, state_rg_h, state_s5_re, state_s5_im, state_gdn_conv, state_gdn_s, norm_pre_mix, norm_post_mix, norm_pre_ffn, norm_post_ffn, w_in, rg_conv_w, rg_conv_b, rg_wa, rg_ba, rg_wx, rg_bx, rg_lambda, rg_w_out, s5_a_re, s5_a_im, s5_b_re, s5_b_im, s5_c_re, s5_c_im, s5_d, s5_log_dt, s5_glu_w, s5_glu_b, s5_w_out, gdn_conv_w, gdn_a_log, gdn_dt_bias, gdn_norm_w, gdn_w_out, w_out, ffn_w_gate, ffn_w_up, ffn_w_down):
    p = dict(norm_pre_mix=norm_pre_mix, norm_post_mix=norm_post_mix, norm_pre_ffn=norm_pre_ffn,
             norm_post_ffn=norm_post_ffn, w_in=w_in, rg_conv_w=rg_conv_w, rg_conv_b=rg_conv_b, rg_wa=rg_wa,
             rg_ba=rg_ba, rg_wx=rg_wx, rg_bx=rg_bx, rg_lambda=rg_lambda, rg_w_out=rg_w_out, s5_a_re=s5_a_re,
             s5_a_im=s5_a_im, s5_b_re=s5_b_re, s5_b_im=s5_b_im, s5_c_re=s5_c_re, s5_c_im=s5_c_im, s5_d=s5_d,
             s5_log_dt=s5_log_dt, s5_glu_w=s5_glu_w, s5_glu_b=s5_glu_b, s5_w_out=s5_w_out,
             gdn_conv_w=gdn_conv_w, gdn_a_log=gdn_a_log, gdn_dt_bias=gdn_dt_bias, gdn_norm_w=gdn_norm_w,
             gdn_w_out=gdn_w_out, w_out=w_out, ffn_w_gate=ffn_w_gate, ffn_w_up=ffn_w_up, ffn_w_down=ffn_w_down)
    depth = w_in.shape[0]
    layers = [_prep_layer(l, p) for l in range(depth)]
    y_p, st_p = _run_group(x_prompt, _zero_states(depth, x_prompt.shape[0]), layers,
                           tc=128, ck=64, gdn_nseq=4, gdn_stack_rows=MXU_EDGE, tiles_per_pass=8)
    y_s, st_s = _run_group(
        x_sample, (state_rg_conv, state_rg_h, state_s5_re, state_s5_im, state_gdn_conv, state_gdn_s), layers,
        tc=x_sample.shape[1], ck=x_sample.shape[1], gdn_nseq=16, gdn_stack_rows=MXU_EDGE, tiles_per_pass=1)
    return (y_p, y_s, *st_p, *st_s)
```

```python
import functools

import jax
import jax.numpy as jnp
from jax import lax
from jax.experimental import pallas as pl
from jax.experimental.pallas import tpu as pltpu

F32 = jnp.float32
BF16 = jnp.bfloat16

D_MODEL = 1024
CONV_W = 4
EPS = 1e-6
RG_WIDTH = 512
RG_BLOCK = 64
RG_C = 8.0
S5_WIDTH = 512
S5_GROUP = 16
S5_GROUPS = 32
S5_STATE = 64
S5_FLAT = S5_GROUPS * S5_STATE
GDN_HEADS = 4
GDN_DK = 128
GDN_DV = 128
GDN_QK = 512
GDN_V = 512
QKV_WIDTH = 1536
D_FF = 2816

LANE = 128
SUBLANE = 8
BF16_ROWS = 16
MXU_EDGE = 256
MASKED_LOG = -1e30

COL_GATES = 0
COL_RG = 3072
COL_S5 = 4096
COL_QKV = 4608
COL_Z = 6144
N_MAIN = 6656
N_AB = LANE

VMEM_LIMIT = 52 * 1024 * 1024


def _cparams(sem):
    return pltpu.CompilerParams(dimension_semantics=sem, vmem_limit_bytes=VMEM_LIMIT)


def _rms(x, g):
    return x * lax.rsqrt(jnp.mean(x * x, axis=-1, keepdims=True) + EPS) * g


def _dot(a, b):
    return jnp.dot(a, b, preferred_element_type=F32)


def _dot_nt(a, b):
    return lax.dot_general(a, b, (((1,), (1,)), ((), ())), preferred_element_type=F32)


def _dot_tn(a, b):
    return lax.dot_general(a, b, (((0,), (0,)), ((), ())), preferred_element_type=F32)


def _split2(a):
    hi = a.astype(BF16)
    return hi, (a - hi.astype(F32)).astype(BF16)


def _dot_exact_lhs(a, b):
    b1 = b.astype(BF16)
    r = b - b1.astype(F32)
    b2 = r.astype(BF16)
    b3 = (r - b2.astype(F32)).astype(BF16)
    return _dot(a, b1) + (_dot(a, b2) + _dot(a, b3))


def _dot_split(a, b):
    a1, a2 = _split2(a)
    b1, b2 = _split2(b)
    return _dot(a1, b1) + (_dot(a1, b2) + _dot(a2, b1))


def _step_pitch(nt_chunk):
    return nt_chunk if nt_chunk <= SUBLANE else nt_chunk + SUBLANE


def _put_rows(ref, tile, val, nb, tc, pitch):
    if pitch == tc:
        ref[tile, 0:nb * tc, :] = val
    else:
        for b in range(nb):
            ref[tile, b * pitch:b * pitch + tc, :] = val[b * tc:(b + 1) * tc, :]


def _get_rows(ref, tile, nb, tc, pitch):
    if pitch == tc:
        return ref[tile, 0:nb * tc, :]
    return jnp.concatenate([ref[tile, b * pitch:b * pitch + tc, :] for b in range(nb)], axis=0)


def _in_proj_kernel(x_ref, g_ref, w_ref, wab_ref, o_ref, oab_ref, h_ref):
    @pl.when(pl.program_id(1) == 0)
    def _():
        h = _rms(x_ref[...], g_ref[...]).astype(BF16)
        h_ref[...] = h
        oab_ref[...] = _dot(h, wab_ref[...])

    o_ref[...] = _dot(h_ref[...], w_ref[...])


def _in_proj(x, g, w, wab, *, tm=2048, tn=512):
    m = x.shape[0]
    tm = min(tm, m)
    return pl.pallas_call(
        _in_proj_kernel,
        grid=(m // tm, N_MAIN // tn),
        in_specs=[
            pl.BlockSpec((tm, D_MODEL), lambda i, j: (i, 0)),
            pl.BlockSpec((1, D_MODEL), lambda i, j: (0, 0)),
            pl.BlockSpec((D_MODEL, tn), lambda i, j: (0, j)),
            pl.BlockSpec((D_MODEL, N_AB), lambda i, j: (0, 0)),
        ],
        out_specs=[
            pl.BlockSpec((tm, tn), lambda i, j: (i, j)),
            pl.BlockSpec((tm, N_AB), lambda i, j: (i, 0)),
        ],
        out_shape=[jax.ShapeDtypeStruct((m, N_MAIN), F32), jax.ShapeDtypeStruct((m, N_AB), F32)],
        scratch_shapes=[pltpu.VMEM((tm, D_MODEL), BF16)],
        compiler_params=_cparams(("parallel", "arbitrary")),
        name="in_proj",
    )(x, g, w, wab)


def _rglru_kernel(p_ref, conv0_ref, h0_ref, cw_ref, cb_ref, wg_ref, ba_ref, bx_ref, lam_ref,
                  y_ref, hl_ref, xp_ref, a_ref, h_ref, hc_ref, *, nb, tc, pitch):
    ntile = RG_WIDTH // LANE
    first = SUBLANE - (CONV_W - 1)

    @pl.when(pl.program_id(0) == 0)
    def _():
        for p in range(ntile):
            xp_ref[:, p, 0:SUBLANE, :] = conv0_ref[:, :, p * LANE:(p + 1) * LANE]
        hc_ref[...] = h0_ref[...]

    cw = cw_ref[...]
    for p in range(ntile):
        cols = slice(p * LANE, (p + 1) * LANE)
        xp_ref[:, p, SUBLANE:SUBLANE + tc, :] = p_ref[:, :, cols]
        xc = cb_ref[:, cols] + xp_ref[:, p, first:first + tc, :] * cw[0:1, cols]
        for k in range(1, CONV_W):
            xc = xc + xp_ref[:, p, first + k:first + k + tc, :] * cw[k:k + 1, cols]
        xp_ref[:, p, 0:SUBLANE, :] = xp_ref[:, p, tc:tc + SUBLANE, :]
        xc = xc.reshape(nb * tc, LANE)
        gm = _dot(xc.astype(BF16), wg_ref[p])
        r = jax.nn.sigmoid(gm[:, 0:LANE] + ba_ref[:, cols])
        i = jax.nn.sigmoid(gm[:, LANE:2 * LANE] + bx_ref[:, cols])
        log_a = -RG_C * r * jax.nn.softplus(-lam_ref[:, cols])
        a = jnp.exp(log_a)
        _put_rows(a_ref, p, a, nb, tc, pitch)
        _put_rows(h_ref, p, jnp.sqrt(1.0 - a * a) * (i * xc), nb, tc, pitch)

    def step(t, hs):
        rs = pl.ds(t, nb, stride=pitch)
        out = []
        for p in range(ntile):
            h = a_ref[p, rs, :] * hs[p] + h_ref[p, rs, :]
            h_ref[p, rs, :] = h
            out.append(h)
        return tuple(out)

    hc = hc_ref[...]
    hs = lax.fori_loop(0, tc, step, tuple(hc[:, p * LANE:(p + 1) * LANE] for p in range(ntile)), unroll=2)
    for p in range(ntile):
        cols = slice(p * LANE, (p + 1) * LANE)
        hc_ref[:, cols] = hs[p]
        hl_ref[:, cols] = hs[p]
        h = _get_rows(h_ref, p, nb, tc, pitch).reshape(nb, tc, LANE)
        gate = p_ref[:, :, RG_WIDTH + p * LANE:RG_WIDTH + (p + 1) * LANE]
        y_ref[:, :, cols] = (h * jax.nn.gelu(gate)).astype(y_ref.dtype)


def _rglru(proj3, conv0, h0, cw, cb, wg, ba, bx, lam, *, tc, ydtype):
    nb, nt, _ = proj3.shape
    pitch = _step_pitch(tc)
    ntile = RG_WIDTH // LANE
    const = lambda c: (0, 0)
    const3 = lambda c: (0, 0, 0)
    return pl.pallas_call(
        functools.partial(_rglru_kernel, nb=nb, tc=tc, pitch=pitch),
        grid=(nt // tc,),
        in_specs=[
            pl.BlockSpec((nb, tc, 2 * RG_WIDTH), lambda c: (0, c, COL_RG // (2 * RG_WIDTH))),
            pl.BlockSpec((nb, SUBLANE, RG_WIDTH), const3),
            pl.BlockSpec((nb, RG_WIDTH), const),
            pl.BlockSpec((CONV_W, RG_WIDTH), const),
            pl.BlockSpec((1, RG_WIDTH), const),
            pl.BlockSpec((ntile, LANE, 2 * LANE), const3),
            pl.BlockSpec((1, RG_WIDTH), const),
            pl.BlockSpec((1, RG_WIDTH), const),
            pl.BlockSpec((1, RG_WIDTH), const),
        ],
        out_specs=[
            pl.BlockSpec((nb, tc, RG_WIDTH), lambda c: (0, c, 0)),
            pl.BlockSpec((nb, RG_WIDTH), const),
        ],
        out_shape=[jax.ShapeDtypeStruct((nb, nt, RG_WIDTH), ydtype), jax.ShapeDtypeStruct((nb, RG_WIDTH), F32)],
        scratch_shapes=[
            pltpu.VMEM((nb, ntile, tc + SUBLANE, LANE), F32),
            pltpu.VMEM((ntile, nb * pitch, LANE), F32),
            pltpu.VMEM((ntile, nb * pitch, LANE), F32),
            pltpu.VMEM((nb, RG_WIDTH), F32),
        ],
        compiler_params=_cparams(("arbitrary",)),
        name="rglru",
    )(proj3, conv0, h0, cw, cb, wg, ba, bx, lam)


def _s5_kernel(u_ref, s0r_ref, s0i_ref, lr_ref, li_ref, bre_ref, bim_ref, cre_ref, cim_ref, d_ref, gw_ref, gb_ref,
               y_ref, sr_out, si_out, sre_ref, sim_ref, src_ref, sic_ref, *, nb, tc, pitch, tiles_per_pass):
    nq = S5_WIDTH // LANE
    qt = S5_FLAT // nq // LANE
    ntile = S5_FLAT // LANE

    @pl.when(pl.program_id(0) == 0)
    def _():
        src_ref[...] = s0r_ref[...]
        sic_ref[...] = s0i_ref[...]

    u = u_ref[...].reshape(nb * tc, S5_WIDTH)
    ub = u.astype(BF16)
    for q in range(nq):
        uq = ub[:, q * LANE:(q + 1) * LANE]
        bur = _dot(uq, bre_ref[q])
        bui = _dot(uq, bim_ref[q])
        for j in range(qt):
            _put_rows(sre_ref, q * qt + j, bur[:, j * LANE:(j + 1) * LANE], nb, tc, pitch)
            _put_rows(sim_ref, q * qt + j, bui[:, j * LANE:(j + 1) * LANE], nb, tc, pitch)

    for t0 in range(0, ntile, tiles_per_pass):
        tiles = list(range(t0, t0 + tiles_per_pass))
        lrs = [jnp.broadcast_to(lr_ref[:, tl * LANE:(tl + 1) * LANE], (nb, LANE)) for tl in tiles]
        lis = [jnp.broadcast_to(li_ref[:, tl * LANE:(tl + 1) * LANE], (nb, LANE)) for tl in tiles]

        def step(t, carry, tiles=tiles, lrs=lrs, lis=lis):
            rs = pl.ds(t, nb, stride=pitch)
            out = []
            for n, tl in enumerate(tiles):
                sr, si = carry[2 * n], carry[2 * n + 1]
                nr = lrs[n] * sr - lis[n] * si + sre_ref[tl, rs, :]
                ni = lrs[n] * si + lis[n] * sr + sim_ref[tl, rs, :]
                sre_ref[tl, rs, :] = nr
                sim_ref[tl, rs, :] = ni
                out += [nr, ni]
            return tuple(out)

        init = []
        for tl in tiles:
            init += [src_ref[:, tl * LANE:(tl + 1) * LANE], sic_ref[:, tl * LANE:(tl + 1) * LANE]]
        fin = lax.fori_loop(0, tc, step, tuple(init), unroll=2)
        for n, tl in enumerate(tiles):
            src_ref[:, tl * LANE:(tl + 1) * LANE] = fin[2 * n]
            sic_ref[:, tl * LANE:(tl + 1) * LANE] = fin[2 * n + 1]

    sr_out[...] = src_ref[...]
    si_out[...] = sic_ref[...]

    ys = []
    for q in range(nq):
        sr_q = jnp.concatenate([_get_rows(sre_ref, q * qt + j, nb, tc, pitch) for j in range(qt)], axis=1)
        si_q = jnp.concatenate([_get_rows(sim_ref, q * qt + j, nb, tc, pitch) for j in range(qt)], axis=1)
        ys.append(_dot(sr_q.astype(BF16), cre_ref[q]) - _dot(si_q.astype(BF16), cim_ref[q]))
    y = jnp.concatenate(ys, axis=1) + d_ref[...] * u
    y = jax.nn.gelu(y)
    y = y * jax.nn.sigmoid(_dot(y.astype(BF16), gw_ref[...]) + gb_ref[...])
    y_ref[...] = y.reshape(nb, tc, S5_WIDTH).astype(y_ref.dtype)


def _s5(proj3, s0r, s0i, lr, li, bre, bim, cre, cim, d, gw, gb, *, tc, tiles_per_pass, ydtype):
    nb, nt, _ = proj3.shape
    pitch = _step_pitch(tc)
    nq = S5_WIDTH // LANE
    qs = S5_FLAT // nq
    ntile = S5_FLAT // LANE
    const = lambda c: (0, 0)
    const3 = lambda c: (0, 0, 0)
    return pl.pallas_call(
        functools.partial(_s5_kernel, nb=nb, tc=tc, pitch=pitch, tiles_per_pass=tiles_per_pass),
        grid=(nt // tc,),
        in_specs=[
            pl.BlockSpec((nb, tc, S5_WIDTH), lambda c: (0, c, COL_S5 // S5_WIDTH)),
            pl.BlockSpec((nb, S5_FLAT), const),
            pl.BlockSpec((nb, S5_FLAT), const),
            pl.BlockSpec((1, S5_FLAT), const),
            pl.BlockSpec((1, S5_FLAT), const),
            pl.BlockSpec((nq, LANE, qs), const3),
            pl.BlockSpec((nq, LANE, qs), const3),
            pl.BlockSpec((nq, qs, LANE), const3),
            pl.BlockSpec((nq, qs, LANE), const3),
            pl.BlockSpec((1, S5_WIDTH), const),
            pl.BlockSpec((S5_WIDTH, S5_WIDTH), const),
            pl.BlockSpec((1, S5_WIDTH), const),
        ],
        out_specs=[
            pl.BlockSpec((nb, tc, S5_WIDTH), lambda c: (0, c, 0)),
            pl.BlockSpec((nb, S5_FLAT), const),
            pl.BlockSpec((nb, S5_FLAT), const),
        ],
        out_shape=[
            jax.ShapeDtypeStruct((nb, nt, S5_WIDTH), ydtype),
            jax.ShapeDtypeStruct((nb, S5_FLAT), F32),
            jax.ShapeDtypeStruct((nb, S5_FLAT), F32),
        ],
        scratch_shapes=[
            pltpu.VMEM((ntile, nb * pitch, LANE), F32),
            pltpu.VMEM((ntile, nb * pitch, LANE), F32),
            pltpu.VMEM((nb, S5_FLAT), F32),
            pltpu.VMEM((nb, S5_FLAT), F32),
        ],
        compiler_params=_cparams(("arbitrary",)),
        name="s5",
    )(proj3, s0r, s0i, lr, li, bre, bim, cre, cim, d, gw, gb)


def _gdn_masks(sr, ck):
    shift = ck.bit_length() - 1
    row = lax.broadcasted_iota(jnp.int32, (sr, sr), 0)
    col = lax.broadcasted_iota(jnp.int32, (sr, sr), 1)
    same = (row >> shift) == (col >> shift)
    causal = same & (row >= col)
    strict = same & (row > col)
    tril = jnp.where(causal, 1.0, 0.0).astype(BF16)
    eye = jnp.where(row == col, 1.0, 0.0).astype(F32)
    return causal, strict, tril, eye


def _gdn_stack(blocks, ck, masks, xc, g_all, beta_all, z, nw, s_ref, o_ref):
    nblk = len(blocks)
    sr = nblk * ck
    causal, strict, tril, eye = masks

    def stack(fn):
        return jnp.concatenate([fn(g, h) for g, h in blocks], axis=0)

    q = stack(lambda g, h: xc[h][g])
    k = stack(lambda g, h: xc[GDN_HEADS + h][g])
    v = stack(lambda g, h: xc[2 * GDN_HEADS + h][g])
    zs = stack(lambda g, h: z[g, :, h * GDN_DV:(h + 1) * GDN_DV])
    gcol = stack(lambda g, h: jnp.broadcast_to(g_all[g, :, h:h + 1], (ck, LANE)))
    beta = stack(lambda g, h: jnp.broadcast_to(beta_all[g, :, GDN_HEADS + h:GDN_HEADS + h + 1], (ck, LANE)))

    q = q * lax.rsqrt(jnp.sum(q * q, axis=-1, keepdims=True) + EPS) * (GDN_DK ** -0.5)
    k = k * lax.rsqrt(jnp.sum(k * k, axis=-1, keepdims=True) + EPS)
    kb = k * beta
    vb = v * beta
    yield

    gc = _dot_exact_lhs(tril, gcol)
    kbf = k.astype(BF16)
    kk = _dot_nt(kb.astype(BF16), kbf)
    qk = _dot_nt(q.astype(BF16), kbf)
    yield
    gc_lanes = jnp.transpose(gc)[0:SUBLANE, :]
    gc_rows = jnp.concatenate([gc] * (sr // LANE), axis=1) if sr > LANE else gc[:, 0:sr]
    seg = gc_rows - jnp.concatenate([gc_lanes] * (sr // SUBLANE), axis=0)
    decay = jnp.exp(jnp.where(causal, seg, MASKED_LOG))

    lmat = jnp.where(strict, kk * decay, 0.0)
    qk = (qk * decay).astype(BF16)

    x = -lmat
    p = _dot_split(lmat, lmat)
    yield
    x = x + p + _dot(x.astype(BF16), p.astype(BF16))
    n = 4
    while n < ck:
        pb = p.astype(BF16)
        p = _dot(pb, pb)
        yield
        x = x + p + _dot(x.astype(BF16), p.astype(BF16))
        yield
        n *= 2
    tmat = (eye + x).astype(BF16)
    egc = jnp.exp(gc)
    uw = _dot(tmat, jnp.concatenate([vb, kb * egc], axis=1).astype(BF16))
    yield
    u = uw[:, 0:GDN_DV]
    w = uw[:, GDN_DV:GDN_DV + GDN_DK]
    qg = q * egc
    g_last = jnp.concatenate(
        [jnp.broadcast_to(gc[(n + 1) * ck - 1:(n + 1) * ck, :], (ck, LANE)) for n in range(nblk)], axis=0)
    k_dec = (k * jnp.exp(g_last - gc)).astype(BF16)

    v_new, qs = [], []
    for n, (g, h) in enumerate(blocks):
        rs = slice(n * ck, (n + 1) * ck)
        sb = s_ref[g, h].astype(BF16)
        r = _dot(jnp.concatenate([w[rs], qg[rs]], axis=0).astype(BF16), sb)
        v_new.append(u[rs] - r[0:ck])
        qs.append(r[ck:2 * ck])
    yield
    v_new = jnp.concatenate(v_new, axis=0)
    vnb = v_new.astype(BF16)
    o = jnp.concatenate(qs, axis=0) + _dot(qk, vnb)
    for n, (g, h) in enumerate(blocks):
        rs = slice(n * ck, (n + 1) * ck)
        s_ref[g, h] = s_ref[g, h] * jnp.exp(gc[(n + 1) * ck - 1:(n + 1) * ck, :]) + _dot_tn(k_dec[rs], vnb[rs])
    yield

    o = _rms(o, nw) * (zs * jax.nn.sigmoid(zs))
    for n, (g, h) in enumerate(blocks):
        o_ref[g, :, h * GDN_DV:(h + 1) * GDN_DV] = o[n * ck:(n + 1) * ck].astype(o_ref.dtype)


def _gdn_kernel(q_ref, k_ref, v_ref, z_ref, ab_ref, conv0_ref, s0_ref, cw_ref, alog_ref, dtb_ref, nw_ref,
                *rest, ck, nseq, stack_rows):
    o_ref, sout_ref, xb_ref, s_ref = rest[-4:]
    c = pl.program_id(1)

    ntile = QKV_WIDTH // LANE
    per_src = GDN_QK // LANE

    @pl.when(c == 0)
    def _():
        for lt in range(ntile):
            xb_ref[:, lt, 0:SUBLANE, :] = conv0_ref[:, :, lt * LANE:(lt + 1) * LANE]
        s_ref[...] = s0_ref[...]

    cw = cw_ref[...]
    first = SUBLANE - (CONV_W - 1)
    xc = []
    for lt in range(ntile):
        src = (q_ref, k_ref, v_ref)[lt // per_src]
        xb_ref[:, lt, SUBLANE:SUBLANE + ck, :] = src[:, :, (lt % per_src) * LANE:(lt % per_src + 1) * LANE]
        acc = xb_ref[:, lt, first:first + ck, :] * cw[0:1, lt * LANE:(lt + 1) * LANE]
        for j in range(1, CONV_W):
            acc = acc + xb_ref[:, lt, first + j:first + j + ck, :] * cw[j:j + 1, lt * LANE:(lt + 1) * LANE]
        xb_ref[:, lt, 0:SUBLANE, :] = xb_ref[:, lt, ck:ck + SUBLANE, :]
        xc.append(acc * jax.nn.sigmoid(acc))

    ab = ab_ref[...]
    g_all = -jnp.exp(alog_ref[...]) * jax.nn.softplus(ab + dtb_ref[...])
    beta_all = jax.nn.sigmoid(ab)
    z = z_ref[...]
    nw = nw_ref[...]
    masks = _gdn_masks(stack_rows, ck)
    blocks = [(g, h) for g in range(nseq) for h in range(GDN_HEADS)]
    per = stack_rows // ck
    stacks = [_gdn_stack(blocks[b0:b0 + per], ck, masks, xc, g_all, beta_all, z, nw, s_ref, o_ref)
              for b0 in range(0, len(blocks), per)]
    while stacks:
        stacks = [s for s in stacks if next(s, True) is None]

    @pl.when(c == pl.num_programs(1) - 1)
    def _():
        sout_ref[...] = s_ref[...]


def _gdn(proj3, ab3, conv0, s_all, s_done, layer, cw, alog, dtb, nw, *, ck, nseq, stack_rows, odtype):
    nb, nt, _ = proj3.shape
    assert (nseq * GDN_HEADS * ck) % stack_rows == 0 and stack_rows % ck == 0
    qblk = COL_QKV // GDN_QK
    const = lambda i, c: (0, 0)
    state_spec = pl.BlockSpec((None, nseq, GDN_HEADS, GDN_DK, GDN_DV), lambda i, c: (layer, i, 0, 0, 0))
    operands = [proj3, proj3, proj3, proj3, ab3, conv0, s_all, cw, alog, dtb, nw]
    extra_specs, aliases = [], {}
    if s_done is not None:
        aliases = {len(operands): 1}
        operands.append(s_done)
        extra_specs = [pl.BlockSpec(memory_space=pl.ANY)]
    return pl.pallas_call(
        functools.partial(_gdn_kernel, ck=ck, nseq=nseq, stack_rows=stack_rows),
        grid=(nb // nseq, nt // ck),
        input_output_aliases=aliases,
        in_specs=[
            pl.BlockSpec((nseq, ck, GDN_QK), lambda i, c: (i, c, qblk)),
            pl.BlockSpec((nseq, ck, GDN_QK), lambda i, c: (i, c, qblk + 1)),
            pl.BlockSpec((nseq, ck, GDN_V), lambda i, c: (i, c, qblk + 2)),
            pl.BlockSpec((nseq, ck, GDN_V), lambda i, c: (i, c, COL_Z // GDN_V)),
            pl.BlockSpec((nseq, ck, N_AB), lambda i, c: (i, c, 0)),
            pl.BlockSpec((nseq, SUBLANE, QKV_WIDTH), lambda i, c: (i, 0, 0)),
            state_spec,
            pl.BlockSpec((CONV_W, QKV_WIDTH), const),
            pl.BlockSpec((1, LANE), const),
            pl.BlockSpec((1, LANE), const),
            pl.BlockSpec((1, GDN_DV), const),
        ] + extra_specs,
        out_specs=[
            pl.BlockSpec((nseq, ck, GDN_V), lambda i, c: (i, c, 0)),
            state_spec,
        ],
        out_shape=[
            jax.ShapeDtypeStruct((nb, nt, GDN_V), odtype),
            jax.ShapeDtypeStruct(s_all.shape, F32),
        ],
        scratch_shapes=[
            pltpu.VMEM((nseq, QKV_WIDTH // LANE, ck + SUBLANE, LANE), F32),
            pltpu.VMEM((nseq, GDN_HEADS, GDN_DK, GDN_DV), F32),
        ],
        compiler_params=_cparams(("parallel", "arbitrary")),
        name="gdn",
    )(*operands)


def _merge_kernel(x_ref, yrg_ref, ys5_ref, ygd_ref, grg_ref, gs5_ref, ggd_ref, wrg_ref, ws5_ref, wgd_ref,
                  wo_ref, g_ref, o_ref):
    mixed = jax.nn.sigmoid(grg_ref[...]) * _dot(yrg_ref[...].astype(BF16), wrg_ref[...])
    mixed = mixed + jax.nn.sigmoid(gs5_ref[...]) * _dot(ys5_ref[...].astype(BF16), ws5_ref[...])
    mixed = mixed + jax.nn.sigmoid(ggd_ref[...]) * _dot(ygd_ref[...].astype(BF16), wgd_ref[...])
    out = _dot(mixed.astype(BF16), wo_ref[...])
    o_ref[...] = x_ref[...] + _rms(out, g_ref[...])


def _merge(x, yrg, ys5, ygd, proj, wrg, ws5, wgd, wo, g, *, tm=512):
    m = x.shape[0]
    tm = min(tm, m)
    const = lambda i: (0, 0)
    half = D_MODEL // 2
    return pl.pallas_call(
        _merge_kernel,
        grid=(m // tm,),
        in_specs=[
            pl.BlockSpec((tm, D_MODEL), lambda i: (i, 0)),
            pl.BlockSpec((tm, half), lambda i: (i, 0)),
            pl.BlockSpec((tm, half), lambda i: (i, 0)),
            pl.BlockSpec((tm, half), lambda i: (i, 0)),
            pl.BlockSpec((tm, D_MODEL), lambda i: (i, 0)),
            pl.BlockSpec((tm, D_MODEL), lambda i: (i, 1)),
            pl.BlockSpec((tm, D_MODEL), lambda i: (i, 2)),
            pl.BlockSpec((half, D_MODEL), const),
            pl.BlockSpec((half, D_MODEL), const),
            pl.BlockSpec((half, D_MODEL), const),
            pl.BlockSpec((D_MODEL, D_MODEL), const),
            pl.BlockSpec((1, D_MODEL), const),
        ],
        out_specs=pl.BlockSpec((tm, D_MODEL), lambda i: (i, 0)),
        out_shape=jax.ShapeDtypeStruct((m, D_MODEL), F32),
        compiler_params=_cparams(("parallel",)),
        name="merge",
    )(x, yrg, ys5, ygd, proj, proj, proj, wrg, ws5, wgd, wo, g)


def _ffn_kernel(x_ref, gpre_ref, wg_ref, wu_ref, wd_ref, gpost_ref, o_ref, h_ref, acc_ref):
    j = pl.program_id(1)

    @pl.when(j == 0)
    def _():
        h_ref[...] = _rms(x_ref[...], gpre_ref[...]).astype(BF16)
        acc_ref[...] = jnp.zeros_like(acc_ref)

    h = h_ref[...]
    gate = _dot(h, wg_ref[...])
    act = (gate * jax.nn.sigmoid(gate)) * _dot(h, wu_ref[...])
    acc_ref[...] += _dot(act.astype(BF16), wd_ref[...])

    @pl.when(j == pl.num_programs(1) - 1)
    def _():
        o_ref[...] = x_ref[...] + _rms(acc_ref[...], gpost_ref[...])


def _ffn(x, gpre, wg, wu, wd, gpost, *, tm=1024, tf=256):
    m = x.shape[0]
    tm = min(tm, m)
    const = lambda i, j: (0, 0)
    return pl.pallas_call(
        _ffn_kernel,
        grid=(m // tm, D_FF // tf),
        in_specs=[
            pl.BlockSpec((tm, D_MODEL), lambda i, j: (i, 0)),
            pl.BlockSpec((1, D_MODEL), const),
            pl.BlockSpec((D_MODEL, tf), lambda i, j: (0, j)),
            pl.BlockSpec((D_MODEL, tf), lambda i, j: (0, j)),
            pl.BlockSpec((tf, D_MODEL), lambda i, j: (j, 0)),
            pl.BlockSpec((1, D_MODEL), const),
        ],
        out_specs=pl.BlockSpec((tm, D_MODEL), lambda i, j: (i, 0)),
        out_shape=jax.ShapeDtypeStruct((m, D_MODEL), F32),
        scratch_shapes=[pltpu.VMEM((tm, D_MODEL), BF16), pltpu.VMEM((tm, D_MODEL), F32)],
        compiler_params=_cparams(("parallel", "arbitrary")),
        name="ffn",
    )(x, gpre, wg, wu, wd, gpost)


def _block_diag(blocks):
    n, r, c = blocks.shape
    eye = jnp.eye(n, dtype=blocks.dtype)
    return jnp.einsum("nrc,nm->nrmc", blocks, eye).reshape(n * r, n * c)


def _prep_layer(l, p):
    f32 = F32
    w_in = p["w_in"][l]
    sizes = (RG_WIDTH, RG_WIDTH, S5_WIDTH, QKV_WIDTH, GDN_V, GDN_HEADS, GDN_HEADS, 3 * D_MODEL)
    offs = [0]
    for s in sizes:
        offs.append(offs[-1] + s)
    seg = lambda i: w_in[:, offs[i]:offs[i + 1]]
    w_main = jnp.concatenate([seg(7), seg(0), seg(1), seg(2), seg(3), seg(4)], axis=1).astype(BF16)
    w_ab = jnp.concatenate(
        [seg(5), seg(6), jnp.zeros((D_MODEL, N_AB - 2 * GDN_HEADS), f32)], axis=1).astype(BF16)

    pair = LANE // RG_BLOCK
    wa = p["rg_wa"][l].reshape(RG_WIDTH // LANE, pair, RG_BLOCK, RG_BLOCK)
    wx = p["rg_wx"][l].reshape(RG_WIDTH // LANE, pair, RG_BLOCK, RG_BLOCK)
    wg = jnp.stack([jnp.concatenate([_block_diag(wa[i]), _block_diag(wx[i])], axis=1)
                    for i in range(RG_WIDTH // LANE)]).astype(BF16)

    a_re = p["s5_a_re"][l].astype(f32)
    a_im = p["s5_a_im"][l].astype(f32)
    dt = jnp.exp(p["s5_log_dt"][l].astype(f32))[:, None]
    mag = jnp.exp(a_re * dt)
    lr = mag * jnp.cos(a_im * dt)
    li = mag * jnp.sin(a_im * dt)
    den = a_re * a_re + a_im * a_im
    cr = ((lr - 1.0) * a_re + li * a_im) / den
    ci = (li * a_re - (lr - 1.0) * a_im) / den
    b_re = p["s5_b_re"][l].astype(f32)
    b_im = p["s5_b_im"][l].astype(f32)
    bb_re = cr[..., None] * b_re - ci[..., None] * b_im
    bb_im = cr[..., None] * b_im + ci[..., None] * b_re
    nq = S5_WIDTH // LANE
    gq = S5_GROUPS // nq
    to_in = lambda a: jnp.swapaxes(a, 1, 2).reshape(nq, gq, S5_GROUP, S5_STATE)
    bre = jnp.stack([_block_diag(to_in(bb_re)[i]) for i in range(nq)]).astype(BF16)
    bim = jnp.stack([_block_diag(to_in(bb_im)[i]) for i in range(nq)]).astype(BF16)
    to_out = lambda a: jnp.swapaxes(a.astype(f32), 1, 2).reshape(nq, gq, S5_STATE, S5_GROUP)
    cre = jnp.stack([_block_diag(to_out(p["s5_c_re"][l])[i]) for i in range(nq)]).astype(BF16)
    cim = jnp.stack([_block_diag(to_out(p["s5_c_im"][l])[i]) for i in range(nq)]).astype(BF16)

    pad4 = lambda a: jnp.pad(a.astype(f32), (0, LANE - GDN_HEADS)).reshape(1, LANE)
    row = lambda a: a.astype(f32).reshape(1, -1)
    return dict(
        g_pre_mix=row(p["norm_pre_mix"][l]), g_post_mix=row(p["norm_post_mix"][l]),
        g_pre_ffn=row(p["norm_pre_ffn"][l]), g_post_ffn=row(p["norm_post_ffn"][l]),
        w_main=w_main, w_ab=w_ab,
        rg_cw=p["rg_conv_w"][l].astype(f32), rg_cb=row(p["rg_conv_b"][l]), rg_wg=wg,
        rg_ba=row(p["rg_ba"][l]), rg_bx=row(p["rg_bx"][l]), rg_lam=row(p["rg_lambda"][l]),
        rg_wo=p["rg_w_out"][l].astype(BF16),
        s5_lr=lr.reshape(1, S5_FLAT), s5_li=li.reshape(1, S5_FLAT),
        s5_bre=bre, s5_bim=bim, s5_cre=cre, s5_cim=cim, s5_d=row(p["s5_d"][l]),
        s5_gw=p["s5_glu_w"][l].astype(BF16), s5_gb=row(p["s5_glu_b"][l]), s5_wo=p["s5_w_out"][l].astype(BF16),
        gd_cw=p["gdn_conv_w"][l].astype(f32), gd_alog=pad4(p["gdn_a_log"][l]), gd_dtb=pad4(p["gdn_dt_bias"][l]),
        gd_nw=row(p["gdn_norm_w"][l]), gd_wo=p["gdn_w_out"][l].astype(BF16),
        w_out=p["w_out"][l].astype(BF16),
        ffn_wg=p["ffn_w_gate"][l].astype(BF16), ffn_wu=p["ffn_w_up"][l].astype(BF16),
        ffn_wd=p["ffn_w_down"][l].astype(BF16),
    )


def _pad_conv_state(s):
    return jnp.pad(s, ((0, 0), (SUBLANE - (CONV_W - 1), 0), (0, 0)))


def _run_group(x, states, layers, *, tc, ck, gdn_nseq, gdn_stack_rows, tiles_per_pass):
    nb, nt, _ = x.shape
    keep = CONV_W - 1
    assert nt >= keep and nt % SUBLANE == 0 and nb % SUBLANE == 0
    st_rg_conv, st_rg_h, st_s5_re, st_s5_im, st_gd_conv, st_gd_s = states
    ydtype = BF16 if (tc % BF16_ROWS == 0 and ck % BF16_ROWS == 0) else F32
    xf = x.reshape(nb * nt, D_MODEL)
    outs = [[] for _ in range(5)]
    s_gd = None
    for l, w in enumerate(layers):
        proj, ab = _in_proj(xf, w["g_pre_mix"], w["w_main"], w["w_ab"])
        proj3 = proj.reshape(nb, nt, N_MAIN)

        y_rg, h_last = _rglru(proj3, _pad_conv_state(st_rg_conv[l]), st_rg_h[l], w["rg_cw"], w["rg_cb"],
                              w["rg_wg"], w["rg_ba"], w["rg_bx"], w["rg_lam"], tc=tc, ydtype=ydtype)

        y_s5, s_re, s_im = _s5(proj3, st_s5_re[l].reshape(nb, S5_FLAT), st_s5_im[l].reshape(nb, S5_FLAT),
                               w["s5_lr"], w["s5_li"], w["s5_bre"], w["s5_bim"], w["s5_cre"], w["s5_cim"],
                               w["s5_d"], w["s5_gw"], w["s5_gb"], tc=tc, tiles_per_pass=tiles_per_pass,
                               ydtype=ydtype)

        y_gd, s_gd = _gdn(proj3, ab.reshape(nb, nt, N_AB), _pad_conv_state(st_gd_conv[l]), st_gd_s, s_gd, l,
                          w["gd_cw"], w["gd_alog"], w["gd_dtb"], w["gd_nw"], ck=ck, nseq=gdn_nseq,
                          stack_rows=gdn_stack_rows, odtype=ydtype)

        flat = lambda a: a.reshape(nb * nt, a.shape[-1])
        x1 = _merge(xf, flat(y_rg), flat(y_s5), flat(y_gd), proj, w["rg_wo"], w["s5_wo"], w["gd_wo"],
                    w["w_out"], w["g_post_mix"])
        xf = _ffn(x1, w["g_pre_ffn"], w["ffn_wg"], w["ffn_wu"], w["ffn_wd"], w["g_post_ffn"])

        outs[0].append(proj3[:, nt - keep:, COL_RG:COL_RG + RG_WIDTH])
        outs[1].append(h_last)
        outs[2].append(s_re.reshape(nb, S5_GROUPS, S5_STATE))
        outs[3].append(s_im.reshape(nb, S5_GROUPS, S5_STATE))
        outs[4].append(proj3[:, nt - keep:, COL_QKV:COL_QKV + QKV_WIDTH])
    return xf.reshape(nb, nt, D_MODEL), [jnp.stack(o) for o in outs[:5]] + [s_gd]


def _zero_states(depth, nb):
    return (jnp.zeros((depth, nb, CONV_W - 1, RG_WIDTH), F32),
            jnp.zeros((depth, nb, RG_WIDTH), F32),
            jnp.zeros((depth, nb, S5_GROUPS, S5_STATE), F32),
            jnp.zeros((depth, nb, S5_GROUPS, S5_STATE), F32),
            jnp.zeros((depth, nb, CONV_W - 1, QKV_WIDTH), F32),
            jnp.zeros((depth, nb, GDN_HEADS, GDN_DK, GDN_DV), F32))


def kernel(x_prompt, x_sample, state_rg_conv, state_rg_h, state_s5_re, state_s5_im, state_gdn_conv, state_gdn_s, norm_pre_mix, norm_post_mix, norm_pre_ffn, norm_post_ffn, w_in, rg_conv_w, rg_conv_b, rg_wa, rg_ba, rg_wx, rg_bx, rg_lambda, rg_w_out, s5_a_re, s5_a_im, s5_b_re, s5_b_im, s5_c_re, s5_c_im, s5_d, s5_log_dt, s5_glu_w, s5_glu_b, s5_w_out, gdn_conv_w, gdn_a_log, gdn_dt_bias, gdn_norm_w, gdn_w_out, w_out, ffn_w_gate, ffn_w_up, ffn_w_down):
    p = dict(norm_pre_mix=norm_pre_mix, norm_post_mix=norm_post_mix, norm_pre_ffn=norm_pre_ffn,
             norm_post_ffn=norm_post_ffn, w_in=w_in, rg_conv_w=rg_conv_w, rg_conv_b=rg_conv_b, rg_wa=rg_wa,
             rg_ba=rg_ba, rg_wx=rg_wx, rg_bx=rg_bx, rg_lambda=rg_lambda, rg_w_out=rg_w_out, s5_a_re=s5_a_re,
             s5_a_im=s5_a_im, s5_b_re=s5_b_re, s5_b_im=s5_b_im, s5_c_re=s5_c_re, s5_c_im=s5_c_im, s5_d=s5_d,
             s5_log_dt=s5_log_dt, s5_glu_w=s5_glu_w, s5_glu_b=s5_glu_b, s5_w_out=s5_w_out,
             gdn_conv_w=gdn_conv_w, gdn_a_log=gdn_a_log, gdn_dt_bias=gdn_dt_bias, gdn_norm_w=gdn_norm_w,
             gdn_w_out=gdn_w_out, w_out=w_out, ffn_w_gate=ffn_w_gate, ffn_w_up=ffn_w_up, ffn_w_down=ffn_w_down)
    depth = w_in.shape[0]
    layers = [_prep_layer(l, p) for l in range(depth)]
    y_p, st_p = _run_group(x_prompt, _zero_states(depth, x_prompt.shape[0]), layers,
                           tc=128, ck=64, gdn_nseq=4, gdn_stack_rows=MXU_EDGE, tiles_per_pass=8)
    y_s, st_s = _run_group(
        x_sample, (state_rg_conv, state_rg_h, state_s5_re, state_s5_im, state_gdn_conv, state_gdn_s), layers,
        tc=x_sample.shape[1], ck=x_sample.shape[1], gdn_nseq=16, gdn_stack_rows=MXU_EDGE, tiles_per_pass=1)
    return (y_p, y_s, *st_p, *st_s)
```

```python
import functools

import jax
import jax.numpy as jnp
from jax import lax
from jax.experimental import pallas as pl
from jax.experimental.pallas import tpu as pltpu

F32 = jnp.float32
BF16 = jnp.bfloat16

D_MODEL = 1024
CONV_W = 4
EPS = 1e-6
RG_WIDTH = 512
RG_BLOCK = 64
RG_C = 8.0
S5_WIDTH = 512
S5_GROUP = 16
S5_GROUPS = 32
S5_STATE = 64
S5_FLAT = S5_GROUPS * S5_STATE
GDN_HEADS = 4
GDN_DK = 128
GDN_DV = 128
GDN_QK = 512
GDN_V = 512
QKV_WIDTH = 1536
D_FF = 2816

LANE = 128
SUBLANE = 8
BF16_ROWS = 16
MXU_EDGE = 256
MASKED_LOG = -1e30

COL_GATES = 0
COL_RG = 3072
COL_S5 = 4096
COL_QKV = 4608
COL_Z = 6144
N_MAIN = 6656
N_AB = LANE

VMEM_LIMIT = 52 * 1024 * 1024

IN_PROJ_TM, IN_PROJ_TN = 2048, 512
MERGE_TM = 512
FFN_TM, FFN_TF = 1024, 256


def _cparams(sem):
    return pltpu.CompilerParams(dimension_semantics=sem, vmem_limit_bytes=VMEM_LIMIT)


def _lspec(layer, block, index_map):
    return pl.BlockSpec((None,) + tuple(block), lambda *g: (layer,) + tuple(index_map(*g)))


def _zeros_map(n):
    return lambda *g: (0,) * n


def _rms(x, g):
    return x * lax.rsqrt(jnp.mean(x * x, axis=-1, keepdims=True) + EPS) * g


def _dot(a, b):
    return jnp.dot(a, b, preferred_element_type=F32)


def _dot_nt(a, b):
    return lax.dot_general(a, b, (((1,), (1,)), ((), ())), preferred_element_type=F32)


def _dot_tn(a, b):
    return lax.dot_general(a, b, (((0,), (0,)), ((), ())), preferred_element_type=F32)


def _split2(a):
    hi = a.astype(BF16)
    return hi, (a - hi.astype(F32)).astype(BF16)


def _dot_exact_lhs(a, b):
    b1 = b.astype(BF16)
    r = b - b1.astype(F32)
    b2 = r.astype(BF16)
    b3 = (r - b2.astype(F32)).astype(BF16)
    return _dot(a, b1) + (_dot(a, b2) + _dot(a, b3))


def _dot_split(a, b):
    a1, a2 = _split2(a)
    b1, b2 = _split2(b)
    return _dot(a1, b1) + (_dot(a1, b2) + _dot(a2, b1))


def _step_pitch(nt_chunk):
    return nt_chunk if nt_chunk <= SUBLANE else nt_chunk + SUBLANE


def _put_rows(ref, tile, val, nb, tc, pitch):
    if pitch == tc:
        ref[tile, 0:nb * tc, :] = val
    else:
        for b in range(nb):
            ref[tile, b * pitch:b * pitch + tc, :] = val[b * tc:(b + 1) * tc, :]


def _get_rows(ref, tile, nb, tc, pitch):
    if pitch == tc:
        return ref[tile, 0:nb * tc, :]
    return jnp.concatenate([ref[tile, b * pitch:b * pitch + tc, :] for b in range(nb)], axis=0)


def _in_proj_kernel(x_ref, g_ref, w_ref, wab_ref, o_ref, oab_ref, h_ref):
    @pl.when(pl.program_id(1) == 0)
    def _():
        h = _rms(x_ref[...], g_ref[...]).astype(BF16)
        h_ref[...] = h
        oab_ref[...] = _dot(h, wab_ref[...])

    o_ref[...] = _dot(h_ref[...], w_ref[...])


def _in_proj(x, w, layer):
    m = x.shape[0]
    tm = min(IN_PROJ_TM, m)
    tn = IN_PROJ_TN
    return pl.pallas_call(
        _in_proj_kernel,
        grid=(m // tm, N_MAIN // tn),
        in_specs=[
            pl.BlockSpec((tm, D_MODEL), lambda i, j: (i, 0)),
            _lspec(layer, (1, D_MODEL), _zeros_map(2)),
            _lspec(layer, (D_MODEL, tn), lambda i, j: (0, j)),
            _lspec(layer, (D_MODEL, N_AB), _zeros_map(2)),
        ],
        out_specs=[
            pl.BlockSpec((tm, tn), lambda i, j: (i, j)),
            pl.BlockSpec((tm, N_AB), lambda i, j: (i, 0)),
        ],
        out_shape=[jax.ShapeDtypeStruct((m, N_MAIN), F32), jax.ShapeDtypeStruct((m, N_AB), F32)],
        scratch_shapes=[pltpu.VMEM((tm, D_MODEL), BF16)],
        compiler_params=_cparams(("parallel", "arbitrary")),
        name="in_proj",
    )(x, w["g_pre_mix"], w["w_main"], w["w_ab"])


def _rglru_kernel(p_ref, conv0_ref, h0_ref, cw_ref, cb_ref, wg_ref, ba_ref, bx_ref, lam_ref,
                  y_ref, hl_ref, xp_ref, a_ref, h_ref, hc_ref, *, nb, tc, pitch):
    ntile = RG_WIDTH // LANE
    first = SUBLANE - (CONV_W - 1)

    @pl.when(pl.program_id(0) == 0)
    def _():
        for p in range(ntile):
            xp_ref[:, p, 0:SUBLANE, :] = conv0_ref[:, :, p * LANE:(p + 1) * LANE]
        hc_ref[...] = h0_ref[...]

    cw = cw_ref[...]
    for p in range(ntile):
        cols = slice(p * LANE, (p + 1) * LANE)
        xp_ref[:, p, SUBLANE:SUBLANE + tc, :] = p_ref[:, :, cols]
        xc = cb_ref[:, cols] + xp_ref[:, p, first:first + tc, :] * cw[0:1, cols]
        for k in range(1, CONV_W):
            xc = xc + xp_ref[:, p, first + k:first + k + tc, :] * cw[k:k + 1, cols]
        xp_ref[:, p, 0:SUBLANE, :] = xp_ref[:, p, tc:tc + SUBLANE, :]
        xc = xc.reshape(nb * tc, LANE)
        gm = _dot(xc.astype(BF16), wg_ref[p])
        r = jax.nn.sigmoid(gm[:, 0:LANE] + ba_ref[:, cols])
        i = jax.nn.sigmoid(gm[:, LANE:2 * LANE] + bx_ref[:, cols])
        log_a = -RG_C * r * jax.nn.softplus(-lam_ref[:, cols])
        a = jnp.exp(log_a)
        _put_rows(a_ref, p, a, nb, tc, pitch)
        _put_rows(h_ref, p, jnp.sqrt(1.0 - a * a) * (i * xc), nb, tc, pitch)

    def step(t, hs):
        rs = pl.ds(t, nb, stride=pitch)
        out = []
        for p in range(ntile):
            h = a_ref[p, rs, :] * hs[p] + h_ref[p, rs, :]
            h_ref[p, rs, :] = h
            out.append(h)
        return tuple(out)

    hc = hc_ref[...]
    hs = lax.fori_loop(0, tc, step, tuple(hc[:, p * LANE:(p + 1) * LANE] for p in range(ntile)), unroll=2)
    for p in range(ntile):
        cols = slice(p * LANE, (p + 1) * LANE)
        hc_ref[:, cols] = hs[p]
        hl_ref[:, cols] = hs[p]
        h = _get_rows(h_ref, p, nb, tc, pitch).reshape(nb, tc, LANE)
        gate = p_ref[:, :, RG_WIDTH + p * LANE:RG_WIDTH + (p + 1) * LANE]
        y_ref[:, :, cols] = (h * jax.nn.gelu(gate)).astype(y_ref.dtype)


def _rglru(proj3, conv0, h0, w, layer, *, tc, ydtype):
    nb, nt, _ = proj3.shape
    pitch = _step_pitch(tc)
    ntile = RG_WIDTH // LANE
    row = _lspec(layer, (1, RG_WIDTH), _zeros_map(2))
    return pl.pallas_call(
        functools.partial(_rglru_kernel, nb=nb, tc=tc, pitch=pitch),
        grid=(nt // tc,),
        in_specs=[
            pl.BlockSpec((nb, tc, 2 * RG_WIDTH), lambda c: (0, c, COL_RG // (2 * RG_WIDTH))),
            _lspec(layer, (nb, SUBLANE, RG_WIDTH), _zeros_map(3)),
            _lspec(layer, (nb, RG_WIDTH), _zeros_map(2)),
            _lspec(layer, (CONV_W, RG_WIDTH), _zeros_map(2)),
            row,
            _lspec(layer, (ntile, LANE, 2 * LANE), _zeros_map(3)),
            row,
            row,
            row,
        ],
        out_specs=[
            pl.BlockSpec((nb, tc, RG_WIDTH), lambda c: (0, c, 0)),
            pl.BlockSpec((nb, RG_WIDTH), _zeros_map(2)),
        ],
        out_shape=[jax.ShapeDtypeStruct((nb, nt, RG_WIDTH), ydtype), jax.ShapeDtypeStruct((nb, RG_WIDTH), F32)],
        scratch_shapes=[
            pltpu.VMEM((nb, ntile, tc + SUBLANE, LANE), F32),
            pltpu.VMEM((ntile, nb * pitch, LANE), F32),
            pltpu.VMEM((ntile, nb * pitch, LANE), F32),
            pltpu.VMEM((nb, RG_WIDTH), F32),
        ],
        compiler_params=_cparams(("arbitrary",)),
        name="rglru",
    )(proj3, conv0, h0, w["rg_cw"], w["rg_cb"], w["rg_wg"], w["rg_ba"], w["rg_bx"], w["rg_lam"])


def _s5_kernel(u_ref, s0r_ref, s0i_ref, lr_ref, li_ref, bre_ref, bim_ref, cre_ref, cim_ref, d_ref, gw_ref, gb_ref,
               y_ref, sr_out, si_out, sre_ref, sim_ref, src_ref, sic_ref, *, nb, tc, pitch, tiles_per_pass):
    nq = S5_WIDTH // LANE
    qt = S5_FLAT // nq // LANE
    ntile = S5_FLAT // LANE

    @pl.when(pl.program_id(0) == 0)
    def _():
        src_ref[...] = s0r_ref[...]
        sic_ref[...] = s0i_ref[...]

    u = u_ref[...].reshape(nb * tc, S5_WIDTH)
    ub = u.astype(BF16)
    for q in range(nq):
        uq = ub[:, q * LANE:(q + 1) * LANE]
        bur = _dot(uq, bre_ref[q])
        bui = _dot(uq, bim_ref[q])
        for j in range(qt):
            _put_rows(sre_ref, q * qt + j, bur[:, j * LANE:(j + 1) * LANE], nb, tc, pitch)
            _put_rows(sim_ref, q * qt + j, bui[:, j * LANE:(j + 1) * LANE], nb, tc, pitch)

    for t0 in range(0, ntile, tiles_per_pass):
        tiles = list(range(t0, t0 + tiles_per_pass))
        lrs = [jnp.broadcast_to(lr_ref[:, tl * LANE:(tl + 1) * LANE], (nb, LANE)) for tl in tiles]
        lis = [jnp.broadcast_to(li_ref[:, tl * LANE:(tl + 1) * LANE], (nb, LANE)) for tl in tiles]

        def step(t, carry, tiles=tiles, lrs=lrs, lis=lis):
            rs = pl.ds(t, nb, stride=pitch)
            out = []
            for n, tl in enumerate(tiles):
                sr, si = carry[2 * n], carry[2 * n + 1]
                nr = lrs[n] * sr - lis[n] * si + sre_ref[tl, rs, :]
                ni = lrs[n] * si + lis[n] * sr + sim_ref[tl, rs, :]
                sre_ref[tl, rs, :] = nr
                sim_ref[tl, rs, :] = ni
                out += [nr, ni]
            return tuple(out)

        init = []
        for tl in tiles:
            init += [src_ref[:, tl * LANE:(tl + 1) * LANE], sic_ref[:, tl * LANE:(tl + 1) * LANE]]
        fin = lax.fori_loop(0, tc, step, tuple(init), unroll=2)
        for n, tl in enumerate(tiles):
            src_ref[:, tl * LANE:(tl + 1) * LANE] = fin[2 * n]
            sic_ref[:, tl * LANE:(tl + 1) * LANE] = fin[2 * n + 1]

    sr_out[...] = src_ref[...]
    si_out[...] = sic_ref[...]

    ys = []
    for q in range(nq):
        sr_q = jnp.concatenate([_get_rows(sre_ref, q * qt + j, nb, tc, pitch) for j in range(qt)], axis=1)
        si_q = jnp.concatenate([_get_rows(sim_ref, q * qt + j, nb, tc, pitch) for j in range(qt)], axis=1)
        ys.append(_dot(sr_q.astype(BF16), cre_ref[q]) - _dot(si_q.astype(BF16), cim_ref[q]))
    y = jnp.concatenate(ys, axis=1) + d_ref[...] * u
    y = jax.nn.gelu(y)
    y = y * jax.nn.sigmoid(_dot(y.astype(BF16), gw_ref[...]) + gb_ref[...])
    y_ref[...] = y.reshape(nb, tc, S5_WIDTH).astype(y_ref.dtype)


def _s5(proj3, s0r, s0i, w, layer, *, tc, tiles_per_pass, ydtype):
    nb, nt, _ = proj3.shape
    pitch = _step_pitch(tc)
    nq = S5_WIDTH // LANE
    qs = S5_FLAT // nq
    ntile = S5_FLAT // LANE
    state = _lspec(layer, (nb, S5_FLAT), _zeros_map(2))
    lam = _lspec(layer, (1, S5_FLAT), _zeros_map(2))
    b_in = _lspec(layer, (nq, LANE, qs), _zeros_map(3))
    c_out = _lspec(layer, (nq, qs, LANE), _zeros_map(3))
    row = _lspec(layer, (1, S5_WIDTH), _zeros_map(2))
    return pl.pallas_call(
        functools.partial(_s5_kernel, nb=nb, tc=tc, pitch=pitch, tiles_per_pass=tiles_per_pass),
        grid=(nt // tc,),
        in_specs=[
            pl.BlockSpec((nb, tc, S5_WIDTH), lambda c: (0, c, COL_S5 // S5_WIDTH)),
            state, state, lam, lam, b_in, b_in, c_out, c_out, row,
            _lspec(layer, (S5_WIDTH, S5_WIDTH), _zeros_map(2)),
            row,
        ],
        out_specs=[
            pl.BlockSpec((nb, tc, S5_WIDTH), lambda c: (0, c, 0)),
            pl.BlockSpec((nb, S5_FLAT), _zeros_map(2)),
            pl.BlockSpec((nb, S5_FLAT), _zeros_map(2)),
        ],
        out_shape=[
            jax.ShapeDtypeStruct((nb, nt, S5_WIDTH), ydtype),
            jax.ShapeDtypeStruct((nb, S5_FLAT), F32),
            jax.ShapeDtypeStruct((nb, S5_FLAT), F32),
        ],
        scratch_shapes=[
            pltpu.VMEM((ntile, nb * pitch, LANE), F32),
            pltpu.VMEM((ntile, nb * pitch, LANE), F32),
            pltpu.VMEM((nb, S5_FLAT), F32),
            pltpu.VMEM((nb, S5_FLAT), F32),
        ],
        compiler_params=_cparams(("arbitrary",)),
        name="s5",
    )(proj3, s0r, s0i, w["s5_lr"], w["s5_li"], w["s5_bre"], w["s5_bim"], w["s5_cre"], w["s5_cim"], w["s5_d"],
      w["s5_gw"], w["s5_gb"])


def _gdn_masks(sr, ck):
    shift = ck.bit_length() - 1
    row = lax.broadcasted_iota(jnp.int32, (sr, sr), 0)
    col = lax.broadcasted_iota(jnp.int32, (sr, sr), 1)
    same = (row >> shift) == (col >> shift)
    causal = same & (row >= col)
    strict = same & (row > col)
    tril = jnp.where(causal, 1.0, 0.0).astype(BF16)
    eye = jnp.where(row == col, 1.0, 0.0).astype(F32)
    return causal, strict, tril, eye


def _gdn_stack(blocks, ck, masks, xc, g_all, beta_all, z, nw, s_ref, o_ref):
    nblk = len(blocks)
    sr = nblk * ck
    causal, strict, tril, eye = masks

    def stack(fn):
        return jnp.concatenate([fn(g, h) for g, h in blocks], axis=0)

    q = stack(lambda g, h: xc[h][g])
    k = stack(lambda g, h: xc[GDN_HEADS + h][g])
    v = stack(lambda g, h: xc[2 * GDN_HEADS + h][g])
    zs = stack(lambda g, h: z[g, :, h * GDN_DV:(h + 1) * GDN_DV])
    gcol = stack(lambda g, h: jnp.broadcast_to(g_all[g, :, h:h + 1], (ck, LANE)))
    beta = stack(lambda g, h: jnp.broadcast_to(beta_all[g, :, GDN_HEADS + h:GDN_HEADS + h + 1], (ck, LANE)))

    q = q * lax.rsqrt(jnp.sum(q * q, axis=-1, keepdims=True) + EPS) * (GDN_DK ** -0.5)
    k = k * lax.rsqrt(jnp.sum(k * k, axis=-1, keepdims=True) + EPS)
    kb = k * beta
    vb = v * beta
    yield

    gc = _dot_exact_lhs(tril, gcol)
    kbf = k.astype(BF16)
    kk = _dot_nt(kb.astype(BF16), kbf)
    qk = _dot_nt(q.astype(BF16), kbf)
    yield
    gc_lanes = jnp.transpose(gc)[0:SUBLANE, :]
    gc_rows = jnp.concatenate([gc] * (sr // LANE), axis=1) if sr > LANE else gc[:, 0:sr]
    seg = gc_rows - jnp.concatenate([gc_lanes] * (sr // SUBLANE), axis=0)
    decay = jnp.exp(jnp.where(causal, seg, MASKED_LOG))
    lmat = jnp.where(strict, kk * decay, 0.0)
    qk = (qk * decay).astype(BF16)

    x = -lmat
    p = _dot_split(lmat, lmat)
    yield
    x = x + p + _dot(x.astype(BF16), p.astype(BF16))
    n = 4
    while n < ck:
        pb = p.astype(BF16)
        p = _dot(pb, pb)
        yield
        x = x + p + _dot(x.astype(BF16), p.astype(BF16))
        yield
        n *= 2
    tmat = (eye + x).astype(BF16)
    egc = jnp.exp(gc)
    uw = _dot(tmat, jnp.concatenate([vb, kb * egc], axis=1).astype(BF16))
    yield
    u = uw[:, 0:GDN_DV]
    w = uw[:, GDN_DV:GDN_DV + GDN_DK]
    qg = q * egc
    g_last = jnp.concatenate(
        [jnp.broadcast_to(gc[(n + 1) * ck - 1:(n + 1) * ck, :], (ck, LANE)) for n in range(nblk)], axis=0)
    k_dec = (k * jnp.exp(g_last - gc)).astype(BF16)

    v_new, qs = [], []
    for n, (g, h) in enumerate(blocks):
        rs = slice(n * ck, (n + 1) * ck)
        sb = s_ref[g, h].astype(BF16)
        r = _dot(jnp.concatenate([w[rs], qg[rs]], axis=0).astype(BF16), sb)
        v_new.append(u[rs] - r[0:ck])
        qs.append(r[ck:2 * ck])
    yield
    v_new = jnp.concatenate(v_new, axis=0)
    vnb = v_new.astype(BF16)
    o = jnp.concatenate(qs, axis=0) + _dot(qk, vnb)
    for n, (g, h) in enumerate(blocks):
        rs = slice(n * ck, (n + 1) * ck)
        s_ref[g, h] = s_ref[g, h] * jnp.exp(gc[(n + 1) * ck - 1:(n + 1) * ck, :]) + _dot_tn(k_dec[rs], vnb[rs])
    yield

    o = _rms(o, nw) * (zs * jax.nn.sigmoid(zs))
    for n, (g, h) in enumerate(blocks):
        o_ref[g, :, h * GDN_DV:(h + 1) * GDN_DV] = o[n * ck:(n + 1) * ck].astype(o_ref.dtype)


def _gdn_kernel(q_ref, k_ref, v_ref, z_ref, ab_ref, conv0_ref, s0_ref, cw_ref, alog_ref, dtb_ref, nw_ref,
                *rest, ck, nseq, stack_rows):
    o_ref, sout_ref, xb_ref, s_ref = rest[-4:]
    c = pl.program_id(1)
    ntile = QKV_WIDTH // LANE
    per_src = GDN_QK // LANE

    @pl.when(c == 0)
    def _():
        for lt in range(ntile):
            xb_ref[:, lt, 0:SUBLANE, :] = conv0_ref[:, :, lt * LANE:(lt + 1) * LANE]
        s_ref[...] = s0_ref[...]

    cw = cw_ref[...]
    first = SUBLANE - (CONV_W - 1)
    xc = []
    for lt in range(ntile):
        src = (q_ref, k_ref, v_ref)[lt // per_src]
        xb_ref[:, lt, SUBLANE:SUBLANE + ck, :] = src[:, :, (lt % per_src) * LANE:(lt % per_src + 1) * LANE]
        acc = xb_ref[:, lt, first:first + ck, :] * cw[0:1, lt * LANE:(lt + 1) * LANE]
        for j in range(1, CONV_W):
            acc = acc + xb_ref[:, lt, first + j:first + j + ck, :] * cw[j:j + 1, lt * LANE:(lt + 1) * LANE]
        xb_ref[:, lt, 0:SUBLANE, :] = xb_ref[:, lt, ck:ck + SUBLANE, :]
        xc.append(acc * jax.nn.sigmoid(acc))

    ab = ab_ref[...]
    g_all = -jnp.exp(alog_ref[...]) * jax.nn.softplus(ab + dtb_ref[...])
    beta_all = jax.nn.sigmoid(ab)
    z = z_ref[...]
    nw = nw_ref[...]
    masks = _gdn_masks(stack_rows, ck)
    blocks = [(g, h) for g in range(nseq) for h in range(GDN_HEADS)]
    per = stack_rows // ck
    stacks = [_gdn_stack(blocks[b0:b0 + per], ck, masks, xc, g_all, beta_all, z, nw, s_ref, o_ref)
              for b0 in range(0, len(blocks), per)]
    while stacks:
        stacks = [s for s in stacks if next(s, True) is None]

    @pl.when(c == pl.num_programs(1) - 1)
    def _():
        sout_ref[...] = s_ref[...]


def _gdn(proj3, ab3, conv0, s_all, s_done, w, layer, *, ck, nseq, stack_rows, odtype):
    nb, nt, _ = proj3.shape
    assert (nseq * GDN_HEADS * ck) % stack_rows == 0 and stack_rows % ck == 0
    qblk = COL_QKV // GDN_QK
    state_spec = _lspec(layer, (nseq, GDN_HEADS, GDN_DK, GDN_DV), lambda i, c: (i, 0, 0, 0))
    head_row = _lspec(layer, (1, LANE), _zeros_map(2))
    operands = [proj3, proj3, proj3, proj3, ab3, conv0, s_all, w["gd_cw"], w["gd_alog"], w["gd_dtb"], w["gd_nw"]]
    extra_specs, aliases = [], {}
    if s_done is not None:
        aliases = {len(operands): 1}
        operands.append(s_done)
        extra_specs = [pl.BlockSpec(memory_space=pl.ANY)]
    return pl.pallas_call(
        functools.partial(_gdn_kernel, ck=ck, nseq=nseq, stack_rows=stack_rows),
        grid=(nb // nseq, nt // ck),
        input_output_aliases=aliases,
        in_specs=[
            pl.BlockSpec((nseq, ck, GDN_QK), lambda i, c: (i, c, qblk)),
            pl.BlockSpec((nseq, ck, GDN_QK), lambda i, c: (i, c, qblk + 1)),
            pl.BlockSpec((nseq, ck, GDN_V), lambda i, c: (i, c, qblk + 2)),
            pl.BlockSpec((nseq, ck, GDN_V), lambda i, c: (i, c, COL_Z // GDN_V)),
            pl.BlockSpec((nseq, ck, N_AB), lambda i, c: (i, c, 0)),
            _lspec(layer, (nseq, SUBLANE, QKV_WIDTH), lambda i, c: (i, 0, 0)),
            state_spec,
            _lspec(layer, (CONV_W, QKV_WIDTH), _zeros_map(2)),
            head_row,
            head_row,
            head_row,
        ] + extra_specs,
        out_specs=[
            pl.BlockSpec((nseq, ck, GDN_V), lambda i, c: (i, c, 0)),
            state_spec,
        ],
        out_shape=[
            jax.ShapeDtypeStruct((nb, nt, GDN_V), odtype),
            jax.ShapeDtypeStruct(s_all.shape, F32),
        ],
        scratch_shapes=[
            pltpu.VMEM((nseq, QKV_WIDTH // LANE, ck + SUBLANE, LANE), F32),
            pltpu.VMEM((nseq, GDN_HEADS, GDN_DK, GDN_DV), F32),
        ],
        compiler_params=_cparams(("parallel", "arbitrary")),
        name="gdn",
    )(*operands)


def _merge_kernel(x_ref, yrg_ref, ys5_ref, ygd_ref, grg_ref, gs5_ref, ggd_ref, wrg_ref, ws5_ref, wgd_ref,
                  wo_ref, g_ref, o_ref):
    mixed = jax.nn.sigmoid(grg_ref[...]) * _dot(yrg_ref[...].astype(BF16), wrg_ref[...])
    mixed = mixed + jax.nn.sigmoid(gs5_ref[...]) * _dot(ys5_ref[...].astype(BF16), ws5_ref[...])
    mixed = mixed + jax.nn.sigmoid(ggd_ref[...]) * _dot(ygd_ref[...].astype(BF16), wgd_ref[...])
    out = _dot(mixed.astype(BF16), wo_ref[...])
    o_ref[...] = x_ref[...] + _rms(out, g_ref[...])


def _merge(x, yrg, ys5, ygd, proj, w, layer):
    m = x.shape[0]
    tm = min(MERGE_TM, m)
    half = D_MODEL // 2
    branch_out = _lspec(layer, (half, D_MODEL), _zeros_map(2))
    return pl.pallas_call(
        _merge_kernel,
        grid=(m // tm,),
        in_specs=[
            pl.BlockSpec((tm, D_MODEL), lambda i: (i, 0)),
            pl.BlockSpec((tm, half), lambda i: (i, 0)),
            pl.BlockSpec((tm, half), lambda i: (i, 0)),
            pl.BlockSpec((tm, half), lambda i: (i, 0)),
            pl.BlockSpec((tm, D_MODEL), lambda i: (i, 0)),
            pl.BlockSpec((tm, D_MODEL), lambda i: (i, 1)),
            pl.BlockSpec((tm, D_MODEL), lambda i: (i, 2)),
            branch_out, branch_out, branch_out,
            _lspec(layer, (D_MODEL, D_MODEL), _zeros_map(2)),
            _lspec(layer, (1, D_MODEL), _zeros_map(2)),
        ],
        out_specs=pl.BlockSpec((tm, D_MODEL), lambda i: (i, 0)),
        out_shape=jax.ShapeDtypeStruct((m, D_MODEL), F32),
        compiler_params=_cparams(("parallel",)),
        name="merge",
    )(x, yrg, ys5, ygd, proj, proj, proj, w["rg_wo"], w["s5_wo"], w["gd_wo"], w["w_out"], w["g_post_mix"])


def _ffn_kernel(x_ref, gpre_ref, wg_ref, wu_ref, wd_ref, gpost_ref, o_ref, h_ref, acc_ref):
    j = pl.program_id(1)

    @pl.when(j == 0)
    def _():
        h_ref[...] = _rms(x_ref[...], gpre_ref[...]).astype(BF16)
        acc_ref[...] = jnp.zeros_like(acc_ref)

    h = h_ref[...]
    gate = _dot(h, wg_ref[...])
    act = (gate * jax.nn.sigmoid(gate)) * _dot(h, wu_ref[...])
    acc_ref[...] += _dot(act.astype(BF16), wd_ref[...])

    @pl.when(j == pl.num_programs(1) - 1)
    def _():
        o_ref[...] = x_ref[...] + _rms(acc_ref[...], gpost_ref[...])


def _ffn(x, w, layer):
    m = x.shape[0]
    tm = min(FFN_TM, m)
    tf = FFN_TF
    row = _lspec(layer, (1, D_MODEL), _zeros_map(2))
    return pl.pallas_call(
        _ffn_kernel,
        grid=(m // tm, D_FF // tf),
        in_specs=[
            pl.BlockSpec((tm, D_MODEL), lambda i, j: (i, 0)),
            row,
            _lspec(layer, (D_MODEL, tf), lambda i, j: (0, j)),
            _lspec(layer, (D_MODEL, tf), lambda i, j: (0, j)),
            _lspec(layer, (tf, D_MODEL), lambda i, j: (j, 0)),
            row,
        ],
        out_specs=pl.BlockSpec((tm, D_MODEL), lambda i, j: (i, 0)),
        out_shape=jax.ShapeDtypeStruct((m, D_MODEL), F32),
        scratch_shapes=[pltpu.VMEM((tm, D_MODEL), BF16), pltpu.VMEM((tm, D_MODEL), F32)],
        compiler_params=_cparams(("parallel", "arbitrary")),
        name="ffn",
    )(x, w["g_pre_ffn"], w["ffn_wg"], w["ffn_wu"], w["ffn_wd"], w["g_post_ffn"])


def _block_diag(blocks):
    n, r, c = blocks.shape[-3:]
    eye = jnp.eye(n, dtype=blocks.dtype)
    out = jnp.einsum("...nrc,nm->...nrmc", blocks, eye)
    return out.reshape(blocks.shape[:-3] + (n * r, n * c))


def _prep_params(p):
    f32 = F32
    depth = p["w_in"].shape[0]
    w_in = p["w_in"]
    sizes = (RG_WIDTH, RG_WIDTH, S5_WIDTH, QKV_WIDTH, GDN_V, GDN_HEADS, GDN_HEADS, 3 * D_MODEL)
    offs = [0]
    for s in sizes:
        offs.append(offs[-1] + s)
    seg = lambda i: w_in[:, :, offs[i]:offs[i + 1]]
    w_main = jnp.concatenate([seg(7), seg(0), seg(1), seg(2), seg(3), seg(4)], axis=2).astype(BF16)
    w_ab = jnp.concatenate(
        [seg(5), seg(6), jnp.zeros((depth, D_MODEL, N_AB - 2 * GDN_HEADS), f32)], axis=2).astype(BF16)

    pair = LANE // RG_BLOCK
    pairs = lambda a: _block_diag(a.reshape(depth, RG_WIDTH // LANE, pair, RG_BLOCK, RG_BLOCK))
    wg = jnp.concatenate([pairs(p["rg_wa"]), pairs(p["rg_wx"])], axis=-1).astype(BF16)

    a_re = p["s5_a_re"].astype(f32)
    a_im = p["s5_a_im"].astype(f32)
    dt = jnp.exp(p["s5_log_dt"].astype(f32))[..., None]
    mag = jnp.exp(a_re * dt)
    lr = mag * jnp.cos(a_im * dt)
    li = mag * jnp.sin(a_im * dt)
    den = a_re * a_re + a_im * a_im
    cr = (((lr - 1.0) * a_re + li * a_im) / den)[..., None]
    ci = ((li * a_re - (lr - 1.0) * a_im) / den)[..., None]
    b_re = p["s5_b_re"].astype(f32)
    b_im = p["s5_b_im"].astype(f32)
    bb_re = cr * b_re - ci * b_im
    bb_im = cr * b_im + ci * b_re
    nq = S5_WIDTH // LANE
    gq = S5_GROUPS // nq
    to_in = lambda a: _block_diag(jnp.swapaxes(a, 2, 3).reshape(depth, nq, gq, S5_GROUP, S5_STATE))
    to_out = lambda a: _block_diag(jnp.swapaxes(a.astype(f32), 2, 3).reshape(depth, nq, gq, S5_STATE, S5_GROUP))

    row = lambda a: a.astype(f32).reshape(depth, 1, -1)
    pad4 = lambda a: jnp.pad(a.astype(f32), ((0, 0), (0, LANE - GDN_HEADS))).reshape(depth, 1, LANE)
    return dict(
        g_pre_mix=row(p["norm_pre_mix"]), g_post_mix=row(p["norm_post_mix"]),
        g_pre_ffn=row(p["norm_pre_ffn"]), g_post_ffn=row(p["norm_post_ffn"]),
        w_main=w_main, w_ab=w_ab,
        rg_cw=p["rg_conv_w"].astype(f32), rg_cb=row(p["rg_conv_b"]), rg_wg=wg,
        rg_ba=row(p["rg_ba"]), rg_bx=row(p["rg_bx"]), rg_lam=row(p["rg_lambda"]),
        rg_wo=p["rg_w_out"].astype(BF16),
        s5_lr=lr.reshape(depth, 1, S5_FLAT), s5_li=li.reshape(depth, 1, S5_FLAT),
        s5_bre=to_in(bb_re).astype(BF16), s5_bim=to_in(bb_im).astype(BF16),
        s5_cre=to_out(p["s5_c_re"]).astype(BF16), s5_cim=to_out(p["s5_c_im"]).astype(BF16),
        s5_d=row(p["s5_d"]), s5_gw=p["s5_glu_w"].astype(BF16), s5_gb=row(p["s5_glu_b"]),
        s5_wo=p["s5_w_out"].astype(BF16),
        gd_cw=p["gdn_conv_w"].astype(f32), gd_alog=pad4(p["gdn_a_log"]), gd_dtb=pad4(p["gdn_dt_bias"]),
        gd_nw=row(p["gdn_norm_w"]), gd_wo=p["gdn_w_out"].astype(BF16),
        w_out=p["w_out"].astype(BF16),
        ffn_wg=p["ffn_w_gate"].astype(BF16), ffn_wu=p["ffn_w_up"].astype(BF16),
        ffn_wd=p["ffn_w_down"].astype(BF16),
    )


def _pad_conv_state(s):
    return jnp.pad(s, ((0, 0), (0, 0), (SUBLANE - (CONV_W - 1), 0), (0, 0)))


def _run_group(x, states, w, *, tc, ck, gdn_nseq, gdn_stack_rows, tiles_per_pass):
    nb, nt, _ = x.shape
    keep = CONV_W - 1
    assert nt >= keep and nt % SUBLANE == 0 and nb % SUBLANE == 0
    st_rg_conv, st_rg_h, st_s5_re, st_s5_im, st_gd_conv, st_gd_s = states
    depth = st_rg_h.shape[0]
    rg_conv0 = _pad_conv_state(st_rg_conv)
    gd_conv0 = _pad_conv_state(st_gd_conv)
    s5_re0 = st_s5_re.reshape(depth, nb, S5_FLAT)
    s5_im0 = st_s5_im.reshape(depth, nb, S5_FLAT)
    ydtype = BF16 if (tc % BF16_ROWS == 0 and ck % BF16_ROWS == 0) else F32
    xf = x.reshape(nb * nt, D_MODEL)
    flat = lambda a: a.reshape(nb * nt, a.shape[-1])
    outs = [[] for _ in range(5)]
    s_gd = None
    for l in range(depth):
        proj, ab = _in_proj(xf, w, l)
        proj3 = proj.reshape(nb, nt, N_MAIN)
        y_rg, h_last = _rglru(proj3, rg_conv0, st_rg_h, w, l, tc=tc, ydtype=ydtype)
        y_s5, s_re, s_im = _s5(proj3, s5_re0, s5_im0, w, l, tc=tc, tiles_per_pass=tiles_per_pass, ydtype=ydtype)
        y_gd, s_gd = _gdn(proj3, ab.reshape(nb, nt, N_AB), gd_conv0, st_gd_s, s_gd, w, l, ck=ck, nseq=gdn_nseq,
                          stack_rows=gdn_stack_rows, odtype=ydtype)
        x1 = _merge(xf, flat(y_rg), flat(y_s5), flat(y_gd), proj, w, l)
        xf = _ffn(x1, w, l)

        outs[0].append(proj3[:, nt - keep:, COL_RG:COL_RG + RG_WIDTH])
        outs[1].append(h_last)
        outs[2].append(s_re.reshape(nb, S5_GROUPS, S5_STATE))
        outs[3].append(s_im.reshape(nb, S5_GROUPS, S5_STATE))
        outs[4].append(proj3[:, nt - keep:, COL_QKV:COL_QKV + QKV_WIDTH])
    return xf.reshape(nb, nt, D_MODEL), [jnp.stack(o) for o in outs] + [s_gd]


def _zero_states(depth, nb):
    return (jnp.zeros((depth, nb, CONV_W - 1, RG_WIDTH), F32),
            jnp.zeros((depth, nb, RG_WIDTH), F32),
            jnp.zeros((depth, nb, S5_GROUPS, S5_STATE), F32),
            jnp.zeros((depth, nb, S5_GROUPS, S5_STATE), F32),
            jnp.zeros((depth, nb, CONV_W - 1, QKV_WIDTH), F32),
            jnp.zeros((depth, nb, GDN_HEADS, GDN_DK, GDN_DV), F32))


def kernel(x_prompt, x_sample, state_rg_conv, state_rg_h, state_s5_re, state_s5_im, state_gdn_conv, state_gdn_s, norm_pre_mix, norm_post_mix, norm_pre_ffn, norm_post_ffn, w_in, rg_conv_w, rg_conv_b, rg_wa, rg_ba, rg_wx, rg_bx, rg_lambda, rg_w_out, s5_a_re, s5_a_im, s5_b_re, s5_b_im, s5_c_re, s5_c_im, s5_d, s5_log_dt, s5_glu_w, s5_glu_b, s5_w_out, gdn_conv_w, gdn_a_log, gdn_dt_bias, gdn_norm_w, gdn_w_out, w_out, ffn_w_gate, ffn_w_up, ffn_w_down):
    p = dict(norm_pre_mix=norm_pre_mix, norm_post_mix=norm_post_mix, norm_pre_ffn=norm_pre_ffn,
             norm_post_ffn=norm_post_ffn, w_in=w_in, rg_conv_w=rg_conv_w, rg_conv_b=rg_conv_b, rg_wa=rg_wa,
             rg_ba=rg_ba, rg_wx=rg_wx, rg_bx=rg_bx, rg_lambda=rg_lambda, rg_w_out=rg_w_out, s5_a_re=s5_a_re,
             s5_a_im=s5_a_im, s5_b_re=s5_b_re, s5_b_im=s5_b_im, s5_c_re=s5_c_re, s5_c_im=s5_c_im, s5_d=s5_d,
             s5_log_dt=s5_log_dt, s5_glu_w=s5_glu_w, s5_glu_b=s5_glu_b, s5_w_out=s5_w_out,
             gdn_conv_w=gdn_conv_w, gdn_a_log=gdn_a_log, gdn_dt_bias=gdn_dt_bias, gdn_norm_w=gdn_norm_w,
             gdn_w_out=gdn_w_out, w_out=w_out, ffn_w_gate=ffn_w_gate, ffn_w_up=ffn_w_up, ffn_w_down=ffn_w_down)
    depth = w_in.shape[0]
    w = _prep_params(p)
    y_p, st_p = _run_group(x_prompt, _zero_states(depth, x_prompt.shape[0]), w,
                           tc=128, ck=64, gdn_nseq=4, gdn_stack_rows=MXU_EDGE, tiles_per_pass=8)
    y_s, st_s = _run_group(
        x_sample, (state_rg_conv, state_rg_h, state_s5_re, state_s5_im, state_gdn_conv, state_gdn_s), w,
        tc=x_sample.shape[1], ck=x_sample.shape[1], gdn_nseq=16, gdn_stack_rows=MXU_EDGE, tiles_per_pass=1)
    return (y_p, y_s, *st_p, *st_s)
```

```python
import functools

import jax
import jax.numpy as jnp
from jax import lax
from jax.experimental import pallas as pl
from jax.experimental.pallas import tpu as pltpu

F32 = jnp.float32
BF16 = jnp.bfloat16

D_MODEL = 1024
CONV_W = 4
EPS = 1e-6
RG_WIDTH = 512
RG_BLOCK = 64
RG_C = 8.0
S5_WIDTH = 512
S5_GROUP = 16
S5_GROUPS = 32
S5_STATE = 64
S5_FLAT = S5_GROUPS * S5_STATE
GDN_HEADS = 4
GDN_DK = 128
GDN_DV = 128
GDN_QK = 512
GDN_V = 512
QKV_WIDTH = 1536
D_FF = 2816

LANE = 128
SUBLANE = 8
BF16_ROWS = 16
MXU_EDGE = 256
MASKED_LOG = -1e30

COL_RG = 0
COL_S5 = 1024
COL_QKV = 1536
COL_Z = 3072
N_MAIN = 3584
N_BRANCH = 3
N_AB = LANE

VMEM_LIMIT = 52 * 1024 * 1024

IN_PROJ_TM, IN_PROJ_TN = 2048, 512
MERGE_TM = 512
FFN_TM, FFN_TF = 1024, 256


def _cparams(sem):
    return pltpu.CompilerParams(dimension_semantics=sem, vmem_limit_bytes=VMEM_LIMIT)


def _lspec(layer, block, index_map):
    return pl.BlockSpec((None,) + tuple(block), lambda *g: (layer,) + tuple(index_map(*g)))


def _zeros_map(n):
    return lambda *g: (0,) * n


def _rms(x, g):
    return x * lax.rsqrt(jnp.mean(x * x, axis=-1, keepdims=True) + EPS) * g


def _dot(a, b):
    return jnp.dot(a, b, preferred_element_type=F32)


def _dot_nt(a, b):
    return lax.dot_general(a, b, (((1,), (1,)), ((), ())), preferred_element_type=F32)


def _dot_tn(a, b):
    return lax.dot_general(a, b, (((0,), (0,)), ((), ())), preferred_element_type=F32)


def _split2(a):
    hi = a.astype(BF16)
    return hi, (a - hi.astype(F32)).astype(BF16)


def _dot_exact_lhs(a, b):
    b1 = b.astype(BF16)
    r = b - b1.astype(F32)
    b2 = r.astype(BF16)
    b3 = (r - b2.astype(F32)).astype(BF16)
    return _dot(a, b1) + (_dot(a, b2) + _dot(a, b3))


def _dot_split(a, b):
    a1, a2 = _split2(a)
    b1, b2 = _split2(b)
    return _dot(a1, b1) + (_dot(a1, b2) + _dot(a2, b1))


def _step_pitch(nt_chunk):
    return nt_chunk if nt_chunk <= SUBLANE else nt_chunk + SUBLANE


def _put_rows(ref, tile, val, nb, tc, pitch):
    if pitch == tc:
        ref[tile, 0:nb * tc, :] = val
    else:
        for b in range(nb):
            ref[tile, b * pitch:b * pitch + tc, :] = val[b * tc:(b + 1) * tc, :]


def _get_rows(ref, tile, nb, tc, pitch):
    if pitch == tc:
        return ref[tile, 0:nb * tc, :]
    return jnp.concatenate([ref[tile, b * pitch:b * pitch + tc, :] for b in range(nb)], axis=0)


def _in_proj_kernel(x_ref, g_ref, w_ref, wab_ref, o_ref, oab_ref, h_ref):
    @pl.when(pl.program_id(1) == 0)
    def _():
        h = _rms(x_ref[...], g_ref[...]).astype(BF16)
        h_ref[...] = h
        oab_ref[...] = _dot(h, wab_ref[...])

    o_ref[...] = _dot(h_ref[...], w_ref[...])


def _in_proj(x, w, layer):
    m = x.shape[0]
    tm = min(IN_PROJ_TM, m)
    tn = IN_PROJ_TN
    return pl.pallas_call(
        _in_proj_kernel,
        grid=(m // tm, N_MAIN // tn),
        in_specs=[
            pl.BlockSpec((tm, D_MODEL), lambda i, j: (i, 0)),
            _lspec(layer, (1, D_MODEL), _zeros_map(2)),
            _lspec(layer, (D_MODEL, tn), lambda i, j: (0, j)),
            _lspec(layer, (D_MODEL, N_AB), _zeros_map(2)),
        ],
        out_specs=[
            pl.BlockSpec((tm, tn), lambda i, j: (i, j)),
            pl.BlockSpec((tm, N_AB), lambda i, j: (i, 0)),
        ],
        out_shape=[jax.ShapeDtypeStruct((m, N_MAIN), F32), jax.ShapeDtypeStruct((m, N_AB), F32)],
        scratch_shapes=[pltpu.VMEM((tm, D_MODEL), BF16)],
        compiler_params=_cparams(("parallel", "arbitrary")),
        name="in_proj",
    )(x, w["g_pre_mix"], w["w_main"], w["w_ab"])


def _rglru_kernel(p_ref, conv0_ref, h0_ref, cw_ref, cb_ref, wg_ref, ba_ref, bx_ref, lam_ref,
                  y_ref, hl_ref, xp_ref, a_ref, h_ref, hc_ref, *, nb, tc, pitch):
    ntile = RG_WIDTH // LANE
    first = SUBLANE - (CONV_W - 1)

    @pl.when(pl.program_id(0) == 0)
    def _():
        for p in range(ntile):
            xp_ref[:, p, 0:SUBLANE, :] = conv0_ref[:, :, p * LANE:(p + 1) * LANE]
        hc_ref[...] = h0_ref[...]

    cw = cw_ref[...]
    for p in range(ntile):
        cols = slice(p * LANE, (p + 1) * LANE)
        xp_ref[:, p, SUBLANE:SUBLANE + tc, :] = p_ref[:, :, cols]
        xc = cb_ref[:, cols] + xp_ref[:, p, first:first + tc, :] * cw[0:1, cols]
        for k in range(1, CONV_W):
            xc = xc + xp_ref[:, p, first + k:first + k + tc, :] * cw[k:k + 1, cols]
        xp_ref[:, p, 0:SUBLANE, :] = xp_ref[:, p, tc:tc + SUBLANE, :]
        xc = xc.reshape(nb * tc, LANE)
        gm = _dot(xc.astype(BF16), wg_ref[p])
        r = jax.nn.sigmoid(gm[:, 0:LANE] + ba_ref[:, cols])
        i = jax.nn.sigmoid(gm[:, LANE:2 * LANE] + bx_ref[:, cols])
        log_a = -RG_C * r * jax.nn.softplus(-lam_ref[:, cols])
        a = jnp.exp(log_a)
        _put_rows(a_ref, p, a, nb, tc, pitch)
        _put_rows(h_ref, p, jnp.sqrt(1.0 - a * a) * (i * xc), nb, tc, pitch)

    def step(t, hs):
        rs = pl.ds(t, nb, stride=pitch)
        out = []
        for p in range(ntile):
            h = a_ref[p, rs, :] * hs[p] + h_ref[p, rs, :]
            h_ref[p, rs, :] = h
            out.append(h)
        return tuple(out)

    hc = hc_ref[...]
    hs = lax.fori_loop(0, tc, step, tuple(hc[:, p * LANE:(p + 1) * LANE] for p in range(ntile)), unroll=2)
    for p in range(ntile):
        cols = slice(p * LANE, (p + 1) * LANE)
        hc_ref[:, cols] = hs[p]
        hl_ref[:, cols] = hs[p]
        h = _get_rows(h_ref, p, nb, tc, pitch).reshape(nb, tc, LANE)
        gate = p_ref[:, :, RG_WIDTH + p * LANE:RG_WIDTH + (p + 1) * LANE]
        y_ref[:, :, cols] = (h * jax.nn.gelu(gate)).astype(y_ref.dtype)


def _rglru(proj3, conv0, h0, w, layer, *, tc, ydtype):
    nb, nt, _ = proj3.shape
    pitch = _step_pitch(tc)
    ntile = RG_WIDTH // LANE
    row = _lspec(layer, (1, RG_WIDTH), _zeros_map(2))
    return pl.pallas_call(
        functools.partial(_rglru_kernel, nb=nb, tc=tc, pitch=pitch),
        grid=(nt // tc,),
        in_specs=[
            pl.BlockSpec((nb, tc, 2 * RG_WIDTH), lambda c: (0, c, COL_RG // (2 * RG_WIDTH))),
            _lspec(layer, (nb, SUBLANE, RG_WIDTH), _zeros_map(3)),
            _lspec(layer, (nb, RG_WIDTH), _zeros_map(2)),
            _lspec(layer, (CONV_W, RG_WIDTH), _zeros_map(2)),
            row,
            _lspec(layer, (ntile, LANE, 2 * LANE), _zeros_map(3)),
            row,
            row,
            row,
        ],
        out_specs=[
            pl.BlockSpec((nb, tc, RG_WIDTH), lambda c: (0, c, 0)),
            pl.BlockSpec((nb, RG_WIDTH), _zeros_map(2)),
        ],
        out_shape=[jax.ShapeDtypeStruct((nb, nt, RG_WIDTH), ydtype), jax.ShapeDtypeStruct((nb, RG_WIDTH), F32)],
        scratch_shapes=[
            pltpu.VMEM((nb, ntile, tc + SUBLANE, LANE), F32),
            pltpu.VMEM((ntile, nb * pitch, LANE), F32),
            pltpu.VMEM((ntile, nb * pitch, LANE), F32),
            pltpu.VMEM((nb, RG_WIDTH), F32),
        ],
        compiler_params=_cparams(("arbitrary",)),
        name="rglru",
    )(proj3, conv0, h0, w["rg_cw"], w["rg_cb"], w["rg_wg"], w["rg_ba"], w["rg_bx"], w["rg_lam"])


def _s5_kernel(u_ref, s0r_ref, s0i_ref, lr_ref, li_ref, bre_ref, bim_ref, cre_ref, cim_ref, d_ref, gw_ref, gb_ref,
               y_ref, sr_out, si_out, sre_ref, sim_ref, src_ref, sic_ref, *, nb, tc, pitch, tiles_per_pass):
    nq = S5_WIDTH // LANE
    qt = S5_FLAT // nq // LANE
    ntile = S5_FLAT // LANE

    @pl.when(pl.program_id(0) == 0)
    def _():
        src_ref[...] = s0r_ref[...]
        sic_ref[...] = s0i_ref[...]

    u = u_ref[...].reshape(nb * tc, S5_WIDTH)
    ub = u.astype(BF16)
    for q in range(nq):
        uq = ub[:, q * LANE:(q + 1) * LANE]
        bur = _dot(uq, bre_ref[q])
        bui = _dot(uq, bim_ref[q])
        for j in range(qt):
            _put_rows(sre_ref, q * qt + j, bur[:, j * LANE:(j + 1) * LANE], nb, tc, pitch)
            _put_rows(sim_ref, q * qt + j, bui[:, j * LANE:(j + 1) * LANE], nb, tc, pitch)

    for t0 in range(0, ntile, tiles_per_pass):
        tiles = list(range(t0, t0 + tiles_per_pass))
        lrs = [jnp.broadcast_to(lr_ref[:, tl * LANE:(tl + 1) * LANE], (nb, LANE)) for tl in tiles]
        lis = [jnp.broadcast_to(li_ref[:, tl * LANE:(tl + 1) * LANE], (nb, LANE)) for tl in tiles]

        def step(t, carry, tiles=tiles, lrs=lrs, lis=lis):
            rs = pl.ds(t, nb, stride=pitch)
            out = []
            for n, tl in enumerate(tiles):
                sr, si = carry[2 * n], carry[2 * n + 1]
                nr = lrs[n] * sr - lis[n] * si + sre_ref[tl, rs, :]
                ni = lrs[n] * si + lis[n] * sr + sim_ref[tl, rs, :]
                sre_ref[tl, rs, :] = nr
                sim_ref[tl, rs, :] = ni
                out += [nr, ni]
            return tuple(out)

        init = []
        for tl in tiles:
            init += [src_ref[:, tl * LANE:(tl + 1) * LANE], sic_ref[:, tl * LANE:(tl + 1) * LANE]]
        fin = lax.fori_loop(0, tc, step, tuple(init), unroll=2)
        for n, tl in enumerate(tiles):
            src_ref[:, tl * LANE:(tl + 1) * LANE] = fin[2 * n]
            sic_ref[:, tl * LANE:(tl + 1) * LANE] = fin[2 * n + 1]

    sr_out[...] = src_ref[...]
    si_out[...] = sic_ref[...]

    ys = []
    for q in range(nq):
        sr_q = jnp.concatenate([_get_rows(sre_ref, q * qt + j, nb, tc, pitch) for j in range(qt)], axis=1)
        si_q = jnp.concatenate([_get_rows(sim_ref, q * qt + j, nb, tc, pitch) for j in range(qt)], axis=1)
        ys.append(_dot(sr_q.astype(BF16), cre_ref[q]) - _dot(si_q.astype(BF16), cim_ref[q]))
    y = jnp.concatenate(ys, axis=1) + d_ref[...] * u
    y = jax.nn.gelu(y)
    y = y * jax.nn.sigmoid(_dot(y.astype(BF16), gw_ref[...]) + gb_ref[...])
    y_ref[...] = y.reshape(nb, tc, S5_WIDTH).astype(y_ref.dtype)


def _s5(proj3, s0r, s0i, w, layer, *, tc, tiles_per_pass, ydtype):
    nb, nt, _ = proj3.shape
    pitch = _step_pitch(tc)
    nq = S5_WIDTH // LANE
    qs = S5_FLAT // nq
    ntile = S5_FLAT // LANE
    state = _lspec(layer, (nb, S5_FLAT), _zeros_map(2))
    lam = _lspec(layer, (1, S5_FLAT), _zeros_map(2))
    b_in = _lspec(layer, (nq, LANE, qs), _zeros_map(3))
    c_out = _lspec(layer, (nq, qs, LANE), _zeros_map(3))
    row = _lspec(layer, (1, S5_WIDTH), _zeros_map(2))
    return pl.pallas_call(
        functools.partial(_s5_kernel, nb=nb, tc=tc, pitch=pitch, tiles_per_pass=tiles_per_pass),
        grid=(nt // tc,),
        in_specs=[
            pl.BlockSpec((nb, tc, S5_WIDTH), lambda c: (0, c, COL_S5 // S5_WIDTH)),
            state, state, lam, lam, b_in, b_in, c_out, c_out, row,
            _lspec(layer, (S5_WIDTH, S5_WIDTH), _zeros_map(2)),
            row,
        ],
        out_specs=[
            pl.BlockSpec((nb, tc, S5_WIDTH), lambda c: (0, c, 0)),
            pl.BlockSpec((nb, S5_FLAT), _zeros_map(2)),
            pl.BlockSpec((nb, S5_FLAT), _zeros_map(2)),
        ],
        out_shape=[
            jax.ShapeDtypeStruct((nb, nt, S5_WIDTH), ydtype),
            jax.ShapeDtypeStruct((nb, S5_FLAT), F32),
            jax.ShapeDtypeStruct((nb, S5_FLAT), F32),
        ],
        scratch_shapes=[
            pltpu.VMEM((ntile, nb * pitch, LANE), F32),
            pltpu.VMEM((ntile, nb * pitch, LANE), F32),
            pltpu.VMEM((nb, S5_FLAT), F32),
            pltpu.VMEM((nb, S5_FLAT), F32),
        ],
        compiler_params=_cparams(("arbitrary",)),
        name="s5",
    )(proj3, s0r, s0i, w["s5_lr"], w["s5_li"], w["s5_bre"], w["s5_bim"], w["s5_cre"], w["s5_cim"], w["s5_d"],
      w["s5_gw"], w["s5_gb"])


def _gdn_masks(sr, ck):
    shift = ck.bit_length() - 1
    row = lax.broadcasted_iota(jnp.int32, (sr, sr), 0)
    col = lax.broadcasted_iota(jnp.int32, (sr, sr), 1)
    same = (row >> shift) == (col >> shift)
    causal = same & (row >= col)
    strict = same & (row > col)
    tril = jnp.where(causal, 1.0, 0.0).astype(BF16)
    eye = jnp.where(row == col, 1.0, 0.0).astype(F32)
    return causal, strict, tril, eye


def _gdn_stack(blocks, ck, masks, xc, g_all, beta_all, z, nw, s_ref, o_ref):
    nblk = len(blocks)
    sr = nblk * ck
    causal, strict, tril, eye = masks

    def stack(fn):
        return jnp.concatenate([fn(g, h) for g, h in blocks], axis=0)

    q = stack(lambda g, h: xc[h][g])
    k = stack(lambda g, h: xc[GDN_HEADS + h][g])
    v = stack(lambda g, h: xc[2 * GDN_HEADS + h][g])
    zs = stack(lambda g, h: z[g, :, h * GDN_DV:(h + 1) * GDN_DV])
    gcol = stack(lambda g, h: jnp.broadcast_to(g_all[g, :, h:h + 1], (ck, LANE)))
    beta = stack(lambda g, h: jnp.broadcast_to(beta_all[g, :, GDN_HEADS + h:GDN_HEADS + h + 1], (ck, LANE)))

    q = q * lax.rsqrt(jnp.sum(q * q, axis=-1, keepdims=True) + EPS) * (GDN_DK ** -0.5)
    k = k * lax.rsqrt(jnp.sum(k * k, axis=-1, keepdims=True) + EPS)
    kb = k * beta
    vb = v * beta
    yield

    gc = _dot_exact_lhs(tril, gcol)
    kbf = k.astype(BF16)
    kk = _dot_nt(kb.astype(BF16), kbf)
    qk = _dot_nt(q.astype(BF16), kbf)
    yield
    gc_lanes = jnp.transpose(gc)[0:SUBLANE, :]
    gc_rows = jnp.concatenate([gc] * (sr // LANE), axis=1) if sr > LANE else gc[:, 0:sr]
    seg = gc_rows - jnp.concatenate([gc_lanes] * (sr // SUBLANE), axis=0)
    decay = jnp.exp(jnp.where(causal, seg, MASKED_LOG))
    lmat = jnp.where(strict, kk * decay, 0.0)
    qk = (qk * decay).astype(BF16)

    x = -lmat
    p = _dot_split(lmat, lmat)
    yield
    x = x + p + _dot(x.astype(BF16), p.astype(BF16))
    n = 4
    while n < ck:
        pb = p.astype(BF16)
        p = _dot(pb, pb)
        yield
        x = x + p + _dot(x.astype(BF16), p.astype(BF16))
        yield
        n *= 2
    tmat = (eye + x).astype(BF16)
    egc = jnp.exp(gc)
    uw = _dot(tmat, jnp.concatenate([vb, kb * egc], axis=1).astype(BF16))
    yield
    u = uw[:, 0:GDN_DV]
    w = uw[:, GDN_DV:GDN_DV + GDN_DK]
    qg = q * egc
    g_last = jnp.concatenate(
        [jnp.broadcast_to(gc[(n + 1) * ck - 1:(n + 1) * ck, :], (ck, LANE)) for n in range(nblk)], axis=0)
    k_dec = (k * jnp.exp(g_last - gc)).astype(BF16)

    v_new, qs = [], []
    for n, (g, h) in enumerate(blocks):
        rs = slice(n * ck, (n + 1) * ck)
        sb = s_ref[g, h].astype(BF16)
        r = _dot(jnp.concatenate([w[rs], qg[rs]], axis=0).astype(BF16), sb)
        v_new.append(u[rs] - r[0:ck])
        qs.append(r[ck:2 * ck])
    yield
    v_new = jnp.concatenate(v_new, axis=0)
    vnb = v_new.astype(BF16)
    o = jnp.concatenate(qs, axis=0) + _dot(qk, vnb)
    for n, (g, h) in enumerate(blocks):
        rs = slice(n * ck, (n + 1) * ck)
        s_ref[g, h] = s_ref[g, h] * jnp.exp(gc[(n + 1) * ck - 1:(n + 1) * ck, :]) + _dot_tn(k_dec[rs], vnb[rs])
    yield

    o = _rms(o, nw) * (zs * jax.nn.sigmoid(zs))
    for n, (g, h) in enumerate(blocks):
        o_ref[g, :, h * GDN_DV:(h + 1) * GDN_DV] = o[n * ck:(n + 1) * ck].astype(o_ref.dtype)


def _gdn_kernel(q_ref, k_ref, v_ref, z_ref, ab_ref, conv0_ref, s0_ref, cw_ref, alog_ref, dtb_ref, nw_ref,
                *rest, ck, nseq, stack_rows):
    o_ref, sout_ref, xb_ref, s_ref = rest[-4:]
    c = pl.program_id(1)
    ntile = QKV_WIDTH // LANE
    per_src = GDN_QK // LANE

    @pl.when(c == 0)
    def _():
        for lt in range(ntile):
            xb_ref[:, lt, 0:SUBLANE, :] = conv0_ref[:, :, lt * LANE:(lt + 1) * LANE]
        s_ref[...] = s0_ref[...]

    cw = cw_ref[...]
    first = SUBLANE - (CONV_W - 1)
    xc = []
    for lt in range(ntile):
        src = (q_ref, k_ref, v_ref)[lt // per_src]
        xb_ref[:, lt, SUBLANE:SUBLANE + ck, :] = src[:, :, (lt % per_src) * LANE:(lt % per_src + 1) * LANE]
        acc = xb_ref[:, lt, first:first + ck, :] * cw[0:1, lt * LANE:(lt + 1) * LANE]
        for j in range(1, CONV_W):
            acc = acc + xb_ref[:, lt, first + j:first + j + ck, :] * cw[j:j + 1, lt * LANE:(lt + 1) * LANE]
        xb_ref[:, lt, 0:SUBLANE, :] = xb_ref[:, lt, ck:ck + SUBLANE, :]
        xc.append(acc * jax.nn.sigmoid(acc))

    ab = ab_ref[...]
    g_all = -jnp.exp(alog_ref[...]) * jax.nn.softplus(ab + dtb_ref[...])
    beta_all = jax.nn.sigmoid(ab)
    z = z_ref[...]
    nw = nw_ref[...]
    masks = _gdn_masks(stack_rows, ck)
    blocks = [(g, h) for g in range(nseq) for h in range(GDN_HEADS)]
    per = stack_rows // ck
    stacks = [_gdn_stack(blocks[b0:b0 + per], ck, masks, xc, g_all, beta_all, z, nw, s_ref, o_ref)
              for b0 in range(0, len(blocks), per)]
    while stacks:
        stacks = [s for s in stacks if next(s, True) is None]

    @pl.when(c == pl.num_programs(1) - 1)
    def _():
        sout_ref[...] = s_ref[...]


def _gdn(proj3, ab3, conv0, s_all, s_done, w, layer, *, ck, nseq, stack_rows, odtype):
    nb, nt, _ = proj3.shape
    assert (nseq * GDN_HEADS * ck) % stack_rows == 0 and stack_rows % ck == 0
    qblk = COL_QKV // GDN_QK
    state_spec = _lspec(layer, (nseq, GDN_HEADS, GDN_DK, GDN_DV), lambda i, c: (i, 0, 0, 0))
    head_row = _lspec(layer, (1, LANE), _zeros_map(2))
    operands = [proj3, proj3, proj3, proj3, ab3, conv0, s_all, w["gd_cw"], w["gd_alog"], w["gd_dtb"], w["gd_nw"]]
    extra_specs, aliases = [], {}
    if s_done is not None:
        aliases = {len(operands): 1}
        operands.append(s_done)
        extra_specs = [pl.BlockSpec(memory_space=pl.ANY)]
    return pl.pallas_call(
        functools.partial(_gdn_kernel, ck=ck, nseq=nseq, stack_rows=stack_rows),
        grid=(nb // nseq, nt // ck),
        input_output_aliases=aliases,
        in_specs=[
            pl.BlockSpec((nseq, ck, GDN_QK), lambda i, c: (i, c, qblk)),
            pl.BlockSpec((nseq, ck, GDN_QK), lambda i, c: (i, c, qblk + 1)),
            pl.BlockSpec((nseq, ck, GDN_V), lambda i, c: (i, c, qblk + 2)),
            pl.BlockSpec((nseq, ck, GDN_V), lambda i, c: (i, c, COL_Z // GDN_V)),
            pl.BlockSpec((nseq, ck, N_AB), lambda i, c: (i, c, 0)),
            _lspec(layer, (nseq, SUBLANE, QKV_WIDTH), lambda i, c: (i, 0, 0)),
            state_spec,
            _lspec(layer, (CONV_W, QKV_WIDTH), _zeros_map(2)),
            head_row,
            head_row,
            head_row,
        ] + extra_specs,
        out_specs=[
            pl.BlockSpec((nseq, ck, GDN_V), lambda i, c: (i, c, 0)),
            state_spec,
        ],
        out_shape=[
            jax.ShapeDtypeStruct((nb, nt, GDN_V), odtype),
            jax.ShapeDtypeStruct(s_all.shape, F32),
        ],
        scratch_shapes=[
            pltpu.VMEM((nseq, QKV_WIDTH // LANE, ck + SUBLANE, LANE), F32),
            pltpu.VMEM((nseq, GDN_HEADS, GDN_DK, GDN_DV), F32),
        ],
        compiler_params=_cparams(("parallel", "arbitrary")),
        name="gdn",
    )(*operands)


def _merge_kernel(x_ref, yrg_ref, ys5_ref, ygd_ref, gpre_ref, wgate_ref, wrg_ref, ws5_ref, wgd_ref,
                  wo_ref, gpost_ref, o_ref):
    x = x_ref[...]
    h = _rms(x, gpre_ref[...]).astype(BF16)
    mixed = None
    for n, (y_ref, w_ref) in enumerate(((yrg_ref, wrg_ref), (ys5_ref, ws5_ref), (ygd_ref, wgd_ref))):
        gate = jax.nn.sigmoid(_dot(h, wgate_ref[:, n * D_MODEL:(n + 1) * D_MODEL]))
        term = gate * _dot(y_ref[...].astype(BF16), w_ref[...])
        mixed = term if mixed is None else mixed + term
    out = _dot(mixed.astype(BF16), wo_ref[...])
    o_ref[...] = x + _rms(out, gpost_ref[...])


def _merge(x, yrg, ys5, ygd, w, layer):
    m = x.shape[0]
    tm = min(MERGE_TM, m)
    half = D_MODEL // 2
    branch_out = _lspec(layer, (half, D_MODEL), _zeros_map(2))
    row = _lspec(layer, (1, D_MODEL), _zeros_map(2))
    return pl.pallas_call(
        _merge_kernel,
        grid=(m // tm,),
        in_specs=[
            pl.BlockSpec((tm, D_MODEL), lambda i: (i, 0)),
            pl.BlockSpec((tm, half), lambda i: (i, 0)),
            pl.BlockSpec((tm, half), lambda i: (i, 0)),
            pl.BlockSpec((tm, half), lambda i: (i, 0)),
            row,
            _lspec(layer, (D_MODEL, N_BRANCH * D_MODEL), _zeros_map(2)),
            branch_out, branch_out, branch_out,
            _lspec(layer, (D_MODEL, D_MODEL), _zeros_map(2)),
            row,
        ],
        out_specs=pl.BlockSpec((tm, D_MODEL), lambda i: (i, 0)),
        out_shape=jax.ShapeDtypeStruct((m, D_MODEL), F32),
        compiler_params=_cparams(("parallel",)),
        name="merge",
    )(x, yrg, ys5, ygd, w["g_pre_mix"], w["w_gates"], w["rg_wo"], w["s5_wo"], w["gd_wo"], w["w_out"],
      w["g_post_mix"])


def _ffn_kernel(x_ref, gpre_ref, wg_ref, wu_ref, wd_ref, gpost_ref, o_ref, h_ref, acc_ref):
    j = pl.program_id(1)

    @pl.when(j == 0)
    def _():
        h_ref[...] = _rms(x_ref[...], gpre_ref[...]).astype(BF16)
        acc_ref[...] = jnp.zeros_like(acc_ref)

    h = h_ref[...]
    gate = _dot(h, wg_ref[...])
    act = (gate * jax.nn.sigmoid(gate)) * _dot(h, wu_ref[...])
    acc_ref[...] += _dot(act.astype(BF16), wd_ref[...])

    @pl.when(j == pl.num_programs(1) - 1)
    def _():
        o_ref[...] = x_ref[...] + _rms(acc_ref[...], gpost_ref[...])


def _ffn(x, w, layer):
    m = x.shape[0]
    tm = min(FFN_TM, m)
    tf = FFN_TF
    row = _lspec(layer, (1, D_MODEL), _zeros_map(2))
    return pl.pallas_call(
        _ffn_kernel,
        grid=(m // tm, D_FF // tf),
        in_specs=[
            pl.BlockSpec((tm, D_MODEL), lambda i, j: (i, 0)),
            row,
            _lspec(layer, (D_MODEL, tf), lambda i, j: (0, j)),
            _lspec(layer, (D_MODEL, tf), lambda i, j: (0, j)),
            _lspec(layer, (tf, D_MODEL), lambda i, j: (j, 0)),
            row,
        ],
        out_specs=pl.BlockSpec((tm, D_MODEL), lambda i, j: (i, 0)),
        out_shape=jax.ShapeDtypeStruct((m, D_MODEL), F32),
        scratch_shapes=[pltpu.VMEM((tm, D_MODEL), BF16), pltpu.VMEM((tm, D_MODEL), F32)],
        compiler_params=_cparams(("parallel", "arbitrary")),
        name="ffn",
    )(x, w["g_pre_ffn"], w["ffn_wg"], w["ffn_wu"], w["ffn_wd"], w["g_post_ffn"])


def _block_diag(blocks):
    n, r, c = blocks.shape[-3:]
    eye = jnp.eye(n, dtype=blocks.dtype)
    out = jnp.einsum("...nrc,nm->...nrmc", blocks, eye)
    return out.reshape(blocks.shape[:-3] + (n * r, n * c))


def _prep_params(p):
    f32 = F32
    depth = p["w_in"].shape[0]
    w_in = p["w_in"]
    sizes = (RG_WIDTH, RG_WIDTH, S5_WIDTH, QKV_WIDTH, GDN_V, GDN_HEADS, GDN_HEADS, 3 * D_MODEL)
    offs = [0]
    for s in sizes:
        offs.append(offs[-1] + s)
    seg = lambda i: w_in[:, :, offs[i]:offs[i + 1]]
    w_main = w_in[:, :, 0:N_MAIN].astype(BF16)
    w_gates = seg(7).astype(BF16)
    w_ab = jnp.concatenate(
        [seg(5), seg(6), jnp.zeros((depth, D_MODEL, N_AB - 2 * GDN_HEADS), f32)], axis=2).astype(BF16)

    pair = LANE // RG_BLOCK
    pairs = lambda a: _block_diag(a.reshape(depth, RG_WIDTH // LANE, pair, RG_BLOCK, RG_BLOCK))
    wg = jnp.concatenate([pairs(p["rg_wa"]), pairs(p["rg_wx"])], axis=-1).astype(BF16)

    a_re = p["s5_a_re"].astype(f32)
    a_im = p["s5_a_im"].astype(f32)
    dt = jnp.exp(p["s5_log_dt"].astype(f32))[..., None]
    mag = jnp.exp(a_re * dt)
    lr = mag * jnp.cos(a_im * dt)
    li = mag * jnp.sin(a_im * dt)
    den = a_re * a_re + a_im * a_im
    cr = (((lr - 1.0) * a_re + li * a_im) / den)[..., None]
    ci = ((li * a_re - (lr - 1.0) * a_im) / den)[..., None]
    b_re = p["s5_b_re"].astype(f32)
    b_im = p["s5_b_im"].astype(f32)
    bb_re = cr * b_re - ci * b_im
    bb_im = cr * b_im + ci * b_re
    nq = S5_WIDTH // LANE
    gq = S5_GROUPS // nq
    to_in = lambda a: _block_diag(jnp.swapaxes(a, 2, 3).reshape(depth, nq, gq, S5_GROUP, S5_STATE))
    to_out = lambda a: _block_diag(jnp.swapaxes(a.astype(f32), 2, 3).reshape(depth, nq, gq, S5_STATE, S5_GROUP))

    row = lambda a: a.astype(f32).reshape(depth, 1, -1)
    pad4 = lambda a: jnp.pad(a.astype(f32), ((0, 0), (0, LANE - GDN_HEADS))).reshape(depth, 1, LANE)
    return dict(
        g_pre_mix=row(p["norm_pre_mix"]), g_post_mix=row(p["norm_post_mix"]),
        g_pre_ffn=row(p["norm_pre_ffn"]), g_post_ffn=row(p["norm_post_ffn"]),
        w_main=w_main, w_ab=w_ab, w_gates=w_gates,
        rg_cw=p["rg_conv_w"].astype(f32), rg_cb=row(p["rg_conv_b"]), rg_wg=wg,
        rg_ba=row(p["rg_ba"]), rg_bx=row(p["rg_bx"]), rg_lam=row(p["rg_lambda"]),
        rg_wo=p["rg_w_out"].astype(BF16),
        s5_lr=lr.reshape(depth, 1, S5_FLAT), s5_li=li.reshape(depth, 1, S5_FLAT),
        s5_bre=to_in(bb_re).astype(BF16), s5_bim=to_in(bb_im).astype(BF16),
        s5_cre=to_out(p["s5_c_re"]).astype(BF16), s5_cim=to_out(p["s5_c_im"]).astype(BF16),
        s5_d=row(p["s5_d"]), s5_gw=p["s5_glu_w"].astype(BF16), s5_gb=row(p["s5_glu_b"]),
        s5_wo=p["s5_w_out"].astype(BF16),
        gd_cw=p["gdn_conv_w"].astype(f32), gd_alog=pad4(p["gdn_a_log"]), gd_dtb=pad4(p["gdn_dt_bias"]),
        gd_nw=row(p["gdn_norm_w"]), gd_wo=p["gdn_w_out"].astype(BF16),
        w_out=p["w_out"].astype(BF16),
        ffn_wg=p["ffn_w_gate"].astype(BF16), ffn_wu=p["ffn_w_up"].astype(BF16),
        ffn_wd=p["ffn_w_down"].astype(BF16),
    )


def _pad_conv_state(s):
    return jnp.pad(s, ((0, 0), (0, 0), (SUBLANE - (CONV_W - 1), 0), (0, 0)))


def _run_group(x, states, w, *, tc, ck, gdn_nseq, gdn_stack_rows, tiles_per_pass):
    nb, nt, _ = x.shape
    keep = CONV_W - 1
    assert nt >= keep and nt % SUBLANE == 0 and nb % SUBLANE == 0
    st_rg_conv, st_rg_h, st_s5_re, st_s5_im, st_gd_conv, st_gd_s = states
    depth = st_rg_h.shape[0]
    rg_conv0 = _pad_conv_state(st_rg_conv)
    gd_conv0 = _pad_conv_state(st_gd_conv)
    s5_re0 = st_s5_re.reshape(depth, nb, S5_FLAT)
    s5_im0 = st_s5_im.reshape(depth, nb, S5_FLAT)
    ydtype = BF16 if (tc % BF16_ROWS == 0 and ck % BF16_ROWS == 0) else F32
    xf = x.reshape(nb * nt, D_MODEL)
    flat = lambda a: a.reshape(nb * nt, a.shape[-1])
    outs = [[] for _ in range(5)]
    s_gd = None
    for l in range(depth):
        proj, ab = _in_proj(xf, w, l)
        proj3 = proj.reshape(nb, nt, N_MAIN)
        y_rg, h_last = _rglru(proj3, rg_conv0, st_rg_h, w, l, tc=tc, ydtype=ydtype)
        y_s5, s_re, s_im = _s5(proj3, s5_re0, s5_im0, w, l, tc=tc, tiles_per_pass=tiles_per_pass, ydtype=ydtype)
        y_gd, s_gd = _gdn(proj3, ab.reshape(nb, nt, N_AB), gd_conv0, st_gd_s, s_gd, w, l, ck=ck, nseq=gdn_nseq,
                          stack_rows=gdn_stack_rows, odtype=ydtype)
        x1 = _merge(xf, flat(y_rg), flat(y_s5), flat(y_gd), w, l)
        xf = _ffn(x1, w, l)

        outs[0].append(proj3[:, nt - keep:, COL_RG:COL_RG + RG_WIDTH])
        outs[1].append(h_last)
        outs[2].append(s_re.reshape(nb, S5_GROUPS, S5_STATE))
        outs[3].append(s_im.reshape(nb, S5_GROUPS, S5_STATE))
        outs[4].append(proj3[:, nt - keep:, COL_QKV:COL_QKV + QKV_WIDTH])
    return xf.reshape(nb, nt, D_MODEL), [jnp.stack(o) for o in outs] + [s_gd]


def _zero_states(depth, nb):
    return (jnp.zeros((depth, nb, CONV_W - 1, RG_WIDTH), F32),
            jnp.zeros((depth, nb, RG_WIDTH), F32),
            jnp.zeros((depth, nb, S5_GROUPS, S5_STATE), F32),
            jnp.zeros((depth, nb, S5_GROUPS, S5_STATE), F32),
            jnp.zeros((depth, nb, CONV_W - 1, QKV_WIDTH), F32),
            jnp.zeros((depth, nb, GDN_HEADS, GDN_DK, GDN_DV), F32))


def kernel(x_prompt, x_sample, state_rg_conv, state_rg_h, state_s5_re, state_s5_im, state_gdn_conv, state_gdn_s, norm_pre_mix, norm_post_mix, norm_pre_ffn, norm_post_ffn, w_in, rg_conv_w, rg_conv_b, rg_wa, rg_ba, rg_wx, rg_bx, rg_lambda, rg_w_out, s5_a_re, s5_a_im, s5_b_re, s5_b_im, s5_c_re, s5_c_im, s5_d, s5_log_dt, s5_glu_w, s5_glu_b, s5_w_out, gdn_conv_w, gdn_a_log, gdn_dt_bias, gdn_norm_w, gdn_w_out, w_out, ffn_w_gate, ffn_w_up, ffn_w_down):
    p = dict(norm_pre_mix=norm_pre_mix, norm_post_mix=norm_post_mix, norm_pre_ffn=norm_pre_ffn,
             norm_post_ffn=norm_post_ffn, w_in=w_in, rg_conv_w=rg_conv_w, rg_conv_b=rg_conv_b, rg_wa=rg_wa,
             rg_ba=rg_ba, rg_wx=rg_wx, rg_bx=rg_bx, rg_lambda=rg_lambda, rg_w_out=rg_w_out, s5_a_re=s5_a_re,
             s5_a_im=s5_a_im, s5_b_re=s5_b_re, s5_b_im=s5_b_im, s5_c_re=s5_c_re, s5_c_im=s5_c_im, s5_d=s5_d,
             s5_log_dt=s5_log_dt, s5_glu_w=s5_glu_w, s5_glu_b=s5_glu_b, s5_w_out=s5_w_out,
             gdn_conv_w=gdn_conv_w, gdn_a_log=gdn_a_log, gdn_dt_bias=gdn_dt_bias, gdn_norm_w=gdn_norm_w,
             gdn_w_out=gdn_w_out, w_out=w_out, ffn_w_gate=ffn_w_gate, ffn_w_up=ffn_w_up, ffn_w_down=ffn_w_down)
    depth = w_in.shape[0]
    w = _prep_params(p)
    y_p, st_p = _run_group(x_prompt, _zero_states(depth, x_prompt.shape[0]), w,
                           tc=128, ck=64, gdn_nseq=4, gdn_stack_rows=MXU_EDGE, tiles_per_pass=8)
    y_s, st_s = _run_group(
        x_sample, (state_rg_conv, state_rg_h, state_s5_re, state_s5_im, state_gdn_conv, state_gdn_s), w,
        tc=x_sample.shape[1], ck=x_sample.shape[1], gdn_nseq=16, gdn_stack_rows=MXU_EDGE, tiles_per_pass=1)
    return (y_p, y_s, *st_p, *st_s)
```

```python
import functools

import jax
import jax.numpy as jnp
from jax import lax
from jax.experimental import pallas as pl
from jax.experimental.pallas import tpu as pltpu

F32 = jnp.float32
BF16 = jnp.bfloat16

D_MODEL = 1024
CONV_W = 4
EPS = 1e-6
RG_WIDTH = 512
RG_BLOCK = 64
RG_C = 8.0
S5_WIDTH = 512
S5_GROUP = 16
S5_GROUPS = 32
S5_STATE = 64
S5_FLAT = S5_GROUPS * S5_STATE
GDN_HEADS = 4
GDN_DK = 128
GDN_DV = 128
GDN_QK = 512
GDN_V = 512
QKV_WIDTH = 1536
D_FF = 2816

LANE = 128
SUBLANE = 8
BF16_ROWS = 16
MXU_EDGE = 256
MASKED_LOG = -1e30

COL_RG = 0
COL_S5 = 1024
COL_QKV = 1536
COL_Z = 3072
N_MAIN = 3584
N_BRANCH = 3
N_AB = LANE

VMEM_LIMIT = 52 * 1024 * 1024

IN_PROJ_TM, IN_PROJ_TN = 512, N_MAIN
MERGE_TM = 512
FFN_TM, FFN_TF = 1024, 256


def _cparams(sem):
    return pltpu.CompilerParams(dimension_semantics=sem, vmem_limit_bytes=VMEM_LIMIT)


def _lspec(layer, block, index_map):
    return pl.BlockSpec((None,) + tuple(block), lambda *g: (layer,) + tuple(index_map(*g)))


def _zeros_map(n):
    return lambda *g: (0,) * n


def _rms(x, g):
    return x * lax.rsqrt(jnp.mean(x * x, axis=-1, keepdims=True) + EPS) * g


def _dot(a, b):
    return jnp.dot(a, b, preferred_element_type=F32)


def _dot_nt(a, b):
    return lax.dot_general(a, b, (((1,), (1,)), ((), ())), preferred_element_type=F32)


def _dot_tn(a, b):
    return lax.dot_general(a, b, (((0,), (0,)), ((), ())), preferred_element_type=F32)


def _split2(a):
    hi = a.astype(BF16)
    return hi, (a - hi.astype(F32)).astype(BF16)


def _dot_split(a, b):
    a1, a2 = _split2(a)
    b1, b2 = _split2(b)
    return _dot(a1, b1) + (_dot(a1, b2) + _dot(a2, b1))


def _step_pitch(nt_chunk):
    return nt_chunk if nt_chunk <= SUBLANE else nt_chunk + SUBLANE // 2


def _put_rows(ref, tile, val, nb, tc, pitch):
    if pitch == tc:
        ref[tile, 0:nb * tc, :] = val
    else:
        for b in range(nb):
            ref[tile, b * pitch:b * pitch + tc, :] = val[b * tc:(b + 1) * tc, :]


def _get_rows(ref, tile, nb, tc, pitch):
    if pitch == tc:
        return ref[tile, 0:nb * tc, :]
    return jnp.concatenate([ref[tile, b * pitch:b * pitch + tc, :] for b in range(nb)], axis=0)


def _in_proj_kernel(x_ref, g_ref, w_ref, wab_ref, o_ref, oab_ref, h_ref):
    @pl.when(pl.program_id(1) == 0)
    def _():
        h = _rms(x_ref[...], g_ref[...]).astype(BF16)
        h_ref[...] = h
        oab_ref[...] = _dot(h, wab_ref[...])

    o_ref[...] = _dot(h_ref[...], w_ref[...])


def _in_proj(x, w, layer):
    m = x.shape[0]
    tm = min(IN_PROJ_TM, m)
    tn = IN_PROJ_TN
    return pl.pallas_call(
        _in_proj_kernel,
        grid=(m // tm, N_MAIN // tn),
        in_specs=[
            pl.BlockSpec((tm, D_MODEL), lambda i, j: (i, 0)),
            _lspec(layer, (1, D_MODEL), _zeros_map(2)),
            _lspec(layer, (D_MODEL, tn), lambda i, j: (0, j)),
            _lspec(layer, (D_MODEL, N_AB), _zeros_map(2)),
        ],
        out_specs=[
            pl.BlockSpec((tm, tn), lambda i, j: (i, j)),
            pl.BlockSpec((tm, N_AB), lambda i, j: (i, 0)),
        ],
        out_shape=[jax.ShapeDtypeStruct((m, N_MAIN), F32), jax.ShapeDtypeStruct((m, N_AB), F32)],
        scratch_shapes=[pltpu.VMEM((tm, D_MODEL), BF16)],
        compiler_params=_cparams(("parallel", "arbitrary")),
        name="in_proj",
    )(x, w["g_pre_mix"], w["w_main"], w["w_ab"])


def _rglru_kernel(p_ref, conv0_ref, h0_ref, cw_ref, cb_ref, wg_ref, ba_ref, bx_ref, lam_ref,
                  y_ref, hl_ref, xp_ref, a_ref, h_ref, hc_ref, *, nb, tc, pitch):
    ntile = RG_WIDTH // LANE
    first = SUBLANE - (CONV_W - 1)

    @pl.when(pl.program_id(0) == 0)
    def _():
        for p in range(ntile):
            xp_ref[:, p, 0:SUBLANE, :] = conv0_ref[:, :, p * LANE:(p + 1) * LANE]
        hc_ref[...] = h0_ref[...]

    cw = cw_ref[...]
    for p in range(ntile):
        cols = slice(p * LANE, (p + 1) * LANE)
        xp_ref[:, p, SUBLANE:SUBLANE + tc, :] = p_ref[:, :, cols]
        xc = cb_ref[:, cols] + xp_ref[:, p, first:first + tc, :] * cw[0:1, cols]
        for k in range(1, CONV_W):
            xc = xc + xp_ref[:, p, first + k:first + k + tc, :] * cw[k:k + 1, cols]
        xp_ref[:, p, 0:SUBLANE, :] = xp_ref[:, p, tc:tc + SUBLANE, :]
        xc = xc.reshape(nb * tc, LANE)
        gm = _dot(xc.astype(BF16), wg_ref[p])
        r = jax.nn.sigmoid(gm[:, 0:LANE] + ba_ref[:, cols])
        i = jax.nn.sigmoid(gm[:, LANE:2 * LANE] + bx_ref[:, cols])
        log_a = -RG_C * r * jax.nn.softplus(-lam_ref[:, cols])
        a = jnp.exp(log_a)
        _put_rows(a_ref, p, a, nb, tc, pitch)
        _put_rows(h_ref, p, jnp.sqrt(1.0 - a * a) * (i * xc), nb, tc, pitch)

    def step(t, hs):
        rs = pl.ds(t, nb, stride=pitch)
        out = []
        for p in range(ntile):
            h = a_ref[p, rs, :] * hs[p] + h_ref[p, rs, :]
            h_ref[p, rs, :] = h
            out.append(h)
        return tuple(out)

    hc = hc_ref[...]
    hs = lax.fori_loop(0, tc, step, tuple(hc[:, p * LANE:(p + 1) * LANE] for p in range(ntile)), unroll=2)
    for p in range(ntile):
        cols = slice(p * LANE, (p + 1) * LANE)
        hc_ref[:, cols] = hs[p]
        hl_ref[:, cols] = hs[p]
        h = _get_rows(h_ref, p, nb, tc, pitch).reshape(nb, tc, LANE)
        gate = p_ref[:, :, RG_WIDTH + p * LANE:RG_WIDTH + (p + 1) * LANE]
        y_ref[:, :, cols] = (h * jax.nn.gelu(gate)).astype(y_ref.dtype)


def _rglru(proj3, conv0, h0, w, layer, *, tc, ydtype):
    nb, nt, _ = proj3.shape
    pitch = _step_pitch(tc)
    ntile = RG_WIDTH // LANE
    row = _lspec(layer, (1, RG_WIDTH), _zeros_map(2))
    return pl.pallas_call(
        functools.partial(_rglru_kernel, nb=nb, tc=tc, pitch=pitch),
        grid=(nt // tc,),
        in_specs=[
            pl.BlockSpec((nb, tc, 2 * RG_WIDTH), lambda c: (0, c, COL_RG // (2 * RG_WIDTH))),
            _lspec(layer, (nb, SUBLANE, RG_WIDTH), _zeros_map(3)),
            _lspec(layer, (nb, RG_WIDTH), _zeros_map(2)),
            _lspec(layer, (CONV_W, RG_WIDTH), _zeros_map(2)),
            row,
            _lspec(layer, (ntile, LANE, 2 * LANE), _zeros_map(3)),
            row,
            row,
            row,
        ],
        out_specs=[
            pl.BlockSpec((nb, tc, RG_WIDTH), lambda c: (0, c, 0)),
            pl.BlockSpec((nb, RG_WIDTH), _zeros_map(2)),
        ],
        out_shape=[jax.ShapeDtypeStruct((nb, nt, RG_WIDTH), ydtype), jax.ShapeDtypeStruct((nb, RG_WIDTH), F32)],
        scratch_shapes=[
            pltpu.VMEM((nb, ntile, tc + SUBLANE, LANE), F32),
            pltpu.VMEM((ntile, nb * pitch, LANE), F32),
            pltpu.VMEM((ntile, nb * pitch, LANE), F32),
            pltpu.VMEM((nb, RG_WIDTH), F32),
        ],
        compiler_params=_cparams(("arbitrary",)),
        name="rglru",
    )(proj3, conv0, h0, w["rg_cw"], w["rg_cb"], w["rg_wg"], w["rg_ba"], w["rg_bx"], w["rg_lam"])


def _s5_kernel(u_ref, s0r_ref, s0i_ref, lr_ref, li_ref, bre_ref, bim_ref, cre_ref, cim_ref, d_ref, gw_ref, gb_ref,
               y_ref, sr_out, si_out, sre_ref, sim_ref, src_ref, sic_ref, *, nb, tc, pitch, tiles_per_pass):
    nq = S5_WIDTH // LANE
    qt = S5_FLAT // nq // LANE
    ntile = S5_FLAT // LANE

    @pl.when(pl.program_id(0) == 0)
    def _():
        src_ref[...] = s0r_ref[...]
        sic_ref[...] = s0i_ref[...]

    u = u_ref[...].reshape(nb * tc, S5_WIDTH)
    ub = u.astype(BF16)
    for q in range(nq):
        uq = ub[:, q * LANE:(q + 1) * LANE]
        bur = _dot(uq, bre_ref[q])
        bui = _dot(uq, bim_ref[q])
        for j in range(qt):
            _put_rows(sre_ref, q * qt + j, bur[:, j * LANE:(j + 1) * LANE], nb, tc, pitch)
            _put_rows(sim_ref, q * qt + j, bui[:, j * LANE:(j + 1) * LANE], nb, tc, pitch)

    for t0 in range(0, ntile, tiles_per_pass):
        tiles = list(range(t0, t0 + tiles_per_pass))
        lrs = [jnp.broadcast_to(lr_ref[:, tl * LANE:(tl + 1) * LANE], (nb, LANE)) for tl in tiles]
        lis = [jnp.broadcast_to(li_ref[:, tl * LANE:(tl + 1) * LANE], (nb, LANE)) for tl in tiles]

        def step(t, carry, tiles=tiles, lrs=lrs, lis=lis):
            rs = pl.ds(t, nb, stride=pitch)
            out = []
            for n, tl in enumerate(tiles):
                sr, si = carry[2 * n], carry[2 * n + 1]
                nr = lrs[n] * sr - lis[n] * si + sre_ref[tl, rs, :]
                ni = lrs[n] * si + lis[n] * sr + sim_ref[tl, rs, :]
                sre_ref[tl, rs, :] = nr
                sim_ref[tl, rs, :] = ni
                out += [nr, ni]
            return tuple(out)

        init = []
        for tl in tiles:
            init += [src_ref[:, tl * LANE:(tl + 1) * LANE], sic_ref[:, tl * LANE:(tl + 1) * LANE]]
        fin = lax.fori_loop(0, tc, step, tuple(init), unroll=2)
        for n, tl in enumerate(tiles):
            src_ref[:, tl * LANE:(tl + 1) * LANE] = fin[2 * n]
            sic_ref[:, tl * LANE:(tl + 1) * LANE] = fin[2 * n + 1]

    sr_out[...] = src_ref[...]
    si_out[...] = sic_ref[...]

    ys = []
    for q in range(nq):
        sr_q = jnp.concatenate([_get_rows(sre_ref, q * qt + j, nb, tc, pitch) for j in range(qt)], axis=1)
        si_q = jnp.concatenate([_get_rows(sim_ref, q * qt + j, nb, tc, pitch) for j in range(qt)], axis=1)
        ys.append(_dot(sr_q.astype(BF16), cre_ref[q]) - _dot(si_q.astype(BF16), cim_ref[q]))
    y = jnp.concatenate(ys, axis=1) + d_ref[...] * u
    y = jax.nn.gelu(y)
    y = y * jax.nn.sigmoid(_dot(y.astype(BF16), gw_ref[...]) + gb_ref[...])
    y_ref[...] = y.reshape(nb, tc, S5_WIDTH).astype(y_ref.dtype)


def _s5(proj3, s0r, s0i, w, layer, *, tc, tiles_per_pass, ydtype):
    nb, nt, _ = proj3.shape
    pitch = _step_pitch(tc)
    nq = S5_WIDTH // LANE
    qs = S5_FLAT // nq
    ntile = S5_FLAT // LANE
    state = _lspec(layer, (nb, S5_FLAT), _zeros_map(2))
    lam = _lspec(layer, (1, S5_FLAT), _zeros_map(2))
    b_in = _lspec(layer, (nq, LANE, qs), _zeros_map(3))
    c_out = _lspec(layer, (nq, qs, LANE), _zeros_map(3))
    row = _lspec(layer, (1, S5_WIDTH), _zeros_map(2))
    return pl.pallas_call(
        functools.partial(_s5_kernel, nb=nb, tc=tc, pitch=pitch, tiles_per_pass=tiles_per_pass),
        grid=(nt // tc,),
        in_specs=[
            pl.BlockSpec((nb, tc, S5_WIDTH), lambda c: (0, c, COL_S5 // S5_WIDTH)),
            state, state, lam, lam, b_in, b_in, c_out, c_out, row,
            _lspec(layer, (S5_WIDTH, S5_WIDTH), _zeros_map(2)),
            row,
        ],
        out_specs=[
            pl.BlockSpec((nb, tc, S5_WIDTH), lambda c: (0, c, 0)),
            pl.BlockSpec((nb, S5_FLAT), _zeros_map(2)),
            pl.BlockSpec((nb, S5_FLAT), _zeros_map(2)),
        ],
        out_shape=[
            jax.ShapeDtypeStruct((nb, nt, S5_WIDTH), ydtype),
            jax.ShapeDtypeStruct((nb, S5_FLAT), F32),
            jax.ShapeDtypeStruct((nb, S5_FLAT), F32),
        ],
        scratch_shapes=[
            pltpu.VMEM((ntile, nb * pitch, LANE), F32),
            pltpu.VMEM((ntile, nb * pitch, LANE), F32),
            pltpu.VMEM((nb, S5_FLAT), F32),
            pltpu.VMEM((nb, S5_FLAT), F32),
        ],
        compiler_params=_cparams(("arbitrary",)),
        name="s5",
    )(proj3, s0r, s0i, w["s5_lr"], w["s5_li"], w["s5_bre"], w["s5_bim"], w["s5_cre"], w["s5_cim"], w["s5_d"],
      w["s5_gw"], w["s5_gb"])


def _gdn_masks(sr, ck):
    shift = ck.bit_length() - 1
    row = lax.broadcasted_iota(jnp.int32, (sr, sr), 0)
    col = lax.broadcasted_iota(jnp.int32, (sr, sr), 1)
    same = (row >> shift) == (col >> shift)
    causal = same & (row >= col)
    strict = same & (row > col)
    eye = jnp.where(row == col, 1.0, 0.0).astype(F32)
    return causal, strict, eye


def _gdn_stack(blocks, ck, masks, xc, gc_all, beta_all, z, nw, s_ref, o_ref):
    nblk = len(blocks)
    sr = nblk * ck
    causal, strict, eye = masks

    def stack(fn):
        return jnp.concatenate([fn(g, h) for g, h in blocks], axis=0)

    q = stack(lambda g, h: xc[h][g])
    k = stack(lambda g, h: xc[GDN_HEADS + h][g])
    v = stack(lambda g, h: xc[2 * GDN_HEADS + h][g])
    zs = stack(lambda g, h: z[g, :, h * GDN_DV:(h + 1) * GDN_DV])
    gc = stack(lambda g, h: jnp.broadcast_to(gc_all[g, :, h:h + 1], (ck, LANE)))
    beta = stack(lambda g, h: jnp.broadcast_to(beta_all[g, :, GDN_HEADS + h:GDN_HEADS + h + 1], (ck, LANE)))

    q = q * lax.rsqrt(jnp.sum(q * q, axis=-1, keepdims=True) + EPS) * (GDN_DK ** -0.5)
    k = k * lax.rsqrt(jnp.sum(k * k, axis=-1, keepdims=True) + EPS)
    kb = k * beta
    vb = v * beta
    yield

    kbf = k.astype(BF16)
    kk = _dot_nt(kb.astype(BF16), kbf)
    qk = _dot_nt(q.astype(BF16), kbf)
    yield
    gc_lanes = jnp.transpose(gc)[0:SUBLANE, :]
    gc_rows = jnp.concatenate([gc] * (sr // LANE), axis=1) if sr > LANE else gc[:, 0:sr]
    seg = gc_rows - jnp.concatenate([gc_lanes] * (sr // SUBLANE), axis=0)
    decay = jnp.exp(jnp.where(causal, seg, MASKED_LOG))
    lmat = jnp.where(strict, kk * decay, 0.0)
    qk = (qk * decay).astype(BF16)

    x = -lmat
    p = _dot_split(lmat, lmat)
    yield
    x = x + p + _dot(x.astype(BF16), p.astype(BF16))
    n = 4
    while n < ck:
        pb = p.astype(BF16)
        p = _dot(pb, pb)
        yield
        x = x + p + _dot(x.astype(BF16), p.astype(BF16))
        yield
        n *= 2
    tmat = (eye + x).astype(BF16)
    egc = jnp.exp(gc)
    uw = _dot(tmat, jnp.concatenate([vb, kb * egc], axis=1).astype(BF16))
    yield
    u = uw[:, 0:GDN_DV]
    w = uw[:, GDN_DV:GDN_DV + GDN_DK]
    qg = q * egc
    g_last = jnp.concatenate(
        [jnp.broadcast_to(gc[(n + 1) * ck - 1:(n + 1) * ck, :], (ck, LANE)) for n in range(nblk)], axis=0)
    k_dec = (k * jnp.exp(g_last - gc)).astype(BF16)

    v_new, qs = [], []
    for n, (g, h) in enumerate(blocks):
        rs = slice(n * ck, (n + 1) * ck)
        sb = s_ref[g, h].astype(BF16)
        r = _dot(jnp.concatenate([w[rs], qg[rs]], axis=0).astype(BF16), sb)
        v_new.append(u[rs] - r[0:ck])
        qs.append(r[ck:2 * ck])
    yield
    v_new = jnp.concatenate(v_new, axis=0)
    vnb = v_new.astype(BF16)
    o = jnp.concatenate(qs, axis=0) + _dot(qk, vnb)
    for n, (g, h) in enumerate(blocks):
        rs = slice(n * ck, (n + 1) * ck)
        s_ref[g, h] = s_ref[g, h] * jnp.exp(gc[(n + 1) * ck - 1:(n + 1) * ck, :]) + _dot_tn(k_dec[rs], vnb[rs])
    yield

    o = _rms(o, nw) * (zs * jax.nn.sigmoid(zs))
    for n, (g, h) in enumerate(blocks):
        o_ref[g, :, h * GDN_DV:(h + 1) * GDN_DV] = o[n * ck:(n + 1) * ck].astype(o_ref.dtype)


def _gdn_kernel(q_ref, k_ref, v_ref, z_ref, ab_ref, conv0_ref, s0_ref, cw_ref, alog_ref, dtb_ref, nw_ref,
                *rest, ck, nseq, stack_rows):
    o_ref, sout_ref, xb_ref, s_ref, gsum_ref = rest[-5:]
    c = pl.program_id(1)
    ntile = QKV_WIDTH // LANE
    per_src = GDN_QK // LANE

    @pl.when(c == 0)
    def _():
        for lt in range(ntile):
            xb_ref[:, lt, 0:SUBLANE, :] = conv0_ref[:, :, lt * LANE:(lt + 1) * LANE]
        s_ref[...] = s0_ref[...]

    cw = cw_ref[...]
    first = SUBLANE - (CONV_W - 1)
    xc = []
    for lt in range(ntile):
        src = (q_ref, k_ref, v_ref)[lt // per_src]
        xb_ref[:, lt, SUBLANE:SUBLANE + ck, :] = src[:, :, (lt % per_src) * LANE:(lt % per_src + 1) * LANE]
        acc = xb_ref[:, lt, first:first + ck, :] * cw[0:1, lt * LANE:(lt + 1) * LANE]
        for j in range(1, CONV_W):
            acc = acc + xb_ref[:, lt, first + j:first + j + ck, :] * cw[j:j + 1, lt * LANE:(lt + 1) * LANE]
        xb_ref[:, lt, 0:SUBLANE, :] = xb_ref[:, lt, ck:ck + SUBLANE, :]
        xc.append(acc * jax.nn.sigmoid(acc))

    ab = ab_ref[...]
    g_all = -jnp.exp(alog_ref[...]) * jax.nn.softplus(ab + dtb_ref[...])
    gsum_ref[:, 0:ck, :] = jnp.zeros_like(g_all)
    gc_all = g_all
    d = 1
    while d < ck:
        gsum_ref[:, ck:2 * ck, :] = gc_all
        gc_all = gc_all + gsum_ref[:, ck - d:2 * ck - d, :]
        d *= 2
    beta_all = jax.nn.sigmoid(ab)
    z = z_ref[...]
    nw = nw_ref[...]
    masks = _gdn_masks(stack_rows, ck)
    blocks = [(g, h) for g in range(nseq) for h in range(GDN_HEADS)]
    per = stack_rows // ck
    stacks = [_gdn_stack(blocks[b0:b0 + per], ck, masks, xc, gc_all, beta_all, z, nw, s_ref, o_ref)
              for b0 in range(0, len(blocks), per)]
    while stacks:
        stacks = [s for s in stacks if next(s, True) is None]

    @pl.when(c == pl.num_programs(1) - 1)
    def _():
        sout_ref[...] = s_ref[...]


def _gdn(proj3, ab3, conv0, s_all, s_done, w, layer, *, ck, nseq, stack_rows, odtype):
    nb, nt, _ = proj3.shape
    assert (nseq * GDN_HEADS * ck) % stack_rows == 0 and stack_rows % ck == 0
    qblk = COL_QKV // GDN_QK
    state_spec = _lspec(layer, (nseq, GDN_HEADS, GDN_DK, GDN_DV), lambda i, c: (i, 0, 0, 0))
    head_row = _lspec(layer, (1, LANE), _zeros_map(2))
    operands = [proj3, proj3, proj3, proj3, ab3, conv0, s_all, w["gd_cw"], w["gd_alog"], w["gd_dtb"], w["gd_nw"]]
    extra_specs, aliases = [], {}
    if s_done is not None:
        aliases = {len(operands): 1}
        operands.append(s_done)
        extra_specs = [pl.BlockSpec(memory_space=pl.ANY)]
    return pl.pallas_call(
        functools.partial(_gdn_kernel, ck=ck, nseq=nseq, stack_rows=stack_rows),
        grid=(nb // nseq, nt // ck),
        input_output_aliases=aliases,
        in_specs=[
            pl.BlockSpec((nseq, ck, GDN_QK), lambda i, c: (i, c, qblk)),
            pl.BlockSpec((nseq, ck, GDN_QK), lambda i, c: (i, c, qblk + 1)),
            pl.BlockSpec((nseq, ck, GDN_V), lambda i, c: (i, c, qblk + 2)),
            pl.BlockSpec((nseq, ck, GDN_V), lambda i, c: (i, c, COL_Z // GDN_V)),
            pl.BlockSpec((nseq, ck, N_AB), lambda i, c: (i, c, 0)),
            _lspec(layer, (nseq, SUBLANE, QKV_WIDTH), lambda i, c: (i, 0, 0)),
            state_spec,
            _lspec(layer, (CONV_W, QKV_WIDTH), _zeros_map(2)),
            head_row,
            head_row,
            head_row,
        ] + extra_specs,
        out_specs=[
            pl.BlockSpec((nseq, ck, GDN_V), lambda i, c: (i, c, 0)),
            state_spec,
        ],
        out_shape=[
            jax.ShapeDtypeStruct((nb, nt, GDN_V), odtype),
            jax.ShapeDtypeStruct(s_all.shape, F32),
        ],
        scratch_shapes=[
            pltpu.VMEM((nseq, QKV_WIDTH // LANE, ck + SUBLANE, LANE), F32),
            pltpu.VMEM((nseq, GDN_HEADS, GDN_DK, GDN_DV), F32),
            pltpu.VMEM((nseq, 2 * ck, N_AB), F32),
        ],
        compiler_params=_cparams(("parallel", "arbitrary")),
        name="gdn",
    )(*operands)


def _merge_kernel(x_ref, yrg_ref, ys5_ref, ygd_ref, gpre_ref, wgate_ref, wrg_ref, ws5_ref, wgd_ref,
                  wo_ref, gpost_ref, o_ref):
    x = x_ref[...]
    h = _rms(x, gpre_ref[...]).astype(BF16)
    mixed = None
    for n, (y_ref, w_ref) in enumerate(((yrg_ref, wrg_ref), (ys5_ref, ws5_ref), (ygd_ref, wgd_ref))):
        gate = jax.nn.sigmoid(_dot(h, wgate_ref[:, n * D_MODEL:(n + 1) * D_MODEL]))
        term = gate * _dot(y_ref[...].astype(BF16), w_ref[...])
        mixed = term if mixed is None else mixed + term
    out = _dot(mixed.astype(BF16), wo_ref[...])
    o_ref[...] = x + _rms(out, gpost_ref[...])


def _merge(x, yrg, ys5, ygd, w, layer):
    m = x.shape[0]
    tm = min(MERGE_TM, m)
    half = D_MODEL // 2
    branch_out = _lspec(layer, (half, D_MODEL), _zeros_map(2))
    row = _lspec(layer, (1, D_MODEL), _zeros_map(2))
    return pl.pallas_call(
        _merge_kernel,
        grid=(m // tm,),
        in_specs=[
            pl.BlockSpec((tm, D_MODEL), lambda i: (i, 0)),
            pl.BlockSpec((tm, half), lambda i: (i, 0)),
            pl.BlockSpec((tm, half), lambda i: (i, 0)),
            pl.BlockSpec((tm, half), lambda i: (i, 0)),
            row,
            _lspec(layer, (D_MODEL, N_BRANCH * D_MODEL), _zeros_map(2)),
            branch_out, branch_out, branch_out,
            _lspec(layer, (D_MODEL, D_MODEL), _zeros_map(2)),
            row,
        ],
        out_specs=pl.BlockSpec((tm, D_MODEL), lambda i: (i, 0)),
        out_shape=jax.ShapeDtypeStruct((m, D_MODEL), F32),
        compiler_params=_cparams(("parallel",)),
        name="merge",
    )(x, yrg, ys5, ygd, w["g_pre_mix"], w["w_gates"], w["rg_wo"], w["s5_wo"], w["gd_wo"], w["w_out"],
      w["g_post_mix"])


def _ffn_kernel(x_ref, gpre_ref, wg_ref, wu_ref, wd_ref, gpost_ref, o_ref, h_ref, acc_ref):
    j = pl.program_id(1)

    @pl.when(j == 0)
    def _():
        h_ref[...] = _rms(x_ref[...], gpre_ref[...]).astype(BF16)
        acc_ref[...] = jnp.zeros_like(acc_ref)

    h = h_ref[...]
    gate = _dot(h, wg_ref[...])
    act = (gate * jax.nn.sigmoid(gate)) * _dot(h, wu_ref[...])
    acc_ref[...] += _dot(act.astype(BF16), wd_ref[...])

    @pl.when(j == pl.num_programs(1) - 1)
    def _():
        o_ref[...] = x_ref[...] + _rms(acc_ref[...], gpost_ref[...])


def _ffn(x, w, layer):
    m = x.shape[0]
    tm = min(FFN_TM, m)
    tf = FFN_TF
    row = _lspec(layer, (1, D_MODEL), _zeros_map(2))
    return pl.pallas_call(
        _ffn_kernel,
        grid=(m // tm, D_FF // tf),
        in_specs=[
            pl.BlockSpec((tm, D_MODEL), lambda i, j: (i, 0)),
            row,
            _lspec(layer, (D_MODEL, tf), lambda i, j: (0, j)),
            _lspec(layer, (D_MODEL, tf), lambda i, j: (0, j)),
            _lspec(layer, (tf, D_MODEL), lambda i, j: (j, 0)),
            row,
        ],
        out_specs=pl.BlockSpec((tm, D_MODEL), lambda i, j: (i, 0)),
        out_shape=jax.ShapeDtypeStruct((m, D_MODEL), F32),
        scratch_shapes=[pltpu.VMEM((tm, D_MODEL), BF16), pltpu.VMEM((tm, D_MODEL), F32)],
        compiler_params=_cparams(("parallel", "arbitrary")),
        name="ffn",
    )(x, w["g_pre_ffn"], w["ffn_wg"], w["ffn_wu"], w["ffn_wd"], w["g_post_ffn"])


def _block_diag(blocks):
    n, r, c = blocks.shape[-3:]
    eye = jnp.eye(n, dtype=blocks.dtype)
    out = jnp.einsum("...nrc,nm->...nrmc", blocks, eye)
    return out.reshape(blocks.shape[:-3] + (n * r, n * c))


def _prep_params(p):
    f32 = F32
    depth = p["w_in"].shape[0]
    w_in = p["w_in"]
    sizes = (RG_WIDTH, RG_WIDTH, S5_WIDTH, QKV_WIDTH, GDN_V, GDN_HEADS, GDN_HEADS, 3 * D_MODEL)
    offs = [0]
    for s in sizes:
        offs.append(offs[-1] + s)
    seg = lambda i: w_in[:, :, offs[i]:offs[i + 1]]
    w_main = w_in[:, :, 0:N_MAIN].astype(BF16)
    w_gates = seg(7).astype(BF16)
    w_ab = jnp.concatenate(
        [seg(5), seg(6), jnp.zeros((depth, D_MODEL, N_AB - 2 * GDN_HEADS), f32)], axis=2).astype(BF16)

    pair = LANE // RG_BLOCK
    pairs = lambda a: _block_diag(a.reshape(depth, RG_WIDTH // LANE, pair, RG_BLOCK, RG_BLOCK))
    wg = jnp.concatenate([pairs(p["rg_wa"]), pairs(p["rg_wx"])], axis=-1).astype(BF16)

    a_re = p["s5_a_re"].astype(f32)
    a_im = p["s5_a_im"].astype(f32)
    dt = jnp.exp(p["s5_log_dt"].astype(f32))[..., None]
    mag = jnp.exp(a_re * dt)
    lr = mag * jnp.cos(a_im * dt)
    li = mag * jnp.sin(a_im * dt)
    den = a_re * a_re + a_im * a_im
    cr = (((lr - 1.0) * a_re + li * a_im) / den)[..., None]
    ci = ((li * a_re - (lr - 1.0) * a_im) / den)[..., None]
    b_re = p["s5_b_re"].astype(f32)
    b_im = p["s5_b_im"].astype(f32)
    bb_re = cr * b_re - ci * b_im
    bb_im = cr * b_im + ci * b_re
    nq = S5_WIDTH // LANE
    gq = S5_GROUPS // nq
    to_in = lambda a: _block_diag(jnp.swapaxes(a, 2, 3).reshape(depth, nq, gq, S5_GROUP, S5_STATE))
    to_out = lambda a: _block_diag(jnp.swapaxes(a.astype(f32), 2, 3).reshape(depth, nq, gq, S5_STATE, S5_GROUP))

    row = lambda a: a.astype(f32).reshape(depth, 1, -1)
    pad4 = lambda a: jnp.pad(a.astype(f32), ((0, 0), (0, LANE - GDN_HEADS))).reshape(depth, 1, LANE)
    return dict(
        g_pre_mix=row(p["norm_pre_mix"]), g_post_mix=row(p["norm_post_mix"]),
        g_pre_ffn=row(p["norm_pre_ffn"]), g_post_ffn=row(p["norm_post_ffn"]),
        w_main=w_main, w_ab=w_ab, w_gates=w_gates,
        rg_cw=p["rg_conv_w"].astype(f32), rg_cb=row(p["rg_conv_b"]), rg_wg=wg,
        rg_ba=row(p["rg_ba"]), rg_bx=row(p["rg_bx"]), rg_lam=row(p["rg_lambda"]),
        rg_wo=p["rg_w_out"].astype(BF16),
        s5_lr=lr.reshape(depth, 1, S5_FLAT), s5_li=li.reshape(depth, 1, S5_FLAT),
        s5_bre=to_in(bb_re).astype(BF16), s5_bim=to_in(bb_im).astype(BF16),
        s5_cre=to_out(p["s5_c_re"]).astype(BF16), s5_cim=to_out(p["s5_c_im"]).astype(BF16),
        s5_d=row(p["s5_d"]), s5_gw=p["s5_glu_w"].astype(BF16), s5_gb=row(p["s5_glu_b"]),
        s5_wo=p["s5_w_out"].astype(BF16),
        gd_cw=p["gdn_conv_w"].astype(f32), gd_alog=pad4(p["gdn_a_log"]), gd_dtb=pad4(p["gdn_dt_bias"]),
        gd_nw=row(p["gdn_norm_w"]), gd_wo=p["gdn_w_out"].astype(BF16),
        w_out=p["w_out"].astype(BF16),
        ffn_wg=p["ffn_w_gate"].astype(BF16), ffn_wu=p["ffn_w_up"].astype(BF16),
        ffn_wd=p["ffn_w_down"].astype(BF16),
    )


def _pad_conv_state(s):
    return jnp.pad(s, ((0, 0), (0, 0), (SUBLANE - (CONV_W - 1), 0), (0, 0)))


def _run_group(x, states, w, *, tc, ck, gdn_nseq, gdn_stack_rows, tiles_per_pass):
    nb, nt, _ = x.shape
    keep = CONV_W - 1
    assert nt >= keep and nt % SUBLANE == 0 and nb % SUBLANE == 0
    st_rg_conv, st_rg_h, st_s5_re, st_s5_im, st_gd_conv, st_gd_s = states
    depth = st_rg_h.shape[0]
    rg_conv0 = _pad_conv_state(st_rg_conv)
    gd_conv0 = _pad_conv_state(st_gd_conv)
    s5_re0 = st_s5_re.reshape(depth, nb, S5_FLAT)
    s5_im0 = st_s5_im.reshape(depth, nb, S5_FLAT)
    ydtype = BF16 if (tc % BF16_ROWS == 0 and ck % BF16_ROWS == 0) else F32
    xf = x.reshape(nb * nt, D_MODEL)
    flat = lambda a: a.reshape(nb * nt, a.shape[-1])
    outs = [[] for _ in range(5)]
    s_gd = None
    for l in range(depth):
        proj, ab = _in_proj(xf, w, l)
        proj3 = proj.reshape(nb, nt, N_MAIN)
        y_rg, h_last = _rglru(proj3, rg_conv0, st_rg_h, w, l, tc=tc, ydtype=ydtype)
        y_s5, s_re, s_im = _s5(proj3, s5_re0, s5_im0, w, l, tc=tc, tiles_per_pass=tiles_per_pass, ydtype=ydtype)
        y_gd, s_gd = _gdn(proj3, ab.reshape(nb, nt, N_AB), gd_conv0, st_gd_s, s_gd, w, l, ck=ck, nseq=gdn_nseq,
                          stack_rows=gdn_stack_rows, odtype=ydtype)
        x1 = _merge(xf, flat(y_rg), flat(y_s5), flat(y_gd), w, l)
        xf = _ffn(x1, w, l)

        outs[0].append(proj3[:, nt - keep:, COL_RG:COL_RG + RG_WIDTH])
        outs[1].append(h_last)
        outs[2].append(s_re.reshape(nb, S5_GROUPS, S5_STATE))
        outs[3].append(s_im.reshape(nb, S5_GROUPS, S5_STATE))
        outs[4].append(proj3[:, nt - keep:, COL_QKV:COL_QKV + QKV_WIDTH])
    return xf.reshape(nb, nt, D_MODEL), [jnp.stack(o) for o in outs] + [s_gd]


def _zero_states(depth, nb):
    return (jnp.zeros((depth, nb, CONV_W - 1, RG_WIDTH), F32),
            jnp.zeros((depth, nb, RG_WIDTH), F32),
            jnp.zeros((depth, nb, S5_GROUPS, S5_STATE), F32),
            jnp.zeros((depth, nb, S5_GROUPS, S5_STATE), F32),
            jnp.zeros((depth, nb, CONV_W - 1, QKV_WIDTH), F32),
            jnp.zeros((depth, nb, GDN_HEADS, GDN_DK, GDN_DV), F32))


def kernel(x_prompt, x_sample, state_rg_conv, state_rg_h, state_s5_re, state_s5_im, state_gdn_conv, state_gdn_s, norm_pre_mix, norm_post_mix, norm_pre_ffn, norm_post_ffn, w_in, rg_conv_w, rg_conv_b, rg_wa, rg_ba, rg_wx, rg_bx, rg_lambda, rg_w_out, s5_a_re, s5_a_im, s5_b_re, s5_b_im, s5_c_re, s5_c_im, s5_d, s5_log_dt, s5_glu_w, s5_glu_b, s5_w_out, gdn_conv_w, gdn_a_log, gdn_dt_bias, gdn_norm_w, gdn_w_out, w_out, ffn_w_gate, ffn_w_up, ffn_w_down):
    p = dict(norm_pre_mix=norm_pre_mix, norm_post_mix=norm_post_mix, norm_pre_ffn=norm_pre_ffn,
             norm_post_ffn=norm_post_ffn, w_in=w_in, rg_conv_w=rg_conv_w, rg_conv_b=rg_conv_b, rg_wa=rg_wa,
             rg_ba=rg_ba, rg_wx=rg_wx, rg_bx=rg_bx, rg_lambda=rg_lambda, rg_w_out=rg_w_out, s5_a_re=s5_a_re,
             s5_a_im=s5_a_im, s5_b_re=s5_b_re, s5_b_im=s5_b_im, s5_c_re=s5_c_re, s5_c_im=s5_c_im, s5_d=s5_d,
             s5_log_dt=s5_log_dt, s5_glu_w=s5_glu_w, s5_glu_b=s5_glu_b, s5_w_out=s5_w_out,
             gdn_conv_w=gdn_conv_w, gdn_a_log=gdn_a_log, gdn_dt_bias=gdn_dt_bias, gdn_norm_w=gdn_norm_w,
             gdn_w_out=gdn_w_out, w_out=w_out, ffn_w_gate=ffn_w_gate, ffn_w_up=ffn_w_up, ffn_w_down=ffn_w_down)
    depth = w_in.shape[0]
    w = _prep_params(p)
    y_p, st_p = _run_group(x_prompt, _zero_states(depth, x_prompt.shape[0]), w,
                           tc=128, ck=64, gdn_nseq=4, gdn_stack_rows=MXU_EDGE, tiles_per_pass=8)
    y_s, st_s = _run_group(
        x_sample, (state_rg_conv, state_rg_h, state_s5_re, state_s5_im, state_gdn_conv, state_gdn_s), w,
        tc=x_sample.shape[1], ck=x_sample.shape[1], gdn_nseq=16, gdn_stack_rows=MXU_EDGE, tiles_per_pass=1)
    return (y_p, y_s, *st_p, *st_s)
```

```python
import functools

import jax
import jax.numpy as jnp
from jax import lax
from jax.experimental import pallas as pl
from jax.experimental.pallas import tpu as pltpu

F32 = jnp.float32
BF16 = jnp.bfloat16

D_MODEL = 1024
CONV_W = 4
EPS = 1e-6
RG_WIDTH = 512
RG_BLOCK = 64
RG_C = 8.0
S5_WIDTH = 512
S5_GROUP = 16
S5_GROUPS = 32
S5_STATE = 64
S5_FLAT = S5_GROUPS * S5_STATE
GDN_HEADS = 4
GDN_DK = 128
GDN_DV = 128
GDN_QK = 512
GDN_V = 512
QKV_WIDTH = 1536
D_FF = 2816

LANE = 128
SUBLANE = 8
BF16_ROWS = 16
MXU_EDGE = 256
MASKED_LOG = -1e30

COL_RG = 0
COL_S5 = 1024
COL_QKV = 1536
COL_Z = 3072
N_MAIN = 3584
N_BRANCH = 3
N_AB = LANE

VMEM_LIMIT = 52 * 1024 * 1024

IN_PROJ_TM = 1024
MERGE_TM = 512
FFN_TM, FFN_TF = 512, 256


def _cparams(sem):
    return pltpu.CompilerParams(dimension_semantics=sem, vmem_limit_bytes=VMEM_LIMIT)


def _lspec(layer, block, index_map):
    return pl.BlockSpec((None,) + tuple(block), lambda *g: (layer,) + tuple(index_map(*g)))


def _zeros_map(n):
    return lambda *g: (0,) * n


def _rms(x, g):
    return x * lax.rsqrt(jnp.mean(x * x, axis=-1, keepdims=True) + EPS) * g


def _dot(a, b):
    return jnp.dot(a, b, preferred_element_type=F32)


def _dot_nt(a, b):
    return lax.dot_general(a, b, (((1,), (1,)), ((), ())), preferred_element_type=F32)


def _dot_tn(a, b):
    return lax.dot_general(a, b, (((0,), (0,)), ((), ())), preferred_element_type=F32)


def _split2(a):
    hi = a.astype(BF16)
    return hi, (a - hi.astype(F32)).astype(BF16)


def _dot_split(a, b):
    a1, a2 = _split2(a)
    b1, b2 = _split2(b)
    return _dot(a1, b1) + (_dot(a1, b2) + _dot(a2, b1))


def _step_pitch(nt_chunk):
    return nt_chunk if nt_chunk <= SUBLANE else nt_chunk + SUBLANE // 2


def _put_rows(ref, tile, val, nb, tc, pitch):
    if pitch == tc:
        ref[tile, 0:nb * tc, :] = val
    else:
        for b in range(nb):
            ref[tile, b * pitch:b * pitch + tc, :] = val[b * tc:(b + 1) * tc, :]


def _get_rows(ref, tile, nb, tc, pitch):
    if pitch == tc:
        return ref[tile, 0:nb * tc, :]
    return jnp.concatenate([ref[tile, b * pitch:b * pitch + tc, :] for b in range(nb)], axis=0)


def _in_proj_kernel(x_ref, g_ref, w_ref, wab_ref, o_ref, oab_ref):
    h = _rms(x_ref[...], g_ref[...]).astype(BF16)
    oab_ref[...] = _dot(h, wab_ref[...])
    o_ref[...] = _dot(h, w_ref[...])


def _in_proj(x, w, layer):
    m = x.shape[0]
    tm = min(IN_PROJ_TM, m)
    once = pl.Buffered(1)
    return pl.pallas_call(
        _in_proj_kernel,
        grid=(m // tm,),
        in_specs=[
            pl.BlockSpec((tm, D_MODEL), lambda i: (i, 0)),
            _lspec(layer, (1, D_MODEL), _zeros_map(2)),
            pl.BlockSpec((None, D_MODEL, N_MAIN), lambda i: (layer, 0, 0), pipeline_mode=once),
            pl.BlockSpec((None, D_MODEL, N_AB), lambda i: (layer, 0, 0), pipeline_mode=once),
        ],
        out_specs=[
            pl.BlockSpec((tm, N_MAIN), lambda i: (i, 0)),
            pl.BlockSpec((tm, N_AB), lambda i: (i, 0)),
        ],
        out_shape=[jax.ShapeDtypeStruct((m, N_MAIN), F32), jax.ShapeDtypeStruct((m, N_AB), F32)],
        compiler_params=_cparams(("parallel",)),
        name="in_proj",
    )(x, w["g_pre_mix"], w["w_main"], w["w_ab"])


def _rglru_kernel(p_ref, conv0_ref, h0_ref, cw_ref, cb_ref, wg_ref, ba_ref, bx_ref, lam_ref,
                  y_ref, hl_ref, xp_ref, a_ref, h_ref, hc_ref, *, nb, tc, pitch):
    ntile = RG_WIDTH // LANE
    first = SUBLANE - (CONV_W - 1)

    @pl.when(pl.program_id(0) == 0)
    def _():
        for p in range(ntile):
            xp_ref[:, p, 0:SUBLANE, :] = conv0_ref[:, :, p * LANE:(p + 1) * LANE]
        hc_ref[...] = h0_ref[...]

    cw = cw_ref[...]
    for p in range(ntile):
        cols = slice(p * LANE, (p + 1) * LANE)
        xp_ref[:, p, SUBLANE:SUBLANE + tc, :] = p_ref[:, :, cols]
        xc = cb_ref[:, cols] + xp_ref[:, p, first:first + tc, :] * cw[0:1, cols]
        for k in range(1, CONV_W):
            xc = xc + xp_ref[:, p, first + k:first + k + tc, :] * cw[k:k + 1, cols]
        xp_ref[:, p, 0:SUBLANE, :] = xp_ref[:, p, tc:tc + SUBLANE, :]
        xc = xc.reshape(nb * tc, LANE)
        gm = _dot(xc.astype(BF16), wg_ref[p])
        r = jax.nn.sigmoid(gm[:, 0:LANE] + ba_ref[:, cols])
        i = jax.nn.sigmoid(gm[:, LANE:2 * LANE] + bx_ref[:, cols])
        log_a = -RG_C * r * jax.nn.softplus(-lam_ref[:, cols])
        a = jnp.exp(log_a)
        _put_rows(a_ref, p, a, nb, tc, pitch)
        _put_rows(h_ref, p, jnp.sqrt(1.0 - a * a) * (i * xc), nb, tc, pitch)

    def step(t, hs):
        rs = pl.ds(t, nb, stride=pitch)
        out = []
        for p in range(ntile):
            h = a_ref[p, rs, :] * hs[p] + h_ref[p, rs, :]
            h_ref[p, rs, :] = h
            out.append(h)
        return tuple(out)

    hc = hc_ref[...]
    hs = lax.fori_loop(0, tc, step, tuple(hc[:, p * LANE:(p + 1) * LANE] for p in range(ntile)), unroll=2)
    for p in range(ntile):
        cols = slice(p * LANE, (p + 1) * LANE)
        hc_ref[:, cols] = hs[p]
        hl_ref[:, cols] = hs[p]
        h = _get_rows(h_ref, p, nb, tc, pitch).reshape(nb, tc, LANE)
        gate = p_ref[:, :, RG_WIDTH + p * LANE:RG_WIDTH + (p + 1) * LANE]
        y_ref[:, :, cols] = (h * jax.nn.gelu(gate)).astype(y_ref.dtype)


def _rglru(proj3, conv0, h0, w, layer, *, tc, ydtype):
    nb, nt, _ = proj3.shape
    pitch = _step_pitch(tc)
    ntile = RG_WIDTH // LANE
    row = _lspec(layer, (1, RG_WIDTH), _zeros_map(2))
    return pl.pallas_call(
        functools.partial(_rglru_kernel, nb=nb, tc=tc, pitch=pitch),
        grid=(nt // tc,),
        in_specs=[
            pl.BlockSpec((nb, tc, 2 * RG_WIDTH), lambda c: (0, c, COL_RG // (2 * RG_WIDTH))),
            _lspec(layer, (nb, SUBLANE, RG_WIDTH), _zeros_map(3)),
            _lspec(layer, (nb, RG_WIDTH), _zeros_map(2)),
            _lspec(layer, (CONV_W, RG_WIDTH), _zeros_map(2)),
            row,
            _lspec(layer, (ntile, LANE, 2 * LANE), _zeros_map(3)),
            row,
            row,
            row,
        ],
        out_specs=[
            pl.BlockSpec((nb, tc, RG_WIDTH), lambda c: (0, c, 0)),
            pl.BlockSpec((nb, RG_WIDTH), _zeros_map(2)),
        ],
        out_shape=[jax.ShapeDtypeStruct((nb, nt, RG_WIDTH), ydtype), jax.ShapeDtypeStruct((nb, RG_WIDTH), F32)],
        scratch_shapes=[
            pltpu.VMEM((nb, ntile, tc + SUBLANE, LANE), F32),
            pltpu.VMEM((ntile, nb * pitch, LANE), F32),
            pltpu.VMEM((ntile, nb * pitch, LANE), F32),
            pltpu.VMEM((nb, RG_WIDTH), F32),
        ],
        compiler_params=_cparams(("arbitrary",)),
        name="rglru",
    )(proj3, conv0, h0, w["rg_cw"], w["rg_cb"], w["rg_wg"], w["rg_ba"], w["rg_bx"], w["rg_lam"])


def _s5_kernel(u_ref, s0r_ref, s0i_ref, lr_ref, li_ref, bre_ref, bim_ref, cre_ref, cim_ref, d_ref, gw_ref, gb_ref,
               y_ref, sr_out, si_out, sre_ref, sim_ref, src_ref, sic_ref, *, nb, tc, pitch, tiles_per_pass):
    nq = S5_WIDTH // LANE
    qt = S5_FLAT // nq // LANE
    ntile = S5_FLAT // LANE

    @pl.when(pl.program_id(0) == 0)
    def _():
        src_ref[...] = s0r_ref[...]
        sic_ref[...] = s0i_ref[...]

    u = u_ref[...].reshape(nb * tc, S5_WIDTH)
    ub = u.astype(BF16)
    for q in range(nq):
        uq = ub[:, q * LANE:(q + 1) * LANE]
        bur = _dot(uq, bre_ref[q])
        bui = _dot(uq, bim_ref[q])
        for j in range(qt):
            _put_rows(sre_ref, q * qt + j, bur[:, j * LANE:(j + 1) * LANE], nb, tc, pitch)
            _put_rows(sim_ref, q * qt + j, bui[:, j * LANE:(j + 1) * LANE], nb, tc, pitch)

    for t0 in range(0, ntile, tiles_per_pass):
        tiles = list(range(t0, t0 + tiles_per_pass))
        lrs = [jnp.broadcast_to(lr_ref[:, tl * LANE:(tl + 1) * LANE], (nb, LANE)) for tl in tiles]
        lis = [jnp.broadcast_to(li_ref[:, tl * LANE:(tl + 1) * LANE], (nb, LANE)) for tl in tiles]

        def step(t, carry, tiles=tiles, lrs=lrs, lis=lis):
            rs = pl.ds(t, nb, stride=pitch)
            out = []
            for n, tl in enumerate(tiles):
                sr, si = carry[2 * n], carry[2 * n + 1]
                nr = lrs[n] * sr - lis[n] * si + sre_ref[tl, rs, :]
                ni = lrs[n] * si + lis[n] * sr + sim_ref[tl, rs, :]
                sre_ref[tl, rs, :] = nr
                sim_ref[tl, rs, :] = ni
                out += [nr, ni]
            return tuple(out)

        init = []
        for tl in tiles:
            init += [src_ref[:, tl * LANE:(tl + 1) * LANE], sic_ref[:, tl * LANE:(tl + 1) * LANE]]
        fin = lax.fori_loop(0, tc, step, tuple(init), unroll=2)
        for n, tl in enumerate(tiles):
            src_ref[:, tl * LANE:(tl + 1) * LANE] = fin[2 * n]
            sic_ref[:, tl * LANE:(tl + 1) * LANE] = fin[2 * n + 1]

    sr_out[...] = src_ref[...]
    si_out[...] = sic_ref[...]

    ys = []
    for q in range(nq):
        sr_q = jnp.concatenate([_get_rows(sre_ref, q * qt + j, nb, tc, pitch) for j in range(qt)], axis=1)
        si_q = jnp.concatenate([_get_rows(sim_ref, q * qt + j, nb, tc, pitch) for j in range(qt)], axis=1)
        ys.append(_dot(sr_q.astype(BF16), cre_ref[q]) - _dot(si_q.astype(BF16), cim_ref[q]))
    y = jnp.concatenate(ys, axis=1) + d_ref[...] * u
    y = jax.nn.gelu(y)
    y = y * jax.nn.sigmoid(_dot(y.astype(BF16), gw_ref[...]) + gb_ref[...])
    y_ref[...] = y.reshape(nb, tc, S5_WIDTH).astype(y_ref.dtype)


def _s5(proj3, s0r, s0i, w, layer, *, tc, tiles_per_pass, ydtype):
    nb, nt, _ = proj3.shape
    pitch = _step_pitch(tc)
    nq = S5_WIDTH // LANE
    qs = S5_FLAT // nq
    ntile = S5_FLAT // LANE
    state = _lspec(layer, (nb, S5_FLAT), _zeros_map(2))
    lam = _lspec(layer, (1, S5_FLAT), _zeros_map(2))
    b_in = _lspec(layer, (nq, LANE, qs), _zeros_map(3))
    c_out = _lspec(layer, (nq, qs, LANE), _zeros_map(3))
    row = _lspec(layer, (1, S5_WIDTH), _zeros_map(2))
    return pl.pallas_call(
        functools.partial(_s5_kernel, nb=nb, tc=tc, pitch=pitch, tiles_per_pass=tiles_per_pass),
        grid=(nt // tc,),
        in_specs=[
            pl.BlockSpec((nb, tc, S5_WIDTH), lambda c: (0, c, COL_S5 // S5_WIDTH)),
            state, state, lam, lam, b_in, b_in, c_out, c_out, row,
            _lspec(layer, (S5_WIDTH, S5_WIDTH), _zeros_map(2)),
            row,
        ],
        out_specs=[
            pl.BlockSpec((nb, tc, S5_WIDTH), lambda c: (0, c, 0)),
            pl.BlockSpec((nb, S5_FLAT), _zeros_map(2)),
            pl.BlockSpec((nb, S5_FLAT), _zeros_map(2)),
        ],
        out_shape=[
            jax.ShapeDtypeStruct((nb, nt, S5_WIDTH), ydtype),
            jax.ShapeDtypeStruct((nb, S5_FLAT), F32),
            jax.ShapeDtypeStruct((nb, S5_FLAT), F32),
        ],
        scratch_shapes=[
            pltpu.VMEM((ntile, nb * pitch, LANE), F32),
            pltpu.VMEM((ntile, nb * pitch, LANE), F32),
            pltpu.VMEM((nb, S5_FLAT), F32),
            pltpu.VMEM((nb, S5_FLAT), F32),
        ],
        compiler_params=_cparams(("arbitrary",)),
        name="s5",
    )(proj3, s0r, s0i, w["s5_lr"], w["s5_li"], w["s5_bre"], w["s5_bim"], w["s5_cre"], w["s5_cim"], w["s5_d"],
      w["s5_gw"], w["s5_gb"])


def _gdn_masks(sr, ck):
    shift = ck.bit_length() - 1
    row = lax.broadcasted_iota(jnp.int32, (sr, sr), 0)
    col = lax.broadcasted_iota(jnp.int32, (sr, sr), 1)
    same = (row >> shift) == (col >> shift)
    causal = same & (row >= col)
    strict = same & (row > col)
    eye = jnp.where(row == col, 1.0, 0.0).astype(F32)
    return causal, strict, eye


def _gdn_stack(blocks, ck, masks, xc, gc_all, beta_all, z, nw, s_ref, o_ref):
    nblk = len(blocks)
    sr = nblk * ck
    causal, strict, eye = masks

    def stack(fn):
        return jnp.concatenate([fn(g, h) for g, h in blocks], axis=0)

    q = stack(lambda g, h: xc[h][g])
    k = stack(lambda g, h: xc[GDN_HEADS + h][g])
    v = stack(lambda g, h: xc[2 * GDN_HEADS + h][g])
    zs = stack(lambda g, h: z[g, :, h * GDN_DV:(h + 1) * GDN_DV])
    gc = stack(lambda g, h: jnp.broadcast_to(gc_all[g, :, h:h + 1], (ck, LANE)))
    beta = stack(lambda g, h: jnp.broadcast_to(beta_all[g, :, GDN_HEADS + h:GDN_HEADS + h + 1], (ck, LANE)))

    q = q * lax.rsqrt(jnp.sum(q * q, axis=-1, keepdims=True) + EPS) * (GDN_DK ** -0.5)
    k = k * lax.rsqrt(jnp.sum(k * k, axis=-1, keepdims=True) + EPS)
    kb = k * beta
    vb = v * beta
    yield

    kbf = k.astype(BF16)
    kk = _dot_nt(kb.astype(BF16), kbf)
    qk = _dot_nt(q.astype(BF16), kbf)
    yield
    gc_lanes = jnp.transpose(gc)[0:SUBLANE, :]
    gc_rows = jnp.concatenate([gc] * (sr // LANE), axis=1) if sr > LANE else gc[:, 0:sr]
    seg = gc_rows - jnp.concatenate([gc_lanes] * (sr // SUBLANE), axis=0)
    decay = jnp.exp(jnp.where(causal, seg, MASKED_LOG))
    lmat = jnp.where(strict, kk * decay, 0.0)
    qk = (qk * decay).astype(BF16)

    x = -lmat
    p = _dot_split(lmat, lmat)
    yield
    x = x + p + _dot(x.astype(BF16), p.astype(BF16))
    n = 4
    while n < ck:
        pb = p.astype(BF16)
        p = _dot(pb, pb)
        yield
        x = x + p + _dot(x.astype(BF16), p.astype(BF16))
        yield
        n *= 2
    tmat = (eye + x).astype(BF16)
    egc = jnp.exp(gc)
    uw = _dot(tmat, jnp.concatenate([vb, kb * egc], axis=1).astype(BF16))
    yield
    u = uw[:, 0:GDN_DV]
    w = uw[:, GDN_DV:GDN_DV + GDN_DK]
    qg = q * egc
    g_last = jnp.concatenate(
        [jnp.broadcast_to(gc[(n + 1) * ck - 1:(n + 1) * ck, :], (ck, LANE)) for n in range(nblk)], axis=0)
    k_dec = (k * jnp.exp(g_last - gc)).astype(BF16)

    v_new, qs = [], []
    for n, (g, h) in enumerate(blocks):
        rs = slice(n * ck, (n + 1) * ck)
        sb = s_ref[g, h].astype(BF16)
        r = _dot(jnp.concatenate([w[rs], qg[rs]], axis=0).astype(BF16), sb)
        v_new.append(u[rs] - r[0:ck])
        qs.append(r[ck:2 * ck])
    yield
    v_new = jnp.concatenate(v_new, axis=0)
    vnb = v_new.astype(BF16)
    o = jnp.concatenate(qs, axis=0) + _dot(qk, vnb)
    for n, (g, h) in enumerate(blocks):
        rs = slice(n * ck, (n + 1) * ck)
        s_ref[g, h] = s_ref[g, h] * jnp.exp(gc[(n + 1) * ck - 1:(n + 1) * ck, :]) + _dot_tn(k_dec[rs], vnb[rs])
    yield

    o = _rms(o, nw) * (zs * jax.nn.sigmoid(zs))
    for n, (g, h) in enumerate(blocks):
        o_ref[g, :, h * GDN_DV:(h + 1) * GDN_DV] = o[n * ck:(n + 1) * ck].astype(o_ref.dtype)


def _gdn_kernel(q_ref, k_ref, v_ref, z_ref, ab_ref, conv0_ref, s0_ref, cw_ref, alog_ref, dtb_ref, nw_ref,
                *rest, ck, nseq, stack_rows, own_layer):
    o_ref, sout_ref, xb_ref, s_ref, gsum_ref = rest[-5:]
    c = pl.program_id(1)
    ntile = QKV_WIDTH // LANE
    per_src = GDN_QK // LANE

    @pl.when(c == 0)
    def _():
        for lt in range(ntile):
            xb_ref[:, lt, 0:SUBLANE, :] = conv0_ref[:, :, lt * LANE:(lt + 1) * LANE]
        s_ref[...] = s0_ref[...]

    cw = cw_ref[...]
    first = SUBLANE - (CONV_W - 1)
    xc = []
    for lt in range(ntile):
        src = (q_ref, k_ref, v_ref)[lt // per_src]
        xb_ref[:, lt, SUBLANE:SUBLANE + ck, :] = src[:, :, (lt % per_src) * LANE:(lt % per_src + 1) * LANE]
        acc = xb_ref[:, lt, first:first + ck, :] * cw[0:1, lt * LANE:(lt + 1) * LANE]
        for j in range(1, CONV_W):
            acc = acc + xb_ref[:, lt, first + j:first + j + ck, :] * cw[j:j + 1, lt * LANE:(lt + 1) * LANE]
        xb_ref[:, lt, 0:SUBLANE, :] = xb_ref[:, lt, ck:ck + SUBLANE, :]
        xc.append(acc * jax.nn.sigmoid(acc))

    ab = ab_ref[...]
    g_all = -jnp.exp(alog_ref[...]) * jax.nn.softplus(ab + dtb_ref[...])
    gsum_ref[:, 0:ck, :] = jnp.zeros_like(g_all)
    gc_all = g_all
    d = 1
    while d < ck:
        gsum_ref[:, ck:2 * ck, :] = gc_all
        gc_all = gc_all + gsum_ref[:, ck - d:2 * ck - d, :]
        d *= 2
    beta_all = jax.nn.sigmoid(ab)
    z = z_ref[...]
    nw = nw_ref[...]
    masks = _gdn_masks(stack_rows, ck)
    blocks = [(g, h) for g in range(nseq) for h in range(GDN_HEADS)]
    per = stack_rows // ck
    stacks = [_gdn_stack(blocks[b0:b0 + per], ck, masks, xc, gc_all, beta_all, z, nw, s_ref, o_ref)
              for b0 in range(0, len(blocks), per)]
    while stacks:
        stacks = [s for s in stacks if next(s, True) is None]

    @pl.when(c == pl.num_programs(1) - 1)
    def _():
        if own_layer is None:
            sout_ref[...] = s_ref[...]
        else:
            sout_ref[...] = jnp.zeros_like(sout_ref)
            sout_ref[own_layer] = s_ref[...]


def _gdn(proj3, ab3, conv0, s_all, s_done, w, layer, *, ck, nseq, stack_rows, odtype):
    nb, nt, _ = proj3.shape
    assert (nseq * GDN_HEADS * ck) % stack_rows == 0 and stack_rows % ck == 0
    qblk = COL_QKV // GDN_QK
    state_block = (nseq, GDN_HEADS, GDN_DK, GDN_DV)
    state_spec = _lspec(layer, state_block, lambda i, c: (i, 0, 0, 0))
    head_row = _lspec(layer, (1, LANE), _zeros_map(2))
    operands = [proj3, proj3, proj3, proj3, ab3, conv0, s_all, w["gd_cw"], w["gd_alog"], w["gd_dtb"], w["gd_nw"]]
    extra_specs, aliases = [], {}
    if s_done is not None:
        aliases = {len(operands): 1}
        operands.append(s_done)
        extra_specs = [pl.BlockSpec(memory_space=pl.ANY)]
        out_state_spec, own_layer = state_spec, None
    else:
        out_state_spec = pl.BlockSpec((s_all.shape[0],) + state_block, lambda i, c: (0, i, 0, 0, 0))
        own_layer = layer
    return pl.pallas_call(
        functools.partial(_gdn_kernel, ck=ck, nseq=nseq, stack_rows=stack_rows, own_layer=own_layer),
        grid=(nb // nseq, nt // ck),
        input_output_aliases=aliases,
        in_specs=[
            pl.BlockSpec((nseq, ck, GDN_QK), lambda i, c: (i, c, qblk)),
            pl.BlockSpec((nseq, ck, GDN_QK), lambda i, c: (i, c, qblk + 1)),
            pl.BlockSpec((nseq, ck, GDN_V), lambda i, c: (i, c, qblk + 2)),
            pl.BlockSpec((nseq, ck, GDN_V), lambda i, c: (i, c, COL_Z // GDN_V)),
            pl.BlockSpec((nseq, ck, N_AB), lambda i, c: (i, c, 0)),
            _lspec(layer, (nseq, SUBLANE, QKV_WIDTH), lambda i, c: (i, 0, 0)),
            state_spec,
            _lspec(layer, (CONV_W, QKV_WIDTH), _zeros_map(2)),
            head_row,
            head_row,
            head_row,
        ] + extra_specs,
        out_specs=[
            pl.BlockSpec((nseq, ck, GDN_V), lambda i, c: (i, c, 0)),
            out_state_spec,
        ],
        out_shape=[
            jax.ShapeDtypeStruct((nb, nt, GDN_V), odtype),
            jax.ShapeDtypeStruct(s_all.shape, F32),
        ],
        scratch_shapes=[
            pltpu.VMEM((nseq, QKV_WIDTH // LANE, ck + SUBLANE, LANE), F32),
            pltpu.VMEM((nseq, GDN_HEADS, GDN_DK, GDN_DV), F32),
            pltpu.VMEM((nseq, 2 * ck, N_AB), F32),
        ],
        compiler_params=_cparams(("parallel", "arbitrary")),
        name="gdn",
    )(*operands)


def _merge_kernel(x_ref, yrg_ref, ys5_ref, ygd_ref, gpre_ref, wgate_ref, wrg_ref, ws5_ref, wgd_ref,
                  wo_ref, gpost_ref, o_ref):
    x = x_ref[...]
    h = _rms(x, gpre_ref[...]).astype(BF16)
    mixed = None
    for n, (y_ref, w_ref) in enumerate(((yrg_ref, wrg_ref), (ys5_ref, ws5_ref), (ygd_ref, wgd_ref))):
        gate = jax.nn.sigmoid(_dot(h, wgate_ref[:, n * D_MODEL:(n + 1) * D_MODEL]))
        term = gate * _dot(y_ref[...].astype(BF16), w_ref[...])
        mixed = term if mixed is None else mixed + term
    out = _dot(mixed.astype(BF16), wo_ref[...])
    o_ref[...] = x + _rms(out, gpost_ref[...])


def _merge(x, yrg, ys5, ygd, w, layer):
    m = x.shape[0]
    tm = min(MERGE_TM, m)
    half = D_MODEL // 2
    branch_out = _lspec(layer, (half, D_MODEL), _zeros_map(2))
    row = _lspec(layer, (1, D_MODEL), _zeros_map(2))
    return pl.pallas_call(
        _merge_kernel,
        grid=(m // tm,),
        in_specs=[
            pl.BlockSpec((tm, D_MODEL), lambda i: (i, 0)),
            pl.BlockSpec((tm, half), lambda i: (i, 0)),
            pl.BlockSpec((tm, half), lambda i: (i, 0)),
            pl.BlockSpec((tm, half), lambda i: (i, 0)),
            row,
            _lspec(layer, (D_MODEL, N_BRANCH * D_MODEL), _zeros_map(2)),
            branch_out, branch_out, branch_out,
            _lspec(layer, (D_MODEL, D_MODEL), _zeros_map(2)),
            row,
        ],
        out_specs=pl.BlockSpec((tm, D_MODEL), lambda i: (i, 0)),
        out_shape=jax.ShapeDtypeStruct((m, D_MODEL), F32),
        compiler_params=_cparams(("parallel",)),
        name="merge",
    )(x, yrg, ys5, ygd, w["g_pre_mix"], w["w_gates"], w["rg_wo"], w["s5_wo"], w["gd_wo"], w["w_out"],
      w["g_post_mix"])


def _ffn_kernel(x_ref, gpre_ref, wg_ref, wu_ref, wd_ref, gpost_ref, o_ref, *, tf):
    x = x_ref[...]
    h = _rms(x, gpre_ref[...]).astype(BF16)
    acc = None
    for f0 in range(0, D_FF, tf):
        gate = _dot(h, wg_ref[:, f0:f0 + tf])
        act = (gate * jax.nn.sigmoid(gate)) * _dot(h, wu_ref[:, f0:f0 + tf])
        part = _dot(act.astype(BF16), wd_ref[f0:f0 + tf, :])
        acc = part if acc is None else acc + part
    o_ref[...] = x + _rms(acc, gpost_ref[...])


def _ffn(x, w, layer):
    m = x.shape[0]
    tm = min(FFN_TM, m)
    row = _lspec(layer, (1, D_MODEL), _zeros_map(2))
    once = pl.Buffered(1)
    wspec = lambda shape: pl.BlockSpec((None,) + shape, lambda i: (layer, 0, 0), pipeline_mode=once)
    return pl.pallas_call(
        functools.partial(_ffn_kernel, tf=FFN_TF),
        grid=(m // tm,),
        in_specs=[
            pl.BlockSpec((tm, D_MODEL), lambda i: (i, 0)),
            row,
            wspec((D_MODEL, D_FF)),
            wspec((D_MODEL, D_FF)),
            wspec((D_FF, D_MODEL)),
            row,
        ],
        out_specs=pl.BlockSpec((tm, D_MODEL), lambda i: (i, 0)),
        out_shape=jax.ShapeDtypeStruct((m, D_MODEL), F32),
        compiler_params=_cparams(("parallel",)),
        name="ffn",
    )(x, w["g_pre_ffn"], w["ffn_wg"], w["ffn_wu"], w["ffn_wd"], w["g_post_ffn"])


def _block_diag(blocks):
    n, r, c = blocks.shape[-3:]
    eye = jnp.eye(n, dtype=blocks.dtype)
    out = jnp.einsum("...nrc,nm->...nrmc", blocks, eye)
    return out.reshape(blocks.shape[:-3] + (n * r, n * c))


def _prep_params(p):
    f32 = F32
    depth = p["w_in"].shape[0]
    w_in = p["w_in"]
    sizes = (RG_WIDTH, RG_WIDTH, S5_WIDTH, QKV_WIDTH, GDN_V, GDN_HEADS, GDN_HEADS, 3 * D_MODEL)
    offs = [0]
    for s in sizes:
        offs.append(offs[-1] + s)
    seg = lambda i: w_in[:, :, offs[i]:offs[i + 1]]
    w_main = w_in[:, :, 0:N_MAIN].astype(BF16)
    w_gates = seg(7).astype(BF16)
    w_ab = jnp.concatenate(
        [seg(5), seg(6), jnp.zeros((depth, D_MODEL, N_AB - 2 * GDN_HEADS), f32)], axis=2).astype(BF16)

    pair = LANE // RG_BLOCK
    pairs = lambda a: _block_diag(a.reshape(depth, RG_WIDTH // LANE, pair, RG_BLOCK, RG_BLOCK))
    wg = jnp.concatenate([pairs(p["rg_wa"]), pairs(p["rg_wx"])], axis=-1).astype(BF16)

    a_re = p["s5_a_re"].astype(f32)
    a_im = p["s5_a_im"].astype(f32)
    dt = jnp.exp(p["s5_log_dt"].astype(f32))[..., None]
    mag = jnp.exp(a_re * dt)
    lr = mag * jnp.cos(a_im * dt)
    li = mag * jnp.sin(a_im * dt)
    den = a_re * a_re + a_im * a_im
    cr = (((lr - 1.0) * a_re + li * a_im) / den)[..., None]
    ci = ((li * a_re - (lr - 1.0) * a_im) / den)[..., None]
    b_re = p["s5_b_re"].astype(f32)
    b_im = p["s5_b_im"].astype(f32)
    bb_re = cr * b_re - ci * b_im
    bb_im = cr * b_im + ci * b_re
    nq = S5_WIDTH // LANE
    gq = S5_GROUPS // nq
    to_in = lambda a: _block_diag(jnp.swapaxes(a, 2, 3).reshape(depth, nq, gq, S5_GROUP, S5_STATE))
    to_out = lambda a: _block_diag(jnp.swapaxes(a.astype(f32), 2, 3).reshape(depth, nq, gq, S5_STATE, S5_GROUP))

    row = lambda a: a.astype(f32).reshape(depth, 1, -1)
    pad4 = lambda a: jnp.pad(a.astype(f32), ((0, 0), (0, LANE - GDN_HEADS))).reshape(depth, 1, LANE)
    return dict(
        g_pre_mix=row(p["norm_pre_mix"]), g_post_mix=row(p["norm_post_mix"]),
        g_pre_ffn=row(p["norm_pre_ffn"]), g_post_ffn=row(p["norm_post_ffn"]),
        w_main=w_main, w_ab=w_ab, w_gates=w_gates,
        rg_cw=p["rg_conv_w"].astype(f32), rg_cb=row(p["rg_conv_b"]), rg_wg=wg,
        rg_ba=row(p["rg_ba"]), rg_bx=row(p["rg_bx"]), rg_lam=row(p["rg_lambda"]),
        rg_wo=p["rg_w_out"].astype(BF16),
        s5_lr=lr.reshape(depth, 1, S5_FLAT), s5_li=li.reshape(depth, 1, S5_FLAT),
        s5_bre=to_in(bb_re).astype(BF16), s5_bim=to_in(bb_im).astype(BF16),
        s5_cre=to_out(p["s5_c_re"]).astype(BF16), s5_cim=to_out(p["s5_c_im"]).astype(BF16),
        s5_d=row(p["s5_d"]), s5_gw=p["s5_glu_w"].astype(BF16), s5_gb=row(p["s5_glu_b"]),
        s5_wo=p["s5_w_out"].astype(BF16),
        gd_cw=p["gdn_conv_w"].astype(f32), gd_alog=pad4(p["gdn_a_log"]), gd_dtb=pad4(p["gdn_dt_bias"]),
        gd_nw=row(p["gdn_norm_w"]), gd_wo=p["gdn_w_out"].astype(BF16),
        w_out=p["w_out"].astype(BF16),
        ffn_wg=p["ffn_w_gate"].astype(BF16), ffn_wu=p["ffn_w_up"].astype(BF16),
        ffn_wd=p["ffn_w_down"].astype(BF16),
    )


def _pad_conv_state(s):
    return jnp.pad(s, ((0, 0), (0, 0), (SUBLANE - (CONV_W - 1), 0), (0, 0)))


def _run_group(x, states, w, *, tc, ck, gdn_nseq, gdn_stack_rows, tiles_per_pass):
    nb, nt, _ = x.shape
    keep = CONV_W - 1
    assert nt >= keep and nt % SUBLANE == 0 and nb % SUBLANE == 0
    st_rg_conv, st_rg_h, st_s5_re, st_s5_im, st_gd_conv, st_gd_s = states
    depth = st_rg_h.shape[0]
    rg_conv0 = _pad_conv_state(st_rg_conv)
    gd_conv0 = _pad_conv_state(st_gd_conv)
    s5_re0 = st_s5_re.reshape(depth, nb, S5_FLAT)
    s5_im0 = st_s5_im.reshape(depth, nb, S5_FLAT)
    ydtype = BF16 if (tc % BF16_ROWS == 0 and ck % BF16_ROWS == 0) else F32
    xf = x.reshape(nb * nt, D_MODEL)
    flat = lambda a: a.reshape(nb * nt, a.shape[-1])
    outs = [[] for _ in range(5)]
    s_gd = None
    for l in range(depth):
        proj, ab = _in_proj(xf, w, l)
        proj3 = proj.reshape(nb, nt, N_MAIN)
        y_rg, h_last = _rglru(proj3, rg_conv0, st_rg_h, w, l, tc=tc, ydtype=ydtype)
        y_s5, s_re, s_im = _s5(proj3, s5_re0, s5_im0, w, l, tc=tc, tiles_per_pass=tiles_per_pass, ydtype=ydtype)
        y_gd, s_gd = _gdn(proj3, ab.reshape(nb, nt, N_AB), gd_conv0, st_gd_s, s_gd, w, l, ck=ck, nseq=gdn_nseq,
                          stack_rows=gdn_stack_rows, odtype=ydtype)
        x1 = _merge(xf, flat(y_rg), flat(y_s5), flat(y_gd), w, l)
        xf = _ffn(x1, w, l)

        outs[0].append(proj3[:, nt - keep:, COL_RG:COL_RG + RG_WIDTH])
        outs[1].append(h_last)
        outs[2].append(s_re.reshape(nb, S5_GROUPS, S5_STATE))
        outs[3].append(s_im.reshape(nb, S5_GROUPS, S5_STATE))
        outs[4].append(proj3[:, nt - keep:, COL_QKV:COL_QKV + QKV_WIDTH])
    return xf.reshape(nb, nt, D_MODEL), [jnp.stack(o) for o in outs] + [s_gd]


def _zero_states(depth, nb):
    return (jnp.zeros((depth, nb, CONV_W - 1, RG_WIDTH), F32),
            jnp.zeros((depth, nb, RG_WIDTH), F32),
            jnp.zeros((depth, nb, S5_GROUPS, S5_STATE), F32),
            jnp.zeros((depth, nb, S5_GROUPS, S5_STATE), F32),
            jnp.zeros((depth, nb, CONV_W - 1, QKV_WIDTH), F32),
            jnp.zeros((depth, nb, GDN_HEADS, GDN_DK, GDN_DV), F32))


def kernel(x_prompt, x_sample, state_rg_conv, state_rg_h, state_s5_re, state_s5_im, state_gdn_conv, state_gdn_s, norm_pre_mix, norm_post_mix, norm_pre_ffn, norm_post_ffn, w_in, rg_conv_w, rg_conv_b, rg_wa, rg_ba, rg_wx, rg_bx, rg_lambda, rg_w_out, s5_a_re, s5_a_im, s5_b_re, s5_b_im, s5_c_re, s5_c_im, s5_d, s5_log_dt, s5_glu_w, s5_glu_b, s5_w_out, gdn_conv_w, gdn_a_log, gdn_dt_bias, gdn_norm_w, gdn_w_out, w_out, ffn_w_gate, ffn_w_up, ffn_w_down):
    p = dict(norm_pre_mix=norm_pre_mix, norm_post_mix=norm_post_mix, norm_pre_ffn=norm_pre_ffn,
             norm_post_ffn=norm_post_ffn, w_in=w_in, rg_conv_w=rg_conv_w, rg_conv_b=rg_conv_b, rg_wa=rg_wa,
             rg_ba=rg_ba, rg_wx=rg_wx, rg_bx=rg_bx, rg_lambda=rg_lambda, rg_w_out=rg_w_out, s5_a_re=s5_a_re,
             s5_a_im=s5_a_im, s5_b_re=s5_b_re, s5_b_im=s5_b_im, s5_c_re=s5_c_re, s5_c_im=s5_c_im, s5_d=s5_d,
             s5_log_dt=s5_log_dt, s5_glu_w=s5_glu_w, s5_glu_b=s5_glu_b, s5_w_out=s5_w_out,
             gdn_conv_w=gdn_conv_w, gdn_a_log=gdn_a_log, gdn_dt_bias=gdn_dt_bias, gdn_norm_w=gdn_norm_w,
             gdn_w_out=gdn_w_out, w_out=w_out, ffn_w_gate=ffn_w_gate, ffn_w_up=ffn_w_up, ffn_w_down=ffn_w_down)
    depth = w_in.shape[0]
    w = _prep_params(p)
    y_p, st_p = _run_group(x_prompt, _zero_states(depth, x_prompt.shape[0]), w,
                           tc=128, ck=64, gdn_nseq=4, gdn_stack_rows=MXU_EDGE, tiles_per_pass=8)
    y_s, st_s = _run_group(
        x_sample, (state_rg_conv, state_rg_h, state_s5_re, state_s5_im, state_gdn_conv, state_gdn_s), w,
        tc=x_sample.shape[1], ck=x_sample.shape[1], gdn_nseq=16, gdn_stack_rows=MXU_EDGE, tiles_per_pass=1)
    return (y_p, y_s, *st_p, *st_s)
```

```python
import functools

import jax
import jax.numpy as jnp
from jax import lax
from jax.experimental import pallas as pl
from jax.experimental.pallas import tpu as pltpu

F32 = jnp.float32
BF16 = jnp.bfloat16

D_MODEL = 1024
CONV_W = 4
EPS = 1e-6
RG_WIDTH = 512
RG_BLOCK = 64
RG_C = 8.0
S5_WIDTH = 512
S5_GROUP = 16
S5_GROUPS = 32
S5_STATE = 64
S5_FLAT = S5_GROUPS * S5_STATE
GDN_HEADS = 4
GDN_DK = 128
GDN_DV = 128
GDN_QK = 512
GDN_V = 512
QKV_WIDTH = 1536
D_FF = 2816

LANE = 128
SUBLANE = 8
BF16_ROWS = 16
MXU_EDGE = 256
MASKED_LOG = -1e30

COL_RG = 0
COL_S5 = 1024
COL_QKV = 1536
COL_Z = 3072
N_MAIN = 3584
N_BRANCH = 3
N_AB = LANE

VMEM_LIMIT = 52 * 1024 * 1024

IN_PROJ_TM = 1024
MERGE_TM = 512
FFN_TM, FFN_TF = 512, 256


def _cparams(sem):
    return pltpu.CompilerParams(dimension_semantics=sem, vmem_limit_bytes=VMEM_LIMIT)


def _lspec(layer, block, index_map):
    return pl.BlockSpec((None,) + tuple(block), lambda *g: (layer,) + tuple(index_map(*g)))


def _zeros_map(n):
    return lambda *g: (0,) * n


def _rms(x, g):
    return x * lax.rsqrt(jnp.mean(x * x, axis=-1, keepdims=True) + EPS) * g


def _dot(a, b):
    return jnp.dot(a, b, preferred_element_type=F32)


def _dot_nt(a, b):
    return lax.dot_general(a, b, (((1,), (1,)), ((), ())), preferred_element_type=F32)


def _dot_tn(a, b):
    return lax.dot_general(a, b, (((0,), (0,)), ((), ())), preferred_element_type=F32)


def _split2(a):
    hi = a.astype(BF16)
    return hi, (a - hi.astype(F32)).astype(BF16)


def _dot_split(a, b):
    a1, a2 = _split2(a)
    b1, b2 = _split2(b)
    return _dot(a1, b1) + (_dot(a1, b2) + _dot(a2, b1))


def _step_pitch(nt_chunk):
    return nt_chunk if nt_chunk <= SUBLANE else nt_chunk + SUBLANE // 2


def _put_rows(ref, tile, val, nb, tc, pitch):
    if pitch == tc:
        ref[tile, 0:nb * tc, :] = val
    else:
        for b in range(nb):
            ref[tile, b * pitch:b * pitch + tc, :] = val[b * tc:(b + 1) * tc, :]


def _get_rows(ref, tile, nb, tc, pitch):
    if pitch == tc:
        return ref[tile, 0:nb * tc, :]
    return jnp.concatenate([ref[tile, b * pitch:b * pitch + tc, :] for b in range(nb)], axis=0)


def _in_proj_kernel(x_ref, g_ref, w_ref, wab_ref, o_ref, oab_ref):
    h = _rms(x_ref[...], g_ref[...]).astype(BF16)
    oab_ref[...] = _dot(h, wab_ref[...])
    o_ref[...] = _dot(h, w_ref[...])


def _in_proj(x, w, layer):
    m = x.shape[0]
    tm = min(IN_PROJ_TM, m)
    once = pl.Buffered(1)
    return pl.pallas_call(
        _in_proj_kernel,
        grid=(m // tm,),
        in_specs=[
            pl.BlockSpec((tm, D_MODEL), lambda i: (i, 0)),
            _lspec(layer, (1, D_MODEL), _zeros_map(2)),
            pl.BlockSpec((None, D_MODEL, N_MAIN), lambda i: (layer, 0, 0), pipeline_mode=once),
            pl.BlockSpec((None, D_MODEL, N_AB), lambda i: (layer, 0, 0), pipeline_mode=once),
        ],
        out_specs=[
            pl.BlockSpec((tm, N_MAIN), lambda i: (i, 0)),
            pl.BlockSpec((tm, N_AB), lambda i: (i, 0)),
        ],
        out_shape=[jax.ShapeDtypeStruct((m, N_MAIN), F32), jax.ShapeDtypeStruct((m, N_AB), F32)],
        compiler_params=_cparams(("parallel",)),
        name="in_proj",
    )(x, w["g_pre_mix"], w["w_main"], w["w_ab"])


def _rglru_kernel(p_ref, conv0_ref, h0_ref, cw_ref, cb_ref, wg_ref, ba_ref, bx_ref, lam_ref,
                  y_ref, hl_ref, xp_ref, a_ref, h_ref, hc_ref, *, nb, tc, pitch):
    ntile = RG_WIDTH // LANE
    first = SUBLANE - (CONV_W - 1)

    @pl.when(pl.program_id(0) == 0)
    def _():
        for p in range(ntile):
            xp_ref[:, p, 0:SUBLANE, :] = conv0_ref[:, :, p * LANE:(p + 1) * LANE]
        hc_ref[...] = h0_ref[...]

    cw = cw_ref[...]
    for p in range(ntile):
        cols = slice(p * LANE, (p + 1) * LANE)
        xp_ref[:, p, SUBLANE:SUBLANE + tc, :] = p_ref[:, :, cols]
        xc = cb_ref[:, cols] + xp_ref[:, p, first:first + tc, :] * cw[0:1, cols]
        for k in range(1, CONV_W):
            xc = xc + xp_ref[:, p, first + k:first + k + tc, :] * cw[k:k + 1, cols]
        xp_ref[:, p, 0:SUBLANE, :] = xp_ref[:, p, tc:tc + SUBLANE, :]
        xc = xc.reshape(nb * tc, LANE)
        gm = _dot(xc.astype(BF16), wg_ref[p])
        r = jax.nn.sigmoid(gm[:, 0:LANE] + ba_ref[:, cols])
        i = jax.nn.sigmoid(gm[:, LANE:2 * LANE] + bx_ref[:, cols])
        log_a = -RG_C * r * jax.nn.softplus(-lam_ref[:, cols])
        a = jnp.exp(log_a)
        _put_rows(a_ref, p, a, nb, tc, pitch)
        _put_rows(h_ref, p, jnp.sqrt(1.0 - a * a) * (i * xc), nb, tc, pitch)

    def step(t, hs):
        rs = pl.ds(t, nb, stride=pitch)
        out = []
        for p in range(ntile):
            h = a_ref[p, rs, :] * hs[p] + h_ref[p, rs, :]
            h_ref[p, rs, :] = h
            out.append(h)
        return tuple(out)

    hc = hc_ref[...]
    hs = lax.fori_loop(0, tc, step, tuple(hc[:, p * LANE:(p + 1) * LANE] for p in range(ntile)), unroll=2)
    for p in range(ntile):
        cols = slice(p * LANE, (p + 1) * LANE)
        hc_ref[:, cols] = hs[p]
        hl_ref[:, cols] = hs[p]
        h = _get_rows(h_ref, p, nb, tc, pitch).reshape(nb, tc, LANE)
        gate = p_ref[:, :, RG_WIDTH + p * LANE:RG_WIDTH + (p + 1) * LANE]
        y_ref[:, :, cols] = (h * jax.nn.gelu(gate)).astype(y_ref.dtype)


def _rglru(proj3, conv0, h0, w, layer, *, tc, ydtype):
    nb, nt, _ = proj3.shape
    pitch = _step_pitch(tc)
    ntile = RG_WIDTH // LANE
    row = _lspec(layer, (1, RG_WIDTH), _zeros_map(2))
    return pl.pallas_call(
        functools.partial(_rglru_kernel, nb=nb, tc=tc, pitch=pitch),
        grid=(nt // tc,),
        in_specs=[
            pl.BlockSpec((nb, tc, 2 * RG_WIDTH), lambda c: (0, c, COL_RG // (2 * RG_WIDTH))),
            _lspec(layer, (nb, SUBLANE, RG_WIDTH), _zeros_map(3)),
            _lspec(layer, (nb, RG_WIDTH), _zeros_map(2)),
            _lspec(layer, (CONV_W, RG_WIDTH), _zeros_map(2)),
            row,
            _lspec(layer, (ntile, LANE, 2 * LANE), _zeros_map(3)),
            row,
            row,
            row,
        ],
        out_specs=[
            pl.BlockSpec((nb, tc, RG_WIDTH), lambda c: (0, c, 0)),
            pl.BlockSpec((nb, RG_WIDTH), _zeros_map(2)),
        ],
        out_shape=[jax.ShapeDtypeStruct((nb, nt, RG_WIDTH), ydtype), jax.ShapeDtypeStruct((nb, RG_WIDTH), F32)],
        scratch_shapes=[
            pltpu.VMEM((nb, ntile, tc + SUBLANE, LANE), F32),
            pltpu.VMEM((ntile, nb * pitch, LANE), F32),
            pltpu.VMEM((ntile, nb * pitch, LANE), F32),
            pltpu.VMEM((nb, RG_WIDTH), F32),
        ],
        compiler_params=_cparams(("arbitrary",)),
        name="rglru",
    )(proj3, conv0, h0, w["rg_cw"], w["rg_cb"], w["rg_wg"], w["rg_ba"], w["rg_bx"], w["rg_lam"])


def _s5_kernel(u_ref, s0r_ref, s0i_ref, lr_ref, li_ref, bre_ref, bim_ref, cre_ref, cim_ref, d_ref, gw_ref, gb_ref,
               y_ref, sr_out, si_out, sre_ref, sim_ref, src_ref, sic_ref, *, nb, tc, pitch, tiles_per_pass):
    nq = S5_WIDTH // LANE
    qt = S5_FLAT // nq // LANE
    ntile = S5_FLAT // LANE

    @pl.when(pl.program_id(0) == 0)
    def _():
        src_ref[...] = s0r_ref[...]
        sic_ref[...] = s0i_ref[...]

    u = u_ref[...].reshape(nb * tc, S5_WIDTH)
    ub = u.astype(BF16)
    for q in range(nq):
        uq = ub[:, q * LANE:(q + 1) * LANE]
        bur = _dot(uq, bre_ref[q])
        bui = _dot(uq, bim_ref[q])
        for j in range(qt):
            _put_rows(sre_ref, q * qt + j, bur[:, j * LANE:(j + 1) * LANE], nb, tc, pitch)
            _put_rows(sim_ref, q * qt + j, bui[:, j * LANE:(j + 1) * LANE], nb, tc, pitch)

    for t0 in range(0, ntile, tiles_per_pass):
        tiles = list(range(t0, t0 + tiles_per_pass))
        lrs = [jnp.broadcast_to(lr_ref[:, tl * LANE:(tl + 1) * LANE], (nb, LANE)) for tl in tiles]
        lis = [jnp.broadcast_to(li_ref[:, tl * LANE:(tl + 1) * LANE], (nb, LANE)) for tl in tiles]

        def step(t, carry, tiles=tiles, lrs=lrs, lis=lis):
            rs = pl.ds(t, nb, stride=pitch)
            out = []
            for n, tl in enumerate(tiles):
                sr, si = carry[2 * n], carry[2 * n + 1]
                nr = lrs[n] * sr - lis[n] * si + sre_ref[tl, rs, :]
                ni = lrs[n] * si + lis[n] * sr + sim_ref[tl, rs, :]
                sre_ref[tl, rs, :] = nr
                sim_ref[tl, rs, :] = ni
                out += [nr, ni]
            return tuple(out)

        init = []
        for tl in tiles:
            init += [src_ref[:, tl * LANE:(tl + 1) * LANE], sic_ref[:, tl * LANE:(tl + 1) * LANE]]
        fin = lax.fori_loop(0, tc, step, tuple(init), unroll=2)
        for n, tl in enumerate(tiles):
            src_ref[:, tl * LANE:(tl + 1) * LANE] = fin[2 * n]
            sic_ref[:, tl * LANE:(tl + 1) * LANE] = fin[2 * n + 1]

    sr_out[...] = src_ref[...]
    si_out[...] = sic_ref[...]

    ys = []
    for q in range(nq):
        sr_q = jnp.concatenate([_get_rows(sre_ref, q * qt + j, nb, tc, pitch) for j in range(qt)], axis=1)
        si_q = jnp.concatenate([_get_rows(sim_ref, q * qt + j, nb, tc, pitch) for j in range(qt)], axis=1)
        ys.append(_dot(sr_q.astype(BF16), cre_ref[q]) - _dot(si_q.astype(BF16), cim_ref[q]))
    y = jnp.concatenate(ys, axis=1) + d_ref[...] * u
    y = jax.nn.gelu(y)
    y = y * jax.nn.sigmoid(_dot(y.astype(BF16), gw_ref[...]) + gb_ref[...])
    y_ref[...] = y.reshape(nb, tc, S5_WIDTH).astype(y_ref.dtype)


def _s5(proj3, s0r, s0i, w, layer, *, tc, tiles_per_pass, ydtype):
    nb, nt, _ = proj3.shape
    pitch = _step_pitch(tc)
    nq = S5_WIDTH // LANE
    qs = S5_FLAT // nq
    ntile = S5_FLAT // LANE
    state = _lspec(layer, (nb, S5_FLAT), _zeros_map(2))
    lam = _lspec(layer, (1, S5_FLAT), _zeros_map(2))
    b_in = _lspec(layer, (nq, LANE, qs), _zeros_map(3))
    c_out = _lspec(layer, (nq, qs, LANE), _zeros_map(3))
    row = _lspec(layer, (1, S5_WIDTH), _zeros_map(2))
    return pl.pallas_call(
        functools.partial(_s5_kernel, nb=nb, tc=tc, pitch=pitch, tiles_per_pass=tiles_per_pass),
        grid=(nt // tc,),
        in_specs=[
            pl.BlockSpec((nb, tc, S5_WIDTH), lambda c: (0, c, COL_S5 // S5_WIDTH)),
            state, state, lam, lam, b_in, b_in, c_out, c_out, row,
            _lspec(layer, (S5_WIDTH, S5_WIDTH), _zeros_map(2)),
            row,
        ],
        out_specs=[
            pl.BlockSpec((nb, tc, S5_WIDTH), lambda c: (0, c, 0)),
            pl.BlockSpec((nb, S5_FLAT), _zeros_map(2)),
            pl.BlockSpec((nb, S5_FLAT), _zeros_map(2)),
        ],
        out_shape=[
            jax.ShapeDtypeStruct((nb, nt, S5_WIDTH), ydtype),
            jax.ShapeDtypeStruct((nb, S5_FLAT), F32),
            jax.ShapeDtypeStruct((nb, S5_FLAT), F32),
        ],
        scratch_shapes=[
            pltpu.VMEM((ntile, nb * pitch, LANE), F32),
            pltpu.VMEM((ntile, nb * pitch, LANE), F32),
            pltpu.VMEM((nb, S5_FLAT), F32),
            pltpu.VMEM((nb, S5_FLAT), F32),
        ],
        compiler_params=_cparams(("arbitrary",)),
        name="s5",
    )(proj3, s0r, s0i, w["s5_lr"], w["s5_li"], w["s5_bre"], w["s5_bim"], w["s5_cre"], w["s5_cim"], w["s5_d"],
      w["s5_gw"], w["s5_gb"])


def _gdn_masks(sr, ck):
    shift = ck.bit_length() - 1
    row = lax.broadcasted_iota(jnp.int32, (sr, sr), 0)
    col = lax.broadcasted_iota(jnp.int32, (sr, sr), 1)
    same = (row >> shift) == (col >> shift)
    causal = same & (row >= col)
    strict = same & (row > col)
    eye = jnp.where(row == col, 1.0, 0.0).astype(F32)
    return causal, strict, eye


def _gdn_stack(blocks, ck, masks, xc, gc_all, beta_all, z, nw, s_ref, o_ref):
    nblk = len(blocks)
    sr = nblk * ck
    causal, strict, eye = masks

    def stack(fn):
        return jnp.concatenate([fn(g, h) for g, h in blocks], axis=0)

    q = stack(lambda g, h: xc[h][g])
    k = stack(lambda g, h: xc[GDN_HEADS + h][g])
    v = stack(lambda g, h: xc[2 * GDN_HEADS + h][g])
    zs = stack(lambda g, h: z[g, :, h * GDN_DV:(h + 1) * GDN_DV])
    gc = stack(lambda g, h: jnp.broadcast_to(gc_all[g, :, h:h + 1], (ck, LANE)))
    beta = stack(lambda g, h: jnp.broadcast_to(beta_all[g, :, GDN_HEADS + h:GDN_HEADS + h + 1], (ck, LANE)))

    q = q * lax.rsqrt(jnp.sum(q * q, axis=-1, keepdims=True) + EPS) * (GDN_DK ** -0.5)
    k = k * lax.rsqrt(jnp.sum(k * k, axis=-1, keepdims=True) + EPS)
    kb = k * beta
    vb = v * beta
    yield

    kbf = k.astype(BF16)
    kk = _dot_nt(kb.astype(BF16), kbf)
    qk = _dot_nt(q.astype(BF16), kbf)
    yield
    gc_lanes = jnp.transpose(gc)[0:SUBLANE, :]
    gc_rows = jnp.concatenate([gc] * (sr // LANE), axis=1) if sr > LANE else gc[:, 0:sr]
    seg = gc_rows - jnp.concatenate([gc_lanes] * (sr // SUBLANE), axis=0)
    decay = jnp.exp(jnp.where(causal, seg, MASKED_LOG))
    lmat = jnp.where(strict, kk * decay, 0.0)
    qk = (qk * decay).astype(BF16)

    x = -lmat
    p = _dot_split(lmat, lmat)
    yield
    x = x + p + _dot(x.astype(BF16), p.astype(BF16))
    n = 4
    while n < ck:
        pb = p.astype(BF16)
        p = _dot(pb, pb)
        yield
        x = x + p + _dot(x.astype(BF16), p.astype(BF16))
        yield
        n *= 2
    tmat = (eye + x).astype(BF16)
    egc = jnp.exp(gc)
    uw = _dot(tmat, jnp.concatenate([vb, kb * egc], axis=1).astype(BF16))
    yield
    u = uw[:, 0:GDN_DV]
    w = uw[:, GDN_DV:GDN_DV + GDN_DK]
    qg = q * egc
    g_last = jnp.concatenate(
        [jnp.broadcast_to(gc[(n + 1) * ck - 1:(n + 1) * ck, :], (ck, LANE)) for n in range(nblk)], axis=0)
    k_dec = (k * jnp.exp(g_last - gc)).astype(BF16)

    v_new, qs = [], []
    for n, (g, h) in enumerate(blocks):
        rs = slice(n * ck, (n + 1) * ck)
        sb = s_ref[g, h].astype(BF16)
        r = _dot(jnp.concatenate([w[rs], qg[rs]], axis=0).astype(BF16), sb)
        v_new.append(u[rs] - r[0:ck])
        qs.append(r[ck:2 * ck])
    yield
    v_new = jnp.concatenate(v_new, axis=0)
    vnb = v_new.astype(BF16)
    o = jnp.concatenate(qs, axis=0) + _dot(qk, vnb)
    for n, (g, h) in enumerate(blocks):
        rs = slice(n * ck, (n + 1) * ck)
        s_ref[g, h] = s_ref[g, h] * jnp.exp(gc[(n + 1) * ck - 1:(n + 1) * ck, :]) + _dot_tn(k_dec[rs], vnb[rs])
    yield

    o = _rms(o, nw) * (zs * jax.nn.sigmoid(zs))
    for n, (g, h) in enumerate(blocks):
        o_ref[g, :, h * GDN_DV:(h + 1) * GDN_DV] = o[n * ck:(n + 1) * ck].astype(o_ref.dtype)


def _gdn_kernel(q_ref, k_ref, v_ref, z_ref, ab_ref, conv0_ref, s0_ref, cw_ref, alog_ref, dtb_ref, nw_ref,
                *rest, ck, nseq, stack_rows, own_layer):
    o_ref, sout_ref, xb_ref, s_ref, gsum_ref = rest[-5:]
    c = pl.program_id(1)
    ntile = QKV_WIDTH // LANE
    per_src = GDN_QK // LANE

    @pl.when(c == 0)
    def _():
        for lt in range(ntile):
            xb_ref[:, lt, 0:SUBLANE, :] = conv0_ref[:, :, lt * LANE:(lt + 1) * LANE]
        s_ref[...] = s0_ref[...]

    cw = cw_ref[...]
    first = SUBLANE - (CONV_W - 1)
    xc = []
    for lt in range(ntile):
        src = (q_ref, k_ref, v_ref)[lt // per_src]
        xb_ref[:, lt, SUBLANE:SUBLANE + ck, :] = src[:, :, (lt % per_src) * LANE:(lt % per_src + 1) * LANE]
        acc = xb_ref[:, lt, first:first + ck, :] * cw[0:1, lt * LANE:(lt + 1) * LANE]
        for j in range(1, CONV_W):
            acc = acc + xb_ref[:, lt, first + j:first + j + ck, :] * cw[j:j + 1, lt * LANE:(lt + 1) * LANE]
        xb_ref[:, lt, 0:SUBLANE, :] = xb_ref[:, lt, ck:ck + SUBLANE, :]
        xc.append(acc * jax.nn.sigmoid(acc))

    ab = ab_ref[...]
    g_all = -jnp.exp(alog_ref[...]) * jax.nn.softplus(ab + dtb_ref[...])
    gsum_ref[:, 0:ck, :] = jnp.zeros_like(g_all)
    gc_all = g_all
    d = 1
    while d < ck:
        gsum_ref[:, ck:2 * ck, :] = gc_all
        gc_all = gc_all + gsum_ref[:, ck - d:2 * ck - d, :]
        d *= 2
    beta_all = jax.nn.sigmoid(ab)
    z = z_ref[...]
    nw = nw_ref[...]
    masks = _gdn_masks(stack_rows, ck)
    blocks = [(g, h) for g in range(nseq) for h in range(GDN_HEADS)]
    per = stack_rows // ck
    stacks = [_gdn_stack(blocks[b0:b0 + per], ck, masks, xc, gc_all, beta_all, z, nw, s_ref, o_ref)
              for b0 in range(0, len(blocks), per)]
    while stacks:
        stacks = [s for s in stacks if next(s, True) is None]

    @pl.when(c == pl.num_programs(1) - 1)
    def _():
        if own_layer is None:
            sout_ref[...] = s_ref[...]
        else:
            sout_ref[...] = jnp.zeros_like(sout_ref)
            sout_ref[own_layer] = s_ref[...]


def _gdn(proj3, ab3, conv0, s_all, s_done, w, layer, *, ck, nseq, stack_rows, odtype):
    nb, nt, _ = proj3.shape
    assert (nseq * GDN_HEADS * ck) % stack_rows == 0 and stack_rows % ck == 0
    qblk = COL_QKV // GDN_QK
    state_block = (nseq, GDN_HEADS, GDN_DK, GDN_DV)
    state_spec = _lspec(layer, state_block, lambda i, c: (i, 0, 0, 0))
    head_row = _lspec(layer, (1, LANE), _zeros_map(2))
    operands = [proj3, proj3, proj3, proj3, ab3, conv0, s_all, w["gd_cw"], w["gd_alog"], w["gd_dtb"], w["gd_nw"]]
    extra_specs, aliases = [], {}
    if s_done is not None:
        aliases = {len(operands): 1}
        operands.append(s_done)
        extra_specs = [pl.BlockSpec(memory_space=pl.ANY)]
        out_state_spec, own_layer = state_spec, None
    else:
        out_state_spec = pl.BlockSpec((s_all.shape[0],) + state_block, lambda i, c: (0, i, 0, 0, 0))
        own_layer = layer
    return pl.pallas_call(
        functools.partial(_gdn_kernel, ck=ck, nseq=nseq, stack_rows=stack_rows, own_layer=own_layer),
        grid=(nb // nseq, nt // ck),
        input_output_aliases=aliases,
        in_specs=[
            pl.BlockSpec((nseq, ck, GDN_QK), lambda i, c: (i, c, qblk)),
            pl.BlockSpec((nseq, ck, GDN_QK), lambda i, c: (i, c, qblk + 1)),
            pl.BlockSpec((nseq, ck, GDN_V), lambda i, c: (i, c, qblk + 2)),
            pl.BlockSpec((nseq, ck, GDN_V), lambda i, c: (i, c, COL_Z // GDN_V)),
            pl.BlockSpec((nseq, ck, N_AB), lambda i, c: (i, c, 0)),
            _lspec(layer, (nseq, SUBLANE, QKV_WIDTH), lambda i, c: (i, 0, 0)),
            state_spec,
            _lspec(layer, (CONV_W, QKV_WIDTH), _zeros_map(2)),
            head_row,
            head_row,
            head_row,
        ] + extra_specs,
        out_specs=[
            pl.BlockSpec((nseq, ck, GDN_V), lambda i, c: (i, c, 0)),
            out_state_spec,
        ],
        out_shape=[
            jax.ShapeDtypeStruct((nb, nt, GDN_V), odtype),
            jax.ShapeDtypeStruct(s_all.shape, F32),
        ],
        scratch_shapes=[
            pltpu.VMEM((nseq, QKV_WIDTH // LANE, ck + SUBLANE, LANE), F32),
            pltpu.VMEM((nseq, GDN_HEADS, GDN_DK, GDN_DV), F32),
            pltpu.VMEM((nseq, 2 * ck, N_AB), F32),
        ],
        compiler_params=_cparams(("parallel", "arbitrary")),
        name="gdn",
    )(*operands)


def _merge_kernel(x_ref, yrg_ref, ys5_ref, ygd_ref, gpre_ref, wgate_ref, wrg_ref, ws5_ref, wgd_ref,
                  wo_ref, gpost_ref, o_ref):
    x = x_ref[...]
    h = _rms(x, gpre_ref[...]).astype(BF16)
    mixed = None
    for n, (y_ref, w_ref) in enumerate(((yrg_ref, wrg_ref), (ys5_ref, ws5_ref), (ygd_ref, wgd_ref))):
        gate = jax.nn.sigmoid(_dot(h, wgate_ref[:, n * D_MODEL:(n + 1) * D_MODEL]))
        term = gate * _dot(y_ref[...].astype(BF16), w_ref[...])
        mixed = term if mixed is None else mixed + term
    out = _dot(mixed.astype(BF16), wo_ref[...])
    o_ref[...] = x + _rms(out, gpost_ref[...])


def _merge(x, yrg, ys5, ygd, w, layer):
    m = x.shape[0]
    tm = min(MERGE_TM, m)
    half = D_MODEL // 2
    branch_out = _lspec(layer, (half, D_MODEL), _zeros_map(2))
    row = _lspec(layer, (1, D_MODEL), _zeros_map(2))
    return pl.pallas_call(
        _merge_kernel,
        grid=(m // tm,),
        in_specs=[
            pl.BlockSpec((tm, D_MODEL), lambda i: (i, 0)),
            pl.BlockSpec((tm, half), lambda i: (i, 0)),
            pl.BlockSpec((tm, half), lambda i: (i, 0)),
            pl.BlockSpec((tm, half), lambda i: (i, 0)),
            row,
            _lspec(layer, (D_MODEL, N_BRANCH * D_MODEL), _zeros_map(2)),
            branch_out, branch_out, branch_out,
            _lspec(layer, (D_MODEL, D_MODEL), _zeros_map(2)),
            row,
        ],
        out_specs=pl.BlockSpec((tm, D_MODEL), lambda i: (i, 0)),
        out_shape=jax.ShapeDtypeStruct((m, D_MODEL), F32),
        compiler_params=_cparams(("parallel",)),
        name="merge",
    )(x, yrg, ys5, ygd, w["g_pre_mix"], w["w_gates"], w["rg_wo"], w["s5_wo"], w["gd_wo"], w["w_out"],
      w["g_post_mix"])


def _ffn_kernel(x_ref, gpre_ref, wg_ref, wu_ref, wd_ref, gpost_ref, o_ref, *, tf):
    x = x_ref[...]
    h = _rms(x, gpre_ref[...]).astype(BF16)
    acc = None
    for f0 in range(0, D_FF, tf):
        gate = _dot(h, wg_ref[:, f0:f0 + tf])
        act = (gate * jax.nn.sigmoid(gate)) * _dot(h, wu_ref[:, f0:f0 + tf])
        part = _dot(act.astype(BF16), wd_ref[f0:f0 + tf, :])
        acc = part if acc is None else acc + part
    o_ref[...] = x + _rms(acc, gpost_ref[...])


def _ffn(x, w, layer):
    m = x.shape[0]
    tm = min(FFN_TM, m)
    row = _lspec(layer, (1, D_MODEL), _zeros_map(2))
    once = pl.Buffered(1)
    wspec = lambda shape: pl.BlockSpec((None,) + shape, lambda i: (layer, 0, 0), pipeline_mode=once)
    return pl.pallas_call(
        functools.partial(_ffn_kernel, tf=FFN_TF),
        grid=(m // tm,),
        in_specs=[
            pl.BlockSpec((tm, D_MODEL), lambda i: (i, 0)),
            row,
            wspec((D_MODEL, D_FF)),
            wspec((D_MODEL, D_FF)),
            wspec((D_FF, D_MODEL)),
            row,
        ],
        out_specs=pl.BlockSpec((tm, D_MODEL), lambda i: (i, 0)),
        out_shape=jax.ShapeDtypeStruct((m, D_MODEL), F32),
        compiler_params=_cparams(("parallel",)),
        name="ffn",
    )(x, w["g_pre_ffn"], w["ffn_wg"], w["ffn_wu"], w["ffn_wd"], w["g_post_ffn"])


def _block_diag(blocks):
    n, r, c = blocks.shape[-3:]
    eye = jnp.eye(n, dtype=blocks.dtype)
    out = jnp.einsum("...nrc,nm->...nrmc", blocks, eye)
    return out.reshape(blocks.shape[:-3] + (n * r, n * c))


def _prep_params(p):
    f32 = F32
    depth = p["w_in"].shape[0]
    w_in = p["w_in"]
    sizes = (RG_WIDTH, RG_WIDTH, S5_WIDTH, QKV_WIDTH, GDN_V, GDN_HEADS, GDN_HEADS, 3 * D_MODEL)
    offs = [0]
    for s in sizes:
        offs.append(offs[-1] + s)
    assert offs[5] == N_MAIN
    w_main = w_in.astype(BF16)
    w_gates = w_main[:, :, offs[7]:offs[8]]
    w_ab = jnp.concatenate(
        [w_main[:, :, offs[5]:offs[7]], jnp.zeros((depth, D_MODEL, N_AB - 2 * GDN_HEADS), BF16)], axis=2)

    pair = LANE // RG_BLOCK
    pairs = lambda a: _block_diag(a.reshape(depth, RG_WIDTH // LANE, pair, RG_BLOCK, RG_BLOCK))
    wg = jnp.concatenate([pairs(p["rg_wa"]), pairs(p["rg_wx"])], axis=-1).astype(BF16)

    a_re = p["s5_a_re"].astype(f32)
    a_im = p["s5_a_im"].astype(f32)
    dt = jnp.exp(p["s5_log_dt"].astype(f32))[..., None]
    mag = jnp.exp(a_re * dt)
    lr = mag * jnp.cos(a_im * dt)
    li = mag * jnp.sin(a_im * dt)
    den = a_re * a_re + a_im * a_im
    cr = (((lr - 1.0) * a_re + li * a_im) / den)[..., None]
    ci = ((li * a_re - (lr - 1.0) * a_im) / den)[..., None]
    b_re = p["s5_b_re"].astype(f32)
    b_im = p["s5_b_im"].astype(f32)
    bb_re = cr * b_re - ci * b_im
    bb_im = cr * b_im + ci * b_re
    nq = S5_WIDTH // LANE
    gq = S5_GROUPS // nq
    to_in = lambda a: _block_diag(jnp.swapaxes(a, 2, 3).reshape(depth, nq, gq, S5_GROUP, S5_STATE))
    to_out = lambda a: _block_diag(jnp.swapaxes(a.astype(f32), 2, 3).reshape(depth, nq, gq, S5_STATE, S5_GROUP))

    row = lambda a: a.astype(f32).reshape(depth, 1, -1)
    pad4 = lambda a: jnp.pad(a.astype(f32), ((0, 0), (0, LANE - GDN_HEADS))).reshape(depth, 1, LANE)
    return dict(
        g_pre_mix=row(p["norm_pre_mix"]), g_post_mix=row(p["norm_post_mix"]),
        g_pre_ffn=row(p["norm_pre_ffn"]), g_post_ffn=row(p["norm_post_ffn"]),
        w_main=w_main, w_ab=w_ab, w_gates=w_gates,
        rg_cw=p["rg_conv_w"].astype(f32), rg_cb=row(p["rg_conv_b"]), rg_wg=wg,
        rg_ba=row(p["rg_ba"]), rg_bx=row(p["rg_bx"]), rg_lam=row(p["rg_lambda"]),
        rg_wo=p["rg_w_out"].astype(BF16),
        s5_lr=lr.reshape(depth, 1, S5_FLAT), s5_li=li.reshape(depth, 1, S5_FLAT),
        s5_bre=to_in(bb_re).astype(BF16), s5_bim=to_in(bb_im).astype(BF16),
        s5_cre=to_out(p["s5_c_re"]).astype(BF16), s5_cim=to_out(p["s5_c_im"]).astype(BF16),
        s5_d=row(p["s5_d"]), s5_gw=p["s5_glu_w"].astype(BF16), s5_gb=row(p["s5_glu_b"]),
        s5_wo=p["s5_w_out"].astype(BF16),
        gd_cw=p["gdn_conv_w"].astype(f32), gd_alog=pad4(p["gdn_a_log"]), gd_dtb=pad4(p["gdn_dt_bias"]),
        gd_nw=row(p["gdn_norm_w"]), gd_wo=p["gdn_w_out"].astype(BF16),
        w_out=p["w_out"].astype(BF16),
        ffn_wg=p["ffn_w_gate"].astype(BF16), ffn_wu=p["ffn_w_up"].astype(BF16),
        ffn_wd=p["ffn_w_down"].astype(BF16),
    )


def _pad_conv_state(s):
    return jnp.pad(s, ((0, 0), (0, 0), (SUBLANE - (CONV_W - 1), 0), (0, 0)))


def _run_group(x, states, w, *, tc, ck, gdn_nseq, gdn_stack_rows, tiles_per_pass):
    nb, nt, _ = x.shape
    keep = CONV_W - 1
    assert nt >= keep and nt % SUBLANE == 0 and nb % SUBLANE == 0
    st_rg_conv, st_rg_h, st_s5_re, st_s5_im, st_gd_conv, st_gd_s = states
    depth = st_rg_h.shape[0]
    rg_conv0 = _pad_conv_state(st_rg_conv)
    gd_conv0 = _pad_conv_state(st_gd_conv)
    s5_re0 = st_s5_re.reshape(depth, nb, S5_FLAT)
    s5_im0 = st_s5_im.reshape(depth, nb, S5_FLAT)
    ydtype = BF16 if (tc % BF16_ROWS == 0 and ck % BF16_ROWS == 0) else F32
    xf = x.reshape(nb * nt, D_MODEL)
    flat = lambda a: a.reshape(nb * nt, a.shape[-1])
    outs = [[] for _ in range(5)]
    s_gd = None
    for l in range(depth):
        proj, ab = _in_proj(xf, w, l)
        proj3 = proj.reshape(nb, nt, N_MAIN)
        y_rg, h_last = _rglru(proj3, rg_conv0, st_rg_h, w, l, tc=tc, ydtype=ydtype)
        y_s5, s_re, s_im = _s5(proj3, s5_re0, s5_im0, w, l, tc=tc, tiles_per_pass=tiles_per_pass, ydtype=ydtype)
        y_gd, s_gd = _gdn(proj3, ab.reshape(nb, nt, N_AB), gd_conv0, st_gd_s, s_gd, w, l, ck=ck, nseq=gdn_nseq,
                          stack_rows=gdn_stack_rows, odtype=ydtype)
        x1 = _merge(xf, flat(y_rg), flat(y_s5), flat(y_gd), w, l)
        xf = _ffn(x1, w, l)

        outs[0].append(proj3[:, nt - keep:, COL_RG:COL_RG + RG_WIDTH])
        outs[1].append(h_last)
        outs[2].append(s_re.reshape(nb, S5_GROUPS, S5_STATE))
        outs[3].append(s_im.reshape(nb, S5_GROUPS, S5_STATE))
        outs[4].append(proj3[:, nt - keep:, COL_QKV:COL_QKV + QKV_WIDTH])
    return xf.reshape(nb, nt, D_MODEL), [jnp.stack(o) for o in outs] + [s_gd]


def _zero_states(depth, nb):
    return (jnp.zeros((depth, nb, CONV_W - 1, RG_WIDTH), F32),
            jnp.zeros((depth, nb, RG_WIDTH), F32),
            jnp.zeros((depth, nb, S5_GROUPS, S5_STATE), F32),
            jnp.zeros((depth, nb, S5_GROUPS, S5_STATE), F32),
            jnp.zeros((depth, nb, CONV_W - 1, QKV_WIDTH), F32),
            jnp.zeros((depth, nb, GDN_HEADS, GDN_DK, GDN_DV), F32))


def kernel(x_prompt, x_sample, state_rg_conv, state_rg_h, state_s5_re, state_s5_im, state_gdn_conv, state_gdn_s, norm_pre_mix, norm_post_mix, norm_pre_ffn, norm_post_ffn, w_in, rg_conv_w, rg_conv_b, rg_wa, rg_ba, rg_wx, rg_bx, rg_lambda, rg_w_out, s5_a_re, s5_a_im, s5_b_re, s5_b_im, s5_c_re, s5_c_im, s5_d, s5_log_dt, s5_glu_w, s5_glu_b, s5_w_out, gdn_conv_w, gdn_a_log, gdn_dt_bias, gdn_norm_w, gdn_w_out, w_out, ffn_w_gate, ffn_w_up, ffn_w_down):
    p = dict(norm_pre_mix=norm_pre_mix, norm_post_mix=norm_post_mix, norm_pre_ffn=norm_pre_ffn,
             norm_post_ffn=norm_post_ffn, w_in=w_in, rg_conv_w=rg_conv_w, rg_conv_b=rg_conv_b, rg_wa=rg_wa,
             rg_ba=rg_ba, rg_wx=rg_wx, rg_bx=rg_bx, rg_lambda=rg_lambda, rg_w_out=rg_w_out, s5_a_re=s5_a_re,
             s5_a_im=s5_a_im, s5_b_re=s5_b_re, s5_b_im=s5_b_im, s5_c_re=s5_c_re, s5_c_im=s5_c_im, s5_d=s5_d,
             s5_log_dt=s5_log_dt, s5_glu_w=s5_glu_w, s5_glu_b=s5_glu_b, s5_w_out=s5_w_out,
             gdn_conv_w=gdn_conv_w, gdn_a_log=gdn_a_log, gdn_dt_bias=gdn_dt_bias, gdn_norm_w=gdn_norm_w,
             gdn_w_out=gdn_w_out, w_out=w_out, ffn_w_gate=ffn_w_gate, ffn_w_up=ffn_w_up, ffn_w_down=ffn_w_down)
    depth = w_in.shape[0]
    w = _prep_params(p)
    y_p, st_p = _run_group(x_prompt, _zero_states(depth, x_prompt.shape[0]), w,
                           tc=128, ck=64, gdn_nseq=8, gdn_stack_rows=MXU_EDGE, tiles_per_pass=8)
    y_s, st_s = _run_group(
        x_sample, (state_rg_conv, state_rg_h, state_s5_re, state_s5_im, state_gdn_conv, state_gdn_s), w,
        tc=x_sample.shape[1], ck=x_sample.shape[1], gdn_nseq=16, gdn_stack_rows=MXU_EDGE, tiles_per_pass=1)
    return (y_p, y_s, *st_p, *st_s)
```

```python
import functools

import jax
import jax.numpy as jnp
from jax import lax
from jax.experimental import pallas as pl
from jax.experimental.pallas import tpu as pltpu

F32 = jnp.float32
BF16 = jnp.bfloat16

D_MODEL = 1024
CONV_W = 4
EPS = 1e-6
RG_WIDTH = 512
RG_BLOCK = 64
RG_C = 8.0
S5_WIDTH = 512
S5_GROUP = 16
S5_GROUPS = 32
S5_STATE = 64
S5_FLAT = S5_GROUPS * S5_STATE
GDN_HEADS = 4
GDN_DK = 128
GDN_DV = 128
GDN_QK = 512
GDN_V = 512
QKV_WIDTH = 1536
D_FF = 2816

LANE = 128
SUBLANE = 8
BF16_ROWS = 16
MXU_EDGE = 256
MASKED_LOG = -1e30
GDN_INV_BASE = 8

COL_RG = 0
COL_S5 = 1024
COL_QKV = 1536
COL_Z = 3072
N_MAIN = 3584
N_BRANCH = 3
N_AB = LANE

VMEM_LIMIT = 52 * 1024 * 1024

IN_PROJ_TM = 1024
MERGE_TM = 512
FFN_TM, FFN_TF = 512, 256


def _cparams(sem):
    return pltpu.CompilerParams(dimension_semantics=sem, vmem_limit_bytes=VMEM_LIMIT)


def _lspec(layer, block, index_map):
    return pl.BlockSpec((None,) + tuple(block), lambda *g: (layer,) + tuple(index_map(*g)))


def _zeros_map(n):
    return lambda *g: (0,) * n


def _rms(x, g):
    return x * lax.rsqrt(jnp.mean(x * x, axis=-1, keepdims=True) + EPS) * g


def _dot(a, b):
    return jnp.dot(a, b, preferred_element_type=F32)


def _dot_nt(a, b):
    return lax.dot_general(a, b, (((1,), (1,)), ((), ())), preferred_element_type=F32)


def _dot_tn(a, b):
    return lax.dot_general(a, b, (((0,), (0,)), ((), ())), preferred_element_type=F32)


def _split2(a):
    hi = a.astype(BF16)
    return hi, (a - hi.astype(F32)).astype(BF16)


def _dot_split(a, b):
    a1, a2 = _split2(a)
    b1, b2 = _split2(b)
    return _dot(a1, b1) + (_dot(a1, b2) + _dot(a2, b1))


def _step_pitch(nt_chunk):
    return nt_chunk if nt_chunk <= SUBLANE else nt_chunk + SUBLANE // 2


def _put_rows(ref, tile, val, nb, tc, pitch):
    if pitch == tc:
        ref[tile, 0:nb * tc, :] = val
    else:
        for b in range(nb):
            ref[tile, b * pitch:b * pitch + tc, :] = val[b * tc:(b + 1) * tc, :]


def _get_rows(ref, tile, nb, tc, pitch):
    if pitch == tc:
        return ref[tile, 0:nb * tc, :]
    return jnp.concatenate([ref[tile, b * pitch:b * pitch + tc, :] for b in range(nb)], axis=0)


def _in_proj_kernel(x_ref, g_ref, w_ref, wab_ref, o_ref, oab_ref):
    h = _rms(x_ref[...], g_ref[...]).astype(BF16)
    oab_ref[...] = _dot(h, wab_ref[...])
    o_ref[...] = _dot(h, w_ref[...])


def _in_proj(x, w, layer):
    m = x.shape[0]
    tm = min(IN_PROJ_TM, m)
    once = pl.Buffered(1)
    return pl.pallas_call(
        _in_proj_kernel,
        grid=(m // tm,),
        in_specs=[
            pl.BlockSpec((tm, D_MODEL), lambda i: (i, 0)),
            _lspec(layer, (1, D_MODEL), _zeros_map(2)),
            pl.BlockSpec((None, D_MODEL, N_MAIN), lambda i: (layer, 0, 0), pipeline_mode=once),
            pl.BlockSpec((None, D_MODEL, N_AB), lambda i: (layer, 0, 0), pipeline_mode=once),
        ],
        out_specs=[
            pl.BlockSpec((tm, N_MAIN), lambda i: (i, 0)),
            pl.BlockSpec((tm, N_AB), lambda i: (i, 0)),
        ],
        out_shape=[jax.ShapeDtypeStruct((m, N_MAIN), F32), jax.ShapeDtypeStruct((m, N_AB), F32)],
        compiler_params=_cparams(("parallel",)),
        name="in_proj",
    )(x, w["g_pre_mix"], w["w_main"], w["w_ab"])


def _rglru_kernel(p_ref, conv0_ref, h0_ref, cw_ref, cb_ref, wg_ref, ba_ref, bx_ref, lam_ref,
                  y_ref, hl_ref, xp_ref, a_ref, h_ref, hc_ref, *, nb, tc, pitch):
    ntile = RG_WIDTH // LANE
    first = SUBLANE - (CONV_W - 1)

    @pl.when(pl.program_id(0) == 0)
    def _():
        for p in range(ntile):
            xp_ref[:, p, 0:SUBLANE, :] = conv0_ref[:, :, p * LANE:(p + 1) * LANE]
        hc_ref[...] = h0_ref[...]

    cw = cw_ref[...]
    for p in range(ntile):
        cols = slice(p * LANE, (p + 1) * LANE)
        xp_ref[:, p, SUBLANE:SUBLANE + tc, :] = p_ref[:, :, cols]
        xc = cb_ref[:, cols] + xp_ref[:, p, first:first + tc, :] * cw[0:1, cols]
        for k in range(1, CONV_W):
            xc = xc + xp_ref[:, p, first + k:first + k + tc, :] * cw[k:k + 1, cols]
        xp_ref[:, p, 0:SUBLANE, :] = xp_ref[:, p, tc:tc + SUBLANE, :]
        xc = xc.reshape(nb * tc, LANE)
        gm = _dot(xc.astype(BF16), wg_ref[p])
        r = jax.nn.sigmoid(gm[:, 0:LANE] + ba_ref[:, cols])
        i = jax.nn.sigmoid(gm[:, LANE:2 * LANE] + bx_ref[:, cols])
        log_a = -RG_C * r * jax.nn.softplus(-lam_ref[:, cols])
        a = jnp.exp(log_a)
        _put_rows(a_ref, p, a, nb, tc, pitch)
        _put_rows(h_ref, p, jnp.sqrt(1.0 - a * a) * (i * xc), nb, tc, pitch)

    def step(t, hs):
        rs = pl.ds(t, nb, stride=pitch)
        out = []
        for p in range(ntile):
            h = a_ref[p, rs, :] * hs[p] + h_ref[p, rs, :]
            h_ref[p, rs, :] = h
            out.append(h)
        return tuple(out)

    hc = hc_ref[...]
    hs = lax.fori_loop(0, tc, step, tuple(hc[:, p * LANE:(p + 1) * LANE] for p in range(ntile)), unroll=2)
    for p in range(ntile):
        cols = slice(p * LANE, (p + 1) * LANE)
        hc_ref[:, cols] = hs[p]
        hl_ref[:, cols] = hs[p]
        h = _get_rows(h_ref, p, nb, tc, pitch).reshape(nb, tc, LANE)
        gate = p_ref[:, :, RG_WIDTH + p * LANE:RG_WIDTH + (p + 1) * LANE]
        y_ref[:, :, cols] = (h * jax.nn.gelu(gate)).astype(y_ref.dtype)


def _rglru(proj3, conv0, h0, w, layer, *, tc, ydtype):
    nb, nt, _ = proj3.shape
    pitch = _step_pitch(tc)
    ntile = RG_WIDTH // LANE
    row = _lspec(layer, (1, RG_WIDTH), _zeros_map(2))
    return pl.pallas_call(
        functools.partial(_rglru_kernel, nb=nb, tc=tc, pitch=pitch),
        grid=(nt // tc,),
        in_specs=[
            pl.BlockSpec((nb, tc, 2 * RG_WIDTH), lambda c: (0, c, COL_RG // (2 * RG_WIDTH))),
            _lspec(layer, (nb, SUBLANE, RG_WIDTH), _zeros_map(3)),
            _lspec(layer, (nb, RG_WIDTH), _zeros_map(2)),
            _lspec(layer, (CONV_W, RG_WIDTH), _zeros_map(2)),
            row,
            _lspec(layer, (ntile, LANE, 2 * LANE), _zeros_map(3)),
            row,
            row,
            row,
        ],
        out_specs=[
            pl.BlockSpec((nb, tc, RG_WIDTH), lambda c: (0, c, 0)),
            pl.BlockSpec((nb, RG_WIDTH), _zeros_map(2)),
        ],
        out_shape=[jax.ShapeDtypeStruct((nb, nt, RG_WIDTH), ydtype), jax.ShapeDtypeStruct((nb, RG_WIDTH), F32)],
        scratch_shapes=[
            pltpu.VMEM((nb, ntile, tc + SUBLANE, LANE), F32),
            pltpu.VMEM((ntile, nb * pitch, LANE), F32),
            pltpu.VMEM((ntile, nb * pitch, LANE), F32),
            pltpu.VMEM((nb, RG_WIDTH), F32),
        ],
        compiler_params=_cparams(("arbitrary",)),
        name="rglru",
    )(proj3, conv0, h0, w["rg_cw"], w["rg_cb"], w["rg_wg"], w["rg_ba"], w["rg_bx"], w["rg_lam"])


def _s5_kernel(u_ref, s0r_ref, s0i_ref, lr_ref, li_ref, bre_ref, bim_ref, cre_ref, cim_ref, d_ref, gw_ref, gb_ref,
               y_ref, sr_out, si_out, sre_ref, sim_ref, src_ref, sic_ref, *, nb, tc, pitch, tiles_per_pass):
    nq = S5_WIDTH // LANE
    qt = S5_FLAT // nq // LANE
    ntile = S5_FLAT // LANE

    @pl.when(pl.program_id(0) == 0)
    def _():
        src_ref[...] = s0r_ref[...]
        sic_ref[...] = s0i_ref[...]

    u = u_ref[...].reshape(nb * tc, S5_WIDTH)
    ub = u.astype(BF16)
    for q in range(nq):
        uq = ub[:, q * LANE:(q + 1) * LANE]
        bur = _dot(uq, bre_ref[q])
        bui = _dot(uq, bim_ref[q])
        for j in range(qt):
            _put_rows(sre_ref, q * qt + j, bur[:, j * LANE:(j + 1) * LANE], nb, tc, pitch)
            _put_rows(sim_ref, q * qt + j, bui[:, j * LANE:(j + 1) * LANE], nb, tc, pitch)

    for t0 in range(0, ntile, tiles_per_pass):
        tiles = list(range(t0, t0 + tiles_per_pass))
        lrs = [jnp.broadcast_to(lr_ref[:, tl * LANE:(tl + 1) * LANE], (nb, LANE)) for tl in tiles]
        lis = [jnp.broadcast_to(li_ref[:, tl * LANE:(tl + 1) * LANE], (nb, LANE)) for tl in tiles]

        def step(t, carry, tiles=tiles, lrs=lrs, lis=lis):
            rs = pl.ds(t, nb, stride=pitch)
            out = []
            for n, tl in enumerate(tiles):
                sr, si = carry[2 * n], carry[2 * n + 1]
                nr = lrs[n] * sr - lis[n] * si + sre_ref[tl, rs, :]
                ni = lrs[n] * si + lis[n] * sr + sim_ref[tl, rs, :]
                sre_ref[tl, rs, :] = nr
                sim_ref[tl, rs, :] = ni
                out += [nr, ni]
            return tuple(out)

        init = []
        for tl in tiles:
            init += [src_ref[:, tl * LANE:(tl + 1) * LANE], sic_ref[:, tl * LANE:(tl + 1) * LANE]]
        fin = lax.fori_loop(0, tc, step, tuple(init), unroll=2)
        for n, tl in enumerate(tiles):
            src_ref[:, tl * LANE:(tl + 1) * LANE] = fin[2 * n]
            sic_ref[:, tl * LANE:(tl + 1) * LANE] = fin[2 * n + 1]

    sr_out[...] = src_ref[...]
    si_out[...] = sic_ref[...]

    ys = []
    for q in range(nq):
        sr_q = jnp.concatenate([_get_rows(sre_ref, q * qt + j, nb, tc, pitch) for j in range(qt)], axis=1)
        si_q = jnp.concatenate([_get_rows(sim_ref, q * qt + j, nb, tc, pitch) for j in range(qt)], axis=1)
        ys.append(_dot(sr_q.astype(BF16), cre_ref[q]) - _dot(si_q.astype(BF16), cim_ref[q]))
    y = jnp.concatenate(ys, axis=1) + d_ref[...] * u
    y = jax.nn.gelu(y)
    y = y * jax.nn.sigmoid(_dot(y.astype(BF16), gw_ref[...]) + gb_ref[...])
    y_ref[...] = y.reshape(nb, tc, S5_WIDTH).astype(y_ref.dtype)


def _s5(proj3, s0r, s0i, w, layer, *, tc, tiles_per_pass, ydtype):
    nb, nt, _ = proj3.shape
    pitch = _step_pitch(tc)
    nq = S5_WIDTH // LANE
    qs = S5_FLAT // nq
    ntile = S5_FLAT // LANE
    state = _lspec(layer, (nb, S5_FLAT), _zeros_map(2))
    lam = _lspec(layer, (1, S5_FLAT), _zeros_map(2))
    b_in = _lspec(layer, (nq, LANE, qs), _zeros_map(3))
    c_out = _lspec(layer, (nq, qs, LANE), _zeros_map(3))
    row = _lspec(layer, (1, S5_WIDTH), _zeros_map(2))
    return pl.pallas_call(
        functools.partial(_s5_kernel, nb=nb, tc=tc, pitch=pitch, tiles_per_pass=tiles_per_pass),
        grid=(nt // tc,),
        in_specs=[
            pl.BlockSpec((nb, tc, S5_WIDTH), lambda c: (0, c, COL_S5 // S5_WIDTH)),
            state, state, lam, lam, b_in, b_in, c_out, c_out, row,
            _lspec(layer, (S5_WIDTH, S5_WIDTH), _zeros_map(2)),
            row,
        ],
        out_specs=[
            pl.BlockSpec((nb, tc, S5_WIDTH), lambda c: (0, c, 0)),
            pl.BlockSpec((nb, S5_FLAT), _zeros_map(2)),
            pl.BlockSpec((nb, S5_FLAT), _zeros_map(2)),
        ],
        out_shape=[
            jax.ShapeDtypeStruct((nb, nt, S5_WIDTH), ydtype),
            jax.ShapeDtypeStruct((nb, S5_FLAT), F32),
            jax.ShapeDtypeStruct((nb, S5_FLAT), F32),
        ],
        scratch_shapes=[
            pltpu.VMEM((ntile, nb * pitch, LANE), F32),
            pltpu.VMEM((ntile, nb * pitch, LANE), F32),
            pltpu.VMEM((nb, S5_FLAT), F32),
            pltpu.VMEM((nb, S5_FLAT), F32),
        ],
        compiler_params=_cparams(("arbitrary",)),
        name="s5",
    )(proj3, s0r, s0i, w["s5_lr"], w["s5_li"], w["s5_bre"], w["s5_bim"], w["s5_cre"], w["s5_cim"], w["s5_d"],
      w["s5_gw"], w["s5_gb"])


def _gdn_masks(sr, ck):
    shift = ck.bit_length() - 1
    row = lax.broadcasted_iota(jnp.int32, (sr, sr), 0)
    col = lax.broadcasted_iota(jnp.int32, (sr, sr), 1)
    same = (row >> shift) == (col >> shift)
    causal = same & (row >= col)
    strict = same & (row > col)
    eye = jnp.where(row == col, 1.0, 0.0).astype(F32)
    within = lambda m: (row >> (m.bit_length() - 1)) == (col >> (m.bit_length() - 1))
    base = min(ck, GDN_INV_BASE)
    inner = within(base)
    joins = []
    m = base
    while m < ck:
        joins.append(within(2 * m) & jnp.logical_not(within(m)))
        m *= 2
    return causal, strict, eye, inner, joins


def _gdn_stack(blocks, ck, masks, xc, gc_all, beta_all, z, nw, s_ref, o_ref):
    nblk = len(blocks)
    sr = nblk * ck
    causal, strict, eye, inner, joins = masks

    def stack(fn):
        return jnp.concatenate([fn(g, h) for g, h in blocks], axis=0)

    q = stack(lambda g, h: xc[h][g])
    k = stack(lambda g, h: xc[GDN_HEADS + h][g])
    v = stack(lambda g, h: xc[2 * GDN_HEADS + h][g])
    zs = stack(lambda g, h: z[g, :, h * GDN_DV:(h + 1) * GDN_DV])
    gc = stack(lambda g, h: jnp.broadcast_to(gc_all[g, :, h:h + 1], (ck, LANE)))
    beta = stack(lambda g, h: jnp.broadcast_to(beta_all[g, :, GDN_HEADS + h:GDN_HEADS + h + 1], (ck, LANE)))

    q = q * lax.rsqrt(jnp.sum(q * q, axis=-1, keepdims=True) + EPS) * (GDN_DK ** -0.5)
    k = k * lax.rsqrt(jnp.sum(k * k, axis=-1, keepdims=True) + EPS)
    kb = k * beta
    vb = v * beta
    yield

    kbf = k.astype(BF16)
    kk = _dot_nt(kb.astype(BF16), kbf)
    qk = _dot_nt(q.astype(BF16), kbf)
    yield
    gc_lanes = jnp.transpose(gc)[0:SUBLANE, :]
    gc_rows = jnp.concatenate([gc] * (sr // LANE), axis=1) if sr > LANE else gc[:, 0:sr]
    seg = gc_rows - jnp.concatenate([gc_lanes] * (sr // SUBLANE), axis=0)
    decay = jnp.exp(jnp.where(causal, seg, MASKED_LOG))
    lmat = jnp.where(strict, kk * decay, 0.0)
    qk = (qk * decay).astype(BF16)

    base = min(ck, GDN_INV_BASE)
    ld = jnp.where(inner, lmat, 0.0)
    x = -ld
    p = _dot_split(ld, ld)
    yield
    x = x + p + _dot(x.astype(BF16), p.astype(BF16))
    n = 4
    while n < base:
        pb = p.astype(BF16)
        p = _dot(pb, pb)
        yield
        x = x + p + _dot(x.astype(BF16), p.astype(BF16))
        yield
        n *= 2
    for join in joins:
        c = jnp.where(join, lmat, 0.0)
        xb = x.astype(BF16)
        ca = c + _dot(c.astype(BF16), xb)
        yield
        x = x - (ca + _dot(xb, ca.astype(BF16)))
        yield
    tmat = (eye + x).astype(BF16)
    egc = jnp.exp(gc)
    uw = _dot(tmat, jnp.concatenate([vb, kb * egc], axis=1).astype(BF16))
    yield
    u = uw[:, 0:GDN_DV]
    w = uw[:, GDN_DV:GDN_DV + GDN_DK]
    qg = q * egc
    g_last = jnp.concatenate(
        [jnp.broadcast_to(gc[(n + 1) * ck - 1:(n + 1) * ck, :], (ck, LANE)) for n in range(nblk)], axis=0)
    k_dec = (k * jnp.exp(g_last - gc)).astype(BF16)

    v_new, qs = [], []
    for n, (g, h) in enumerate(blocks):
        rs = slice(n * ck, (n + 1) * ck)
        sb = s_ref[g, h].astype(BF16)
        r = _dot(jnp.concatenate([w[rs], qg[rs]], axis=0).astype(BF16), sb)
        v_new.append(u[rs] - r[0:ck])
        qs.append(r[ck:2 * ck])
    yield
    v_new = jnp.concatenate(v_new, axis=0)
    vnb = v_new.astype(BF16)
    o = jnp.concatenate(qs, axis=0) + _dot(qk, vnb)
    for n, (g, h) in enumerate(blocks):
        rs = slice(n * ck, (n + 1) * ck)
        s_ref[g, h] = s_ref[g, h] * jnp.exp(gc[(n + 1) * ck - 1:(n + 1) * ck, :]) + _dot_tn(k_dec[rs], vnb[rs])
    yield

    o = _rms(o, nw) * (zs * jax.nn.sigmoid(zs))
    for n, (g, h) in enumerate(blocks):
        o_ref[g, :, h * GDN_DV:(h + 1) * GDN_DV] = o[n * ck:(n + 1) * ck].astype(o_ref.dtype)


def _gdn_kernel(q_ref, k_ref, v_ref, z_ref, ab_ref, conv0_ref, s0_ref, cw_ref, alog_ref, dtb_ref, nw_ref,
                *rest, ck, nseq, stack_rows, own_layer):
    o_ref, sout_ref, xb_ref, s_ref, gsum_ref = rest[-5:]
    c = pl.program_id(1)
    ntile = QKV_WIDTH // LANE
    per_src = GDN_QK // LANE

    @pl.when(c == 0)
    def _():
        for lt in range(ntile):
            xb_ref[:, lt, 0:SUBLANE, :] = conv0_ref[:, :, lt * LANE:(lt + 1) * LANE]
        s_ref[...] = s0_ref[...]

    cw = cw_ref[...]
    first = SUBLANE - (CONV_W - 1)
    xc = []
    for lt in range(ntile):
        src = (q_ref, k_ref, v_ref)[lt // per_src]
        xb_ref[:, lt, SUBLANE:SUBLANE + ck, :] = src[:, :, (lt % per_src) * LANE:(lt % per_src + 1) * LANE]
        acc = xb_ref[:, lt, first:first + ck, :] * cw[0:1, lt * LANE:(lt + 1) * LANE]
        for j in range(1, CONV_W):
            acc = acc + xb_ref[:, lt, first + j:first + j + ck, :] * cw[j:j + 1, lt * LANE:(lt + 1) * LANE]
        xb_ref[:, lt, 0:SUBLANE, :] = xb_ref[:, lt, ck:ck + SUBLANE, :]
        xc.append(acc * jax.nn.sigmoid(acc))

    ab = ab_ref[...]
    g_all = -jnp.exp(alog_ref[...]) * jax.nn.softplus(ab + dtb_ref[...])
    gsum_ref[:, 0:ck, :] = jnp.zeros_like(g_all)
    gc_all = g_all
    d = 1
    while d < ck:
        gsum_ref[:, ck:2 * ck, :] = gc_all
        gc_all = gc_all + gsum_ref[:, ck - d:2 * ck - d, :]
        d *= 2
    beta_all = jax.nn.sigmoid(ab)
    z = z_ref[...]
    nw = nw_ref[...]
    masks = _gdn_masks(stack_rows, ck)
    blocks = [(g, h) for g in range(nseq) for h in range(GDN_HEADS)]
    per = stack_rows // ck
    stacks = [_gdn_stack(blocks[b0:b0 + per], ck, masks, xc, gc_all, beta_all, z, nw, s_ref, o_ref)
              for b0 in range(0, len(blocks), per)]
    while stacks:
        stacks = [s for s in stacks if next(s, True) is None]

    @pl.when(c == pl.num_programs(1) - 1)
    def _():
        if own_layer is None:
            sout_ref[...] = s_ref[...]
        else:
            sout_ref[...] = jnp.zeros_like(sout_ref)
            sout_ref[own_layer] = s_ref[...]


def _gdn(proj3, ab3, conv0, s_all, s_done, w, layer, *, ck, nseq, stack_rows, odtype):
    nb, nt, _ = proj3.shape
    assert (nseq * GDN_HEADS * ck) % stack_rows == 0 and stack_rows % ck == 0
    qblk = COL_QKV // GDN_QK
    state_block = (nseq, GDN_HEADS, GDN_DK, GDN_DV)
    state_spec = _lspec(layer, state_block, lambda i, c: (i, 0, 0, 0))
    head_row = _lspec(layer, (1, LANE), _zeros_map(2))
    operands = [proj3, proj3, proj3, proj3, ab3, conv0, s_all, w["gd_cw"], w["gd_alog"], w["gd_dtb"], w["gd_nw"]]
    extra_specs, aliases = [], {}
    if s_done is not None:
        aliases = {len(operands): 1}
        operands.append(s_done)
        extra_specs = [pl.BlockSpec(memory_space=pl.ANY)]
        out_state_spec, own_layer = state_spec, None
    else:
        out_state_spec = pl.BlockSpec((s_all.shape[0],) + state_block, lambda i, c: (0, i, 0, 0, 0))
        own_layer = layer
    return pl.pallas_call(
        functools.partial(_gdn_kernel, ck=ck, nseq=nseq, stack_rows=stack_rows, own_layer=own_layer),
        grid=(nb // nseq, nt // ck),
        input_output_aliases=aliases,
        in_specs=[
            pl.BlockSpec((nseq, ck, GDN_QK), lambda i, c: (i, c, qblk)),
            pl.BlockSpec((nseq, ck, GDN_QK), lambda i, c: (i, c, qblk + 1)),
            pl.BlockSpec((nseq, ck, GDN_V), lambda i, c: (i, c, qblk + 2)),
            pl.BlockSpec((nseq, ck, GDN_V), lambda i, c: (i, c, COL_Z // GDN_V)),
            pl.BlockSpec((nseq, ck, N_AB), lambda i, c: (i, c, 0)),
            _lspec(layer, (nseq, SUBLANE, QKV_WIDTH), lambda i, c: (i, 0, 0)),
            state_spec,
            _lspec(layer, (CONV_W, QKV_WIDTH), _zeros_map(2)),
            head_row,
            head_row,
            head_row,
        ] + extra_specs,
        out_specs=[
            pl.BlockSpec((nseq, ck, GDN_V), lambda i, c: (i, c, 0)),
            out_state_spec,
        ],
        out_shape=[
            jax.ShapeDtypeStruct((nb, nt, GDN_V), odtype),
            jax.ShapeDtypeStruct(s_all.shape, F32),
        ],
        scratch_shapes=[
            pltpu.VMEM((nseq, QKV_WIDTH // LANE, ck + SUBLANE, LANE), F32),
            pltpu.VMEM((nseq, GDN_HEADS, GDN_DK, GDN_DV), F32),
            pltpu.VMEM((nseq, 2 * ck, N_AB), F32),
        ],
        compiler_params=_cparams(("parallel", "arbitrary")),
        name="gdn",
    )(*operands)


def _merge_kernel(x_ref, yrg_ref, ys5_ref, ygd_ref, gpre_ref, wgate_ref, wrg_ref, ws5_ref, wgd_ref,
                  wo_ref, gpost_ref, o_ref):
    x = x_ref[...]
    h = _rms(x, gpre_ref[...]).astype(BF16)
    mixed = None
    for n, (y_ref, w_ref) in enumerate(((yrg_ref, wrg_ref), (ys5_ref, ws5_ref), (ygd_ref, wgd_ref))):
        gate = jax.nn.sigmoid(_dot(h, wgate_ref[:, n * D_MODEL:(n + 1) * D_MODEL]))
        term = gate * _dot(y_ref[...].astype(BF16), w_ref[...])
        mixed = term if mixed is None else mixed + term
    out = _dot(mixed.astype(BF16), wo_ref[...])
    o_ref[...] = x + _rms(out, gpost_ref[...])


def _merge(x, yrg, ys5, ygd, w, layer):
    m = x.shape[0]
    tm = min(MERGE_TM, m)
    half = D_MODEL // 2
    branch_out = _lspec(layer, (half, D_MODEL), _zeros_map(2))
    row = _lspec(layer, (1, D_MODEL), _zeros_map(2))
    return pl.pallas_call(
        _merge_kernel,
        grid=(m // tm,),
        in_specs=[
            pl.BlockSpec((tm, D_MODEL), lambda i: (i, 0)),
            pl.BlockSpec((tm, half), lambda i: (i, 0)),
            pl.BlockSpec((tm, half), lambda i: (i, 0)),
            pl.BlockSpec((tm, half), lambda i: (i, 0)),
            row,
            _lspec(layer, (D_MODEL, N_BRANCH * D_MODEL), _zeros_map(2)),
            branch_out, branch_out, branch_out,
            _lspec(layer, (D_MODEL, D_MODEL), _zeros_map(2)),
            row,
        ],
        out_specs=pl.BlockSpec((tm, D_MODEL), lambda i: (i, 0)),
        out_shape=jax.ShapeDtypeStruct((m, D_MODEL), F32),
        compiler_params=_cparams(("parallel",)),
        name="merge",
    )(x, yrg, ys5, ygd, w["g_pre_mix"], w["w_gates"], w["rg_wo"], w["s5_wo"], w["gd_wo"], w["w_out"],
      w["g_post_mix"])


def _ffn_kernel(x_ref, gpre_ref, wg_ref, wu_ref, wd_ref, gpost_ref, o_ref, *, tf):
    x = x_ref[...]
    h = _rms(x, gpre_ref[...]).astype(BF16)
    acc = None
    for f0 in range(0, D_FF, tf):
        gate = _dot(h, wg_ref[:, f0:f0 + tf])
        act = (gate * jax.nn.sigmoid(gate)) * _dot(h, wu_ref[:, f0:f0 + tf])
        part = _dot(act.astype(BF16), wd_ref[f0:f0 + tf, :])
        acc = part if acc is None else acc + part
    o_ref[...] = x + _rms(acc, gpost_ref[...])


def _ffn(x, w, layer):
    m = x.shape[0]
    tm = min(FFN_TM, m)
    row = _lspec(layer, (1, D_MODEL), _zeros_map(2))
    once = pl.Buffered(1)
    wspec = lambda shape: pl.BlockSpec((None,) + shape, lambda i: (layer, 0, 0), pipeline_mode=once)
    return pl.pallas_call(
        functools.partial(_ffn_kernel, tf=FFN_TF),
        grid=(m // tm,),
        in_specs=[
            pl.BlockSpec((tm, D_MODEL), lambda i: (i, 0)),
            row,
            wspec((D_MODEL, D_FF)),
            wspec((D_MODEL, D_FF)),
            wspec((D_FF, D_MODEL)),
            row,
        ],
        out_specs=pl.BlockSpec((tm, D_MODEL), lambda i: (i, 0)),
        out_shape=jax.ShapeDtypeStruct((m, D_MODEL), F32),
        compiler_params=_cparams(("parallel",)),
        name="ffn",
    )(x, w["g_pre_ffn"], w["ffn_wg"], w["ffn_wu"], w["ffn_wd"], w["g_post_ffn"])


def _block_diag(blocks):
    n, r, c = blocks.shape[-3:]
    eye = jnp.eye(n, dtype=blocks.dtype)
    out = jnp.einsum("...nrc,nm->...nrmc", blocks, eye)
    return out.reshape(blocks.shape[:-3] + (n * r, n * c))


def _prep_params(p):
    f32 = F32
    depth = p["w_in"].shape[0]
    w_in = p["w_in"]
    sizes = (RG_WIDTH, RG_WIDTH, S5_WIDTH, QKV_WIDTH, GDN_V, GDN_HEADS, GDN_HEADS, 3 * D_MODEL)
    offs = [0]
    for s in sizes:
        offs.append(offs[-1] + s)
    assert offs[5] == N_MAIN
    w_main = w_in.astype(BF16)
    w_gates = w_main[:, :, offs[7]:offs[8]]
    w_ab = jnp.concatenate(
        [w_main[:, :, offs[5]:offs[7]], jnp.zeros((depth, D_MODEL, N_AB - 2 * GDN_HEADS), BF16)], axis=2)

    pair = LANE // RG_BLOCK
    pairs = lambda a: _block_diag(a.reshape(depth, RG_WIDTH // LANE, pair, RG_BLOCK, RG_BLOCK))
    wg = jnp.concatenate([pairs(p["rg_wa"]), pairs(p["rg_wx"])], axis=-1).astype(BF16)

    a_re = p["s5_a_re"].astype(f32)
    a_im = p["s5_a_im"].astype(f32)
    dt = jnp.exp(p["s5_log_dt"].astype(f32))[..., None]
    mag = jnp.exp(a_re * dt)
    lr = mag * jnp.cos(a_im * dt)
    li = mag * jnp.sin(a_im * dt)
    den = a_re * a_re + a_im * a_im
    cr = (((lr - 1.0) * a_re + li * a_im) / den)[..., None]
    ci = ((li * a_re - (lr - 1.0) * a_im) / den)[..., None]
    b_re = p["s5_b_re"].astype(f32)
    b_im = p["s5_b_im"].astype(f32)
    bb_re = cr * b_re - ci * b_im
    bb_im = cr * b_im + ci * b_re
    nq = S5_WIDTH // LANE
    gq = S5_GROUPS // nq
    to_in = lambda a: _block_diag(jnp.swapaxes(a, 2, 3).reshape(depth, nq, gq, S5_GROUP, S5_STATE))
    to_out = lambda a: _block_diag(jnp.swapaxes(a.astype(f32), 2, 3).reshape(depth, nq, gq, S5_STATE, S5_GROUP))

    row = lambda a: a.astype(f32).reshape(depth, 1, -1)
    pad4 = lambda a: jnp.pad(a.astype(f32), ((0, 0), (0, LANE - GDN_HEADS))).reshape(depth, 1, LANE)
    return dict(
        g_pre_mix=row(p["norm_pre_mix"]), g_post_mix=row(p["norm_post_mix"]),
        g_pre_ffn=row(p["norm_pre_ffn"]), g_post_ffn=row(p["norm_post_ffn"]),
        w_main=w_main, w_ab=w_ab, w_gates=w_gates,
        rg_cw=p["rg_conv_w"].astype(f32), rg_cb=row(p["rg_conv_b"]), rg_wg=wg,
        rg_ba=row(p["rg_ba"]), rg_bx=row(p["rg_bx"]), rg_lam=row(p["rg_lambda"]),
        rg_wo=p["rg_w_out"].astype(BF16),
        s5_lr=lr.reshape(depth, 1, S5_FLAT), s5_li=li.reshape(depth, 1, S5_FLAT),
        s5_bre=to_in(bb_re).astype(BF16), s5_bim=to_in(bb_im).astype(BF16),
        s5_cre=to_out(p["s5_c_re"]).astype(BF16), s5_cim=to_out(p["s5_c_im"]).astype(BF16),
        s5_d=row(p["s5_d"]), s5_gw=p["s5_glu_w"].astype(BF16), s5_gb=row(p["s5_glu_b"]),
        s5_wo=p["s5_w_out"].astype(BF16),
        gd_cw=p["gdn_conv_w"].astype(f32), gd_alog=pad4(p["gdn_a_log"]), gd_dtb=pad4(p["gdn_dt_bias"]),
        gd_nw=row(p["gdn_norm_w"]), gd_wo=p["gdn_w_out"].astype(BF16),
        w_out=p["w_out"].astype(BF16),
        ffn_wg=p["ffn_w_gate"].astype(BF16), ffn_wu=p["ffn_w_up"].astype(BF16),
        ffn_wd=p["ffn_w_down"].astype(BF16),
    )


def _pad_conv_state(s):
    return jnp.pad(s, ((0, 0), (0, 0), (SUBLANE - (CONV_W - 1), 0), (0, 0)))


def _run_group(x, states, w, *, tc, ck, gdn_nseq, gdn_stack_rows, tiles_per_pass):
    nb, nt, _ = x.shape
    keep = CONV_W - 1
    assert nt >= keep and nt % SUBLANE == 0 and nb % SUBLANE == 0
    st_rg_conv, st_rg_h, st_s5_re, st_s5_im, st_gd_conv, st_gd_s = states
    depth = st_rg_h.shape[0]
    rg_conv0 = _pad_conv_state(st_rg_conv)
    gd_conv0 = _pad_conv_state(st_gd_conv)
    s5_re0 = st_s5_re.reshape(depth, nb, S5_FLAT)
    s5_im0 = st_s5_im.reshape(depth, nb, S5_FLAT)
    ydtype = BF16 if (tc % BF16_ROWS == 0 and ck % BF16_ROWS == 0) else F32
    xf = x.reshape(nb * nt, D_MODEL)
    flat = lambda a: a.reshape(nb * nt, a.shape[-1])
    outs = [[] for _ in range(5)]
    s_gd = None
    for l in range(depth):
        proj, ab = _in_proj(xf, w, l)
        proj3 = proj.reshape(nb, nt, N_MAIN)
        y_rg, h_last = _rglru(proj3, rg_conv0, st_rg_h, w, l, tc=tc, ydtype=ydtype)
        y_s5, s_re, s_im = _s5(proj3, s5_re0, s5_im0, w, l, tc=tc, tiles_per_pass=tiles_per_pass, ydtype=ydtype)
        y_gd, s_gd = _gdn(proj3, ab.reshape(nb, nt, N_AB), gd_conv0, st_gd_s, s_gd, w, l, ck=ck, nseq=gdn_nseq,
                          stack_rows=gdn_stack_rows, odtype=ydtype)
        x1 = _merge(xf, flat(y_rg), flat(y_s5), flat(y_gd), w, l)
        xf = _ffn(x1, w, l)

        outs[0].append(proj3[:, nt - keep:, COL_RG:COL_RG + RG_WIDTH])
        outs[1].append(h_last)
        outs[2].append(s_re.reshape(nb, S5_GROUPS, S5_STATE))
        outs[3].append(s_im.reshape(nb, S5_GROUPS, S5_STATE))
        outs[4].append(proj3[:, nt - keep:, COL_QKV:COL_QKV + QKV_WIDTH])
    return xf.reshape(nb, nt, D_MODEL), [jnp.stack(o) for o in outs] + [s_gd]


def _zero_states(depth, nb):
    return (jnp.zeros((depth, nb, CONV_W - 1, RG_WIDTH), F32),
            jnp.zeros((depth, nb, RG_WIDTH), F32),
            jnp.zeros((depth, nb, S5_GROUPS, S5_STATE), F32),
            jnp.zeros((depth, nb, S5_GROUPS, S5_STATE), F32),
            jnp.zeros((depth, nb, CONV_W - 1, QKV_WIDTH), F32),
            jnp.zeros((depth, nb, GDN_HEADS, GDN_DK, GDN_DV), F32))


def kernel(x_prompt, x_sample, state_rg_conv, state_rg_h, state_s5_re, state_s5_im, state_gdn_conv, state_gdn_s, norm_pre_mix, norm_post_mix, norm_pre_ffn, norm_post_ffn, w_in, rg_conv_w, rg_conv_b, rg_wa, rg_ba, rg_wx, rg_bx, rg_lambda, rg_w_out, s5_a_re, s5_a_im, s5_b_re, s5_b_im, s5_c_re, s5_c_im, s5_d, s5_log_dt, s5_glu_w, s5_glu_b, s5_w_out, gdn_conv_w, gdn_a_log, gdn_dt_bias, gdn_norm_w, gdn_w_out, w_out, ffn_w_gate, ffn_w_up, ffn_w_down):
    p = dict(norm_pre_mix=norm_pre_mix, norm_post_mix=norm_post_mix, norm_pre_ffn=norm_pre_ffn,
             norm_post_ffn=norm_post_ffn, w_in=w_in, rg_conv_w=rg_conv_w, rg_conv_b=rg_conv_b, rg_wa=rg_wa,
             rg_ba=rg_ba, rg_wx=rg_wx, rg_bx=rg_bx, rg_lambda=rg_lambda, rg_w_out=rg_w_out, s5_a_re=s5_a_re,
             s5_a_im=s5_a_im, s5_b_re=s5_b_re, s5_b_im=s5_b_im, s5_c_re=s5_c_re, s5_c_im=s5_c_im, s5_d=s5_d,
             s5_log_dt=s5_log_dt, s5_glu_w=s5_glu_w, s5_glu_b=s5_glu_b, s5_w_out=s5_w_out,
             gdn_conv_w=gdn_conv_w, gdn_a_log=gdn_a_log, gdn_dt_bias=gdn_dt_bias, gdn_norm_w=gdn_norm_w,
             gdn_w_out=gdn_w_out, w_out=w_out, ffn_w_gate=ffn_w_gate, ffn_w_up=ffn_w_up, ffn_w_down=ffn_w_down)
    depth = w_in.shape[0]
    w = _prep_params(p)
    y_p, st_p = _run_group(x_prompt, _zero_states(depth, x_prompt.shape[0]), w,
                           tc=128, ck=64, gdn_nseq=8, gdn_stack_rows=MXU_EDGE, tiles_per_pass=8)
    y_s, st_s = _run_group(
        x_sample, (state_rg_conv, state_rg_h, state_s5_re, state_s5_im, state_gdn_conv, state_gdn_s), w,
        tc=x_sample.shape[1], ck=x_sample.shape[1], gdn_nseq=16, gdn_stack_rows=MXU_EDGE, tiles_per_pass=1)
    return (y_p, y_s, *st_p, *st_s)
```

```python
import functools

import jax
import jax.numpy as jnp
from jax import lax
from jax.experimental import pallas as pl
from jax.experimental.pallas import tpu as pltpu

F32 = jnp.float32
BF16 = jnp.bfloat16

D_MODEL = 1024
CONV_W = 4
EPS = 1e-6
RG_WIDTH = 512
RG_BLOCK = 64
RG_C = 8.0
S5_WIDTH = 512
S5_GROUP = 16
S5_GROUPS = 32
S5_STATE = 64
S5_FLAT = S5_GROUPS * S5_STATE
GDN_HEADS = 4
GDN_DK = 128
GDN_DV = 128
GDN_QK = 512
GDN_V = 512
QKV_WIDTH = 1536
D_FF = 2816

LANE = 128
SUBLANE = 8
BF16_ROWS = 16
MXU_EDGE = 256
MASKED_LOG = -1e30
GDN_INV_BASE = 8

COL_RG = 0
COL_S5 = 1024
COL_QKV = 1536
COL_Z = 3072
N_MAIN = 3584
N_BRANCH = 3
N_AB = LANE

VMEM_LIMIT = 52 * 1024 * 1024

IN_PROJ_TM = 512
MERGE_TM = 512
FFN_TM, FFN_TF = 512, 256


def _cparams(sem):
    return pltpu.CompilerParams(dimension_semantics=sem, vmem_limit_bytes=VMEM_LIMIT)


def _lspec(layer, block, index_map):
    return pl.BlockSpec((None,) + tuple(block), lambda *g: (layer,) + tuple(index_map(*g)))


def _zeros_map(n):
    return lambda *g: (0,) * n


def _rms(x, g):
    return x * lax.rsqrt(jnp.mean(x * x, axis=-1, keepdims=True) + EPS) * g


def _dot(a, b):
    return jnp.dot(a, b, preferred_element_type=F32)


def _dot_nt(a, b):
    return lax.dot_general(a, b, (((1,), (1,)), ((), ())), preferred_element_type=F32)


def _dot_tn(a, b):
    return lax.dot_general(a, b, (((0,), (0,)), ((), ())), preferred_element_type=F32)


def _split2(a):
    hi = a.astype(BF16)
    return hi, (a - hi.astype(F32)).astype(BF16)


def _dot_split(a, b):
    a1, a2 = _split2(a)
    b1, b2 = _split2(b)
    return _dot(a1, b1) + (_dot(a1, b2) + _dot(a2, b1))


def _step_pitch(nt_chunk):
    return nt_chunk if nt_chunk <= SUBLANE else nt_chunk + SUBLANE // 2


def _put_rows(ref, tile, val, nb, tc, pitch):
    if pitch == tc:
        ref[tile, 0:nb * tc, :] = val
    else:
        for b in range(nb):
            ref[tile, b * pitch:b * pitch + tc, :] = val[b * tc:(b + 1) * tc, :]


def _get_rows(ref, tile, nb, tc, pitch):
    if pitch == tc:
        return ref[tile, 0:nb * tc, :]
    return jnp.concatenate([ref[tile, b * pitch:b * pitch + tc, :] for b in range(nb)], axis=0)


def _row_tiled_call(body, groups, consts, const_specs, out_cols, tm, name):
    n_tiles = [g[0].shape[0] // tm for g in groups]
    assert all(a.shape[0] == n * tm for g, n in zip(groups, n_tiles) for a in g)
    starts = [sum(n_tiles[:g]) for g in range(len(groups))]
    n_in, n_out, n_const = len(groups[0]), len(out_cols), len(consts)

    def spec(width, g):
        return pl.BlockSpec((tm, width), lambda i, lo=starts[g], n=n_tiles[g]: (jnp.clip(i - lo, 0, n - 1), 0))

    def kernel(*refs):
        ins, cs, outs = refs[:len(groups) * n_in], refs[len(groups) * n_in:][:n_const], refs[-len(groups) * n_out:]
        i = pl.program_id(0)
        for g in range(len(groups)):
            @pl.when((i >= starts[g]) & (i < starts[g] + n_tiles[g]))
            def _(g=g):
                body(ins[g * n_in:(g + 1) * n_in], cs, outs[g * n_out:(g + 1) * n_out])

    outs = pl.pallas_call(
        kernel,
        grid=(sum(n_tiles),),
        in_specs=[spec(a.shape[1], g) for g, arrs in enumerate(groups) for a in arrs] + list(const_specs),
        out_specs=[spec(wd, g) for g in range(len(groups)) for wd, _ in out_cols],
        out_shape=[jax.ShapeDtypeStruct((groups[g][0].shape[0], wd), dt)
                   for g in range(len(groups)) for wd, dt in out_cols],
        compiler_params=_cparams(("arbitrary",)),
        name=name,
    )(*[a for arrs in groups for a in arrs], *consts)
    return [outs[g * n_out:(g + 1) * n_out] for g in range(len(groups))]


def _resident(layer, shape):
    return pl.BlockSpec((None,) + tuple(shape), lambda i: (layer,) + (0,) * len(shape), pipeline_mode=pl.Buffered(1))


def _in_proj_body(ins, cs, outs):
    (x_ref,), (g_ref, w_ref, wab_ref), (o_ref, oab_ref) = ins, cs, outs
    h = _rms(x_ref[...], g_ref[...]).astype(BF16)
    oab_ref[...] = _dot(h, wab_ref[...])
    o_ref[...] = _dot(h, w_ref[...])


def _in_proj(xs, w, layer):
    tm = min([IN_PROJ_TM] + [x.shape[0] for x in xs])
    return _row_tiled_call(
        _in_proj_body, [[x] for x in xs], [w["g_pre_mix"], w["w_main"], w["w_ab"]],
        [_lspec(layer, (1, D_MODEL), _zeros_map(2)), _resident(layer, (D_MODEL, N_MAIN)),
         _resident(layer, (D_MODEL, N_AB))],
        [(N_MAIN, F32), (N_AB, F32)], tm, "in_proj")


def _rglru_kernel(p_ref, conv0_ref, h0_ref, cw_ref, cb_ref, wg_ref, ba_ref, bx_ref, lam_ref,
                  y_ref, hl_ref, xp_ref, a_ref, h_ref, hc_ref, *, nb, tc, pitch):
    ntile = RG_WIDTH // LANE
    first = SUBLANE - (CONV_W - 1)

    @pl.when(pl.program_id(0) == 0)
    def _():
        for p in range(ntile):
            xp_ref[:, p, 0:SUBLANE, :] = conv0_ref[:, :, p * LANE:(p + 1) * LANE]
        hc_ref[...] = h0_ref[...]

    cw = cw_ref[...]
    for p in range(ntile):
        cols = slice(p * LANE, (p + 1) * LANE)
        xp_ref[:, p, SUBLANE:SUBLANE + tc, :] = p_ref[:, :, cols]
        xc = cb_ref[:, cols] + xp_ref[:, p, first:first + tc, :] * cw[0:1, cols]
        for k in range(1, CONV_W):
            xc = xc + xp_ref[:, p, first + k:first + k + tc, :] * cw[k:k + 1, cols]
        xp_ref[:, p, 0:SUBLANE, :] = xp_ref[:, p, tc:tc + SUBLANE, :]
        xc = xc.reshape(nb * tc, LANE)
        gm = _dot(xc.astype(BF16), wg_ref[p])
        r = jax.nn.sigmoid(gm[:, 0:LANE] + ba_ref[:, cols])
        i = jax.nn.sigmoid(gm[:, LANE:2 * LANE] + bx_ref[:, cols])
        log_a = -RG_C * r * jax.nn.softplus(-lam_ref[:, cols])
        a = jnp.exp(log_a)
        _put_rows(a_ref, p, a, nb, tc, pitch)
        _put_rows(h_ref, p, jnp.sqrt(1.0 - a * a) * (i * xc), nb, tc, pitch)

    def step(t, hs):
        rs = pl.ds(t, nb, stride=pitch)
        out = []
        for p in range(ntile):
            h = a_ref[p, rs, :] * hs[p] + h_ref[p, rs, :]
            h_ref[p, rs, :] = h
            out.append(h)
        return tuple(out)

    hc = hc_ref[...]
    hs = lax.fori_loop(0, tc, step, tuple(hc[:, p * LANE:(p + 1) * LANE] for p in range(ntile)), unroll=2)
    for p in range(ntile):
        cols = slice(p * LANE, (p + 1) * LANE)
        hc_ref[:, cols] = hs[p]
        hl_ref[:, cols] = hs[p]
        h = _get_rows(h_ref, p, nb, tc, pitch).reshape(nb, tc, LANE)
        gate = p_ref[:, :, RG_WIDTH + p * LANE:RG_WIDTH + (p + 1) * LANE]
        y_ref[:, :, cols] = (h * jax.nn.gelu(gate)).astype(y_ref.dtype)


def _rglru(proj3, conv0, h0, w, layer, *, tc, ydtype):
    nb, nt, _ = proj3.shape
    pitch = _step_pitch(tc)
    ntile = RG_WIDTH // LANE
    row = _lspec(layer, (1, RG_WIDTH), _zeros_map(2))
    return pl.pallas_call(
        functools.partial(_rglru_kernel, nb=nb, tc=tc, pitch=pitch),
        grid=(nt // tc,),
        in_specs=[
            pl.BlockSpec((nb, tc, 2 * RG_WIDTH), lambda c: (0, c, COL_RG // (2 * RG_WIDTH))),
            _lspec(layer, (nb, SUBLANE, RG_WIDTH), _zeros_map(3)),
            _lspec(layer, (nb, RG_WIDTH), _zeros_map(2)),
            _lspec(layer, (CONV_W, RG_WIDTH), _zeros_map(2)),
            row,
            _lspec(layer, (ntile, LANE, 2 * LANE), _zeros_map(3)),
            row,
            row,
            row,
        ],
        out_specs=[
            pl.BlockSpec((nb, tc, RG_WIDTH), lambda c: (0, c, 0)),
            pl.BlockSpec((nb, RG_WIDTH), _zeros_map(2)),
        ],
        out_shape=[jax.ShapeDtypeStruct((nb, nt, RG_WIDTH), ydtype), jax.ShapeDtypeStruct((nb, RG_WIDTH), F32)],
        scratch_shapes=[
            pltpu.VMEM((nb, ntile, tc + SUBLANE, LANE), F32),
            pltpu.VMEM((ntile, nb * pitch, LANE), F32),
            pltpu.VMEM((ntile, nb * pitch, LANE), F32),
            pltpu.VMEM((nb, RG_WIDTH), F32),
        ],
        compiler_params=_cparams(("arbitrary",)),
        name="rglru",
    )(proj3, conv0, h0, w["rg_cw"], w["rg_cb"], w["rg_wg"], w["rg_ba"], w["rg_bx"], w["rg_lam"])


def _s5_kernel(u_ref, s0r_ref, s0i_ref, lr_ref, li_ref, bre_ref, bim_ref, cre_ref, cim_ref, d_ref, gw_ref, gb_ref,
               y_ref, sr_out, si_out, sre_ref, sim_ref, src_ref, sic_ref, *, nb, tc, pitch, tiles_per_pass):
    nq = S5_WIDTH // LANE
    qt = S5_FLAT // nq // LANE
    ntile = S5_FLAT // LANE

    @pl.when(pl.program_id(0) == 0)
    def _():
        src_ref[...] = s0r_ref[...]
        sic_ref[...] = s0i_ref[...]

    u = u_ref[...].reshape(nb * tc, S5_WIDTH)
    ub = u.astype(BF16)
    for q in range(nq):
        uq = ub[:, q * LANE:(q + 1) * LANE]
        bur = _dot(uq, bre_ref[q])
        bui = _dot(uq, bim_ref[q])
        for j in range(qt):
            _put_rows(sre_ref, q * qt + j, bur[:, j * LANE:(j + 1) * LANE], nb, tc, pitch)
            _put_rows(sim_ref, q * qt + j, bui[:, j * LANE:(j + 1) * LANE], nb, tc, pitch)

    for t0 in range(0, ntile, tiles_per_pass):
        tiles = list(range(t0, t0 + tiles_per_pass))
        lrs = [jnp.broadcast_to(lr_ref[:, tl * LANE:(tl + 1) * LANE], (nb, LANE)) for tl in tiles]
        lis = [jnp.broadcast_to(li_ref[:, tl * LANE:(tl + 1) * LANE], (nb, LANE)) for tl in tiles]

        def step(t, carry, tiles=tiles, lrs=lrs, lis=lis):
            rs = pl.ds(t, nb, stride=pitch)
            out = []
            for n, tl in enumerate(tiles):
                sr, si = carry[2 * n], carry[2 * n + 1]
                nr = lrs[n] * sr - lis[n] * si + sre_ref[tl, rs, :]
                ni = lrs[n] * si + lis[n] * sr + sim_ref[tl, rs, :]
                sre_ref[tl, rs, :] = nr
                sim_ref[tl, rs, :] = ni
                out += [nr, ni]
            return tuple(out)

        init = []
        for tl in tiles:
            init += [src_ref[:, tl * LANE:(tl + 1) * LANE], sic_ref[:, tl * LANE:(tl + 1) * LANE]]
        fin = lax.fori_loop(0, tc, step, tuple(init), unroll=2)
        for n, tl in enumerate(tiles):
            src_ref[:, tl * LANE:(tl + 1) * LANE] = fin[2 * n]
            sic_ref[:, tl * LANE:(tl + 1) * LANE] = fin[2 * n + 1]

    sr_out[...] = src_ref[...]
    si_out[...] = sic_ref[...]

    ys = []
    for q in range(nq):
        sr_q = jnp.concatenate([_get_rows(sre_ref, q * qt + j, nb, tc, pitch) for j in range(qt)], axis=1)
        si_q = jnp.concatenate([_get_rows(sim_ref, q * qt + j, nb, tc, pitch) for j in range(qt)], axis=1)
        ys.append(_dot(sr_q.astype(BF16), cre_ref[q]) - _dot(si_q.astype(BF16), cim_ref[q]))
    y = jnp.concatenate(ys, axis=1) + d_ref[...] * u
    y = jax.nn.gelu(y)
    y = y * jax.nn.sigmoid(_dot(y.astype(BF16), gw_ref[...]) + gb_ref[...])
    y_ref[...] = y.reshape(nb, tc, S5_WIDTH).astype(y_ref.dtype)


def _s5(proj3, s0r, s0i, w, layer, *, tc, tiles_per_pass, ydtype):
    nb, nt, _ = proj3.shape
    pitch = _step_pitch(tc)
    nq = S5_WIDTH // LANE
    qs = S5_FLAT // nq
    ntile = S5_FLAT // LANE
    state = _lspec(layer, (nb, S5_FLAT), _zeros_map(2))
    lam = _lspec(layer, (1, S5_FLAT), _zeros_map(2))
    b_in = _lspec(layer, (nq, LANE, qs), _zeros_map(3))
    c_out = _lspec(layer, (nq, qs, LANE), _zeros_map(3))
    row = _lspec(layer, (1, S5_WIDTH), _zeros_map(2))
    return pl.pallas_call(
        functools.partial(_s5_kernel, nb=nb, tc=tc, pitch=pitch, tiles_per_pass=tiles_per_pass),
        grid=(nt // tc,),
        in_specs=[
            pl.BlockSpec((nb, tc, S5_WIDTH), lambda c: (0, c, COL_S5 // S5_WIDTH)),
            state, state, lam, lam, b_in, b_in, c_out, c_out, row,
            _lspec(layer, (S5_WIDTH, S5_WIDTH), _zeros_map(2)),
            row,
        ],
        out_specs=[
            pl.BlockSpec((nb, tc, S5_WIDTH), lambda c: (0, c, 0)),
            pl.BlockSpec((nb, S5_FLAT), _zeros_map(2)),
            pl.BlockSpec((nb, S5_FLAT), _zeros_map(2)),
        ],
        out_shape=[
            jax.ShapeDtypeStruct((nb, nt, S5_WIDTH), ydtype),
            jax.ShapeDtypeStruct((nb, S5_FLAT), F32),
            jax.ShapeDtypeStruct((nb, S5_FLAT), F32),
        ],
        scratch_shapes=[
            pltpu.VMEM((ntile, nb * pitch, LANE), F32),
            pltpu.VMEM((ntile, nb * pitch, LANE), F32),
            pltpu.VMEM((nb, S5_FLAT), F32),
            pltpu.VMEM((nb, S5_FLAT), F32),
        ],
        compiler_params=_cparams(("arbitrary",)),
        name="s5",
    )(proj3, s0r, s0i, w["s5_lr"], w["s5_li"], w["s5_bre"], w["s5_bim"], w["s5_cre"], w["s5_cim"], w["s5_d"],
      w["s5_gw"], w["s5_gb"])


def _gdn_masks(sr, ck):
    shift = ck.bit_length() - 1
    row = lax.broadcasted_iota(jnp.int32, (sr, sr), 0)
    col = lax.broadcasted_iota(jnp.int32, (sr, sr), 1)
    same = (row >> shift) == (col >> shift)
    causal = same & (row >= col)
    strict = same & (row > col)
    eye = jnp.where(row == col, 1.0, 0.0).astype(F32)
    within = lambda m: (row >> (m.bit_length() - 1)) == (col >> (m.bit_length() - 1))
    base = min(ck, GDN_INV_BASE)
    inner = within(base)
    joins = []
    m = base
    while m < ck:
        joins.append(within(2 * m) & jnp.logical_not(within(m)))
        m *= 2
    return causal, strict, eye, inner, joins


def _gdn_stack(blocks, ck, masks, xc, gc_all, beta_all, z, nw, s_ref, o_ref):
    nblk = len(blocks)
    sr = nblk * ck
    causal, strict, eye, inner, joins = masks

    def stack(fn):
        return jnp.concatenate([fn(g, h) for g, h in blocks], axis=0)

    q = stack(lambda g, h: xc[h][g])
    k = stack(lambda g, h: xc[GDN_HEADS + h][g])
    v = stack(lambda g, h: xc[2 * GDN_HEADS + h][g])
    zs = stack(lambda g, h: z[g, :, h * GDN_DV:(h + 1) * GDN_DV])
    gc = stack(lambda g, h: jnp.broadcast_to(gc_all[g, :, h:h + 1], (ck, LANE)))
    beta = stack(lambda g, h: jnp.broadcast_to(beta_all[g, :, GDN_HEADS + h:GDN_HEADS + h + 1], (ck, LANE)))

    q = q * lax.rsqrt(jnp.sum(q * q, axis=-1, keepdims=True) + EPS) * (GDN_DK ** -0.5)
    k = k * lax.rsqrt(jnp.sum(k * k, axis=-1, keepdims=True) + EPS)
    kb = k * beta
    vb = v * beta
    yield

    kbf = k.astype(BF16)
    kk = _dot_nt(kb.astype(BF16), kbf)
    qk = _dot_nt(q.astype(BF16), kbf)
    yield
    gc_lanes = jnp.transpose(gc)[0:SUBLANE, :]
    gc_rows = jnp.concatenate([gc] * (sr // LANE), axis=1) if sr > LANE else gc[:, 0:sr]
    seg = gc_rows - jnp.concatenate([gc_lanes] * (sr // SUBLANE), axis=0)
    decay = jnp.exp(jnp.where(causal, seg, MASKED_LOG))
    lmat = jnp.where(strict, kk * decay, 0.0)
    qk = (qk * decay).astype(BF16)

    base = min(ck, GDN_INV_BASE)
    ld = jnp.where(inner, lmat, 0.0)
    x = -ld
    p = _dot_split(ld, ld)
    yield
    x = x + p + _dot(x.astype(BF16), p.astype(BF16))
    n = 4
    while n < base:
        pb = p.astype(BF16)
        p = _dot(pb, pb)
        yield
        x = x + p + _dot(x.astype(BF16), p.astype(BF16))
        yield
        n *= 2
    for join in joins:
        c = jnp.where(join, lmat, 0.0)
        xb = x.astype(BF16)
        ca = c + _dot(c.astype(BF16), xb)
        yield
        x = x - (ca + _dot(xb, ca.astype(BF16)))
        yield
    tmat = (eye + x).astype(BF16)
    egc = jnp.exp(gc)
    uw = _dot(tmat, jnp.concatenate([vb, kb * egc], axis=1).astype(BF16))
    yield
    u = uw[:, 0:GDN_DV]
    w = uw[:, GDN_DV:GDN_DV + GDN_DK]
    qg = q * egc
    g_last = jnp.concatenate(
        [jnp.broadcast_to(gc[(n + 1) * ck - 1:(n + 1) * ck, :], (ck, LANE)) for n in range(nblk)], axis=0)
    k_dec = (k * jnp.exp(g_last - gc)).astype(BF16)

    v_new, qs = [], []
    for n, (g, h) in enumerate(blocks):
        rs = slice(n * ck, (n + 1) * ck)
        sb = s_ref[g, h].astype(BF16)
        r = _dot(jnp.concatenate([w[rs], qg[rs]], axis=0).astype(BF16), sb)
        v_new.append(u[rs] - r[0:ck])
        qs.append(r[ck:2 * ck])
    yield
    v_new = jnp.concatenate(v_new, axis=0)
    vnb = v_new.astype(BF16)
    o = jnp.concatenate(qs, axis=0) + _dot(qk, vnb)
    for n, (g, h) in enumerate(blocks):
        rs = slice(n * ck, (n + 1) * ck)
        s_ref[g, h] = s_ref[g, h] * jnp.exp(gc[(n + 1) * ck - 1:(n + 1) * ck, :]) + _dot_tn(k_dec[rs], vnb[rs])
    yield

    o = _rms(o, nw) * (zs * jax.nn.sigmoid(zs))
    for n, (g, h) in enumerate(blocks):
        o_ref[g, :, h * GDN_DV:(h + 1) * GDN_DV] = o[n * ck:(n + 1) * ck].astype(o_ref.dtype)


def _gdn_kernel(q_ref, k_ref, v_ref, z_ref, ab_ref, conv0_ref, s0_ref, cw_ref, alog_ref, dtb_ref, nw_ref,
                *rest, ck, nseq, stack_rows, own_layer):
    o_ref, sout_ref, xb_ref, s_ref, gsum_ref = rest[-5:]
    c = pl.program_id(1)
    ntile = QKV_WIDTH // LANE
    per_src = GDN_QK // LANE

    @pl.when(c == 0)
    def _():
        for lt in range(ntile):
            xb_ref[:, lt, 0:SUBLANE, :] = conv0_ref[:, :, lt * LANE:(lt + 1) * LANE]
        s_ref[...] = s0_ref[...]

    cw = cw_ref[...]
    first = SUBLANE - (CONV_W - 1)
    xc = []
    for lt in range(ntile):
        src = (q_ref, k_ref, v_ref)[lt // per_src]
        xb_ref[:, lt, SUBLANE:SUBLANE + ck, :] = src[:, :, (lt % per_src) * LANE:(lt % per_src + 1) * LANE]
        acc = xb_ref[:, lt, first:first + ck, :] * cw[0:1, lt * LANE:(lt + 1) * LANE]
        for j in range(1, CONV_W):
            acc = acc + xb_ref[:, lt, first + j:first + j + ck, :] * cw[j:j + 1, lt * LANE:(lt + 1) * LANE]
        xb_ref[:, lt, 0:SUBLANE, :] = xb_ref[:, lt, ck:ck + SUBLANE, :]
        xc.append(acc * jax.nn.sigmoid(acc))

    ab = ab_ref[...]
    g_all = -jnp.exp(alog_ref[...]) * jax.nn.softplus(ab + dtb_ref[...])
    gsum_ref[:, 0:ck, :] = jnp.zeros_like(g_all)
    gc_all = g_all
    d = 1
    while d < ck:
        gsum_ref[:, ck:2 * ck, :] = gc_all
        gc_all = gc_all + gsum_ref[:, ck - d:2 * ck - d, :]
        d *= 2
    beta_all = jax.nn.sigmoid(ab)
    z = z_ref[...]
    nw = nw_ref[...]
    masks = _gdn_masks(stack_rows, ck)
    blocks = [(g, h) for g in range(nseq) for h in range(GDN_HEADS)]
    per = stack_rows // ck
    stacks = [_gdn_stack(blocks[b0:b0 + per], ck, masks, xc, gc_all, beta_all, z, nw, s_ref, o_ref)
              for b0 in range(0, len(blocks), per)]
    while stacks:
        stacks = [s for s in stacks if next(s, True) is None]

    @pl.when(c == pl.num_programs(1) - 1)
    def _():
        if own_layer is None:
            sout_ref[...] = s_ref[...]
        else:
            sout_ref[...] = jnp.zeros_like(sout_ref)
            sout_ref[own_layer] = s_ref[...]


def _gdn(proj3, ab3, conv0, s_all, s_done, w, layer, *, ck, nseq, stack_rows, odtype):
    nb, nt, _ = proj3.shape
    assert (nseq * GDN_HEADS * ck) % stack_rows == 0 and stack_rows % ck == 0
    qblk = COL_QKV // GDN_QK
    state_block = (nseq, GDN_HEADS, GDN_DK, GDN_DV)
    state_spec = _lspec(layer, state_block, lambda i, c: (i, 0, 0, 0))
    head_row = _lspec(layer, (1, LANE), _zeros_map(2))
    operands = [proj3, proj3, proj3, proj3, ab3, conv0, s_all, w["gd_cw"], w["gd_alog"], w["gd_dtb"], w["gd_nw"]]
    extra_specs, aliases = [], {}
    if s_done is not None:
        aliases = {len(operands): 1}
        operands.append(s_done)
        extra_specs = [pl.BlockSpec(memory_space=pl.ANY)]
        out_state_spec, own_layer = state_spec, None
    else:
        out_state_spec = pl.BlockSpec((s_all.shape[0],) + state_block, lambda i, c: (0, i, 0, 0, 0))
        own_layer = layer
    return pl.pallas_call(
        functools.partial(_gdn_kernel, ck=ck, nseq=nseq, stack_rows=stack_rows, own_layer=own_layer),
        grid=(nb // nseq, nt // ck),
        input_output_aliases=aliases,
        in_specs=[
            pl.BlockSpec((nseq, ck, GDN_QK), lambda i, c: (i, c, qblk)),
            pl.BlockSpec((nseq, ck, GDN_QK), lambda i, c: (i, c, qblk + 1)),
            pl.BlockSpec((nseq, ck, GDN_V), lambda i, c: (i, c, qblk + 2)),
            pl.BlockSpec((nseq, ck, GDN_V), lambda i, c: (i, c, COL_Z // GDN_V)),
            pl.BlockSpec((nseq, ck, N_AB), lambda i, c: (i, c, 0)),
            _lspec(layer, (nseq, SUBLANE, QKV_WIDTH), lambda i, c: (i, 0, 0)),
            state_spec,
            _lspec(layer, (CONV_W, QKV_WIDTH), _zeros_map(2)),
            head_row,
            head_row,
            head_row,
        ] + extra_specs,
        out_specs=[
            pl.BlockSpec((nseq, ck, GDN_V), lambda i, c: (i, c, 0)),
            out_state_spec,
        ],
        out_shape=[
            jax.ShapeDtypeStruct((nb, nt, GDN_V), odtype),
            jax.ShapeDtypeStruct(s_all.shape, F32),
        ],
        scratch_shapes=[
            pltpu.VMEM((nseq, QKV_WIDTH // LANE, ck + SUBLANE, LANE), F32),
            pltpu.VMEM((nseq, GDN_HEADS, GDN_DK, GDN_DV), F32),
            pltpu.VMEM((nseq, 2 * ck, N_AB), F32),
        ],
        compiler_params=_cparams(("parallel", "arbitrary")),
        name="gdn",
    )(*operands)


def _merge_body(ins, cs, outs):
    x_ref, yrg_ref, ys5_ref, ygd_ref = ins
    gpre_ref, wgate_ref, wrg_ref, ws5_ref, wgd_ref, wo_ref, gpost_ref = cs
    (o_ref,) = outs
    x = x_ref[...]
    h = _rms(x, gpre_ref[...]).astype(BF16)
    mixed = None
    for n, (y_ref, w_ref) in enumerate(((yrg_ref, wrg_ref), (ys5_ref, ws5_ref), (ygd_ref, wgd_ref))):
        gate = jax.nn.sigmoid(_dot(h, wgate_ref[:, n * D_MODEL:(n + 1) * D_MODEL]))
        term = gate * _dot(y_ref[...].astype(BF16), w_ref[...])
        mixed = term if mixed is None else mixed + term
    out = _dot(mixed.astype(BF16), wo_ref[...])
    o_ref[...] = x + _rms(out, gpost_ref[...])


def _merge(groups, w, layer):
    half = D_MODEL // 2
    row = _lspec(layer, (1, D_MODEL), _zeros_map(2))
    branch_out = _resident(layer, (half, D_MODEL))
    tm = min([MERGE_TM] + [g[0].shape[0] for g in groups])
    outs = _row_tiled_call(
        _merge_body, groups,
        [w["g_pre_mix"], w["w_gates"], w["rg_wo"], w["s5_wo"], w["gd_wo"], w["w_out"], w["g_post_mix"]],
        [row, _resident(layer, (D_MODEL, N_BRANCH * D_MODEL)), branch_out, branch_out, branch_out,
         _resident(layer, (D_MODEL, D_MODEL)), row],
        [(D_MODEL, F32)], tm, "merge")
    return [o[0] for o in outs]


def _ffn_body(ins, cs, outs):
    (x_ref,), (gpre_ref, wg_ref, wu_ref, wd_ref, gpost_ref), (o_ref,) = ins, cs, outs
    x = x_ref[...]
    h = _rms(x, gpre_ref[...]).astype(BF16)
    acc = None
    for f0 in range(0, D_FF, FFN_TF):
        gate = _dot(h, wg_ref[:, f0:f0 + FFN_TF])
        act = (gate * jax.nn.sigmoid(gate)) * _dot(h, wu_ref[:, f0:f0 + FFN_TF])
        part = _dot(act.astype(BF16), wd_ref[f0:f0 + FFN_TF, :])
        acc = part if acc is None else acc + part
    o_ref[...] = x + _rms(acc, gpost_ref[...])


def _ffn(xs, w, layer):
    row = _lspec(layer, (1, D_MODEL), _zeros_map(2))
    tm = min([FFN_TM] + [x.shape[0] for x in xs])
    outs = _row_tiled_call(
        _ffn_body, [[x] for x in xs],
        [w["g_pre_ffn"], w["ffn_wg"], w["ffn_wu"], w["ffn_wd"], w["g_post_ffn"]],
        [row, _resident(layer, (D_MODEL, D_FF)), _resident(layer, (D_MODEL, D_FF)),
         _resident(layer, (D_FF, D_MODEL)), row],
        [(D_MODEL, F32)], tm, "ffn")
    return [o[0] for o in outs]


def _block_diag(blocks):
    n, r, c = blocks.shape[-3:]
    eye = jnp.eye(n, dtype=blocks.dtype)
    out = jnp.einsum("...nrc,nm->...nrmc", blocks, eye)
    return out.reshape(blocks.shape[:-3] + (n * r, n * c))


def _prep_params(p):
    f32 = F32
    depth = p["w_in"].shape[0]
    w_in = p["w_in"]
    sizes = (RG_WIDTH, RG_WIDTH, S5_WIDTH, QKV_WIDTH, GDN_V, GDN_HEADS, GDN_HEADS, 3 * D_MODEL)
    offs = [0]
    for s in sizes:
        offs.append(offs[-1] + s)
    assert offs[5] == N_MAIN
    w_main = w_in.astype(BF16)
    w_gates = w_main[:, :, offs[7]:offs[8]]
    w_ab = jnp.concatenate(
        [w_main[:, :, offs[5]:offs[7]], jnp.zeros((depth, D_MODEL, N_AB - 2 * GDN_HEADS), BF16)], axis=2)

    pair = LANE // RG_BLOCK
    pairs = lambda a: _block_diag(a.reshape(depth, RG_WIDTH // LANE, pair, RG_BLOCK, RG_BLOCK))
    wg = jnp.concatenate([pairs(p["rg_wa"]), pairs(p["rg_wx"])], axis=-1).astype(BF16)

    a_re = p["s5_a_re"].astype(f32)
    a_im = p["s5_a_im"].astype(f32)
    dt = jnp.exp(p["s5_log_dt"].astype(f32))[..., None]
    mag = jnp.exp(a_re * dt)
    lr = mag * jnp.cos(a_im * dt)
    li = mag * jnp.sin(a_im * dt)
    den = a_re * a_re + a_im * a_im
    cr = (((lr - 1.0) * a_re + li * a_im) / den)[..., None]
    ci = ((li * a_re - (lr - 1.0) * a_im) / den)[..., None]
    b_re = p["s5_b_re"].astype(f32)
    b_im = p["s5_b_im"].astype(f32)
    bb_re = cr * b_re - ci * b_im
    bb_im = cr * b_im + ci * b_re
    nq = S5_WIDTH // LANE
    gq = S5_GROUPS // nq
    to_in = lambda a: _block_diag(jnp.swapaxes(a, 2, 3).reshape(depth, nq, gq, S5_GROUP, S5_STATE))
    to_out = lambda a: _block_diag(jnp.swapaxes(a.astype(f32), 2, 3).reshape(depth, nq, gq, S5_STATE, S5_GROUP))

    row = lambda a: a.astype(f32).reshape(depth, 1, -1)
    pad4 = lambda a: jnp.pad(a.astype(f32), ((0, 0), (0, LANE - GDN_HEADS))).reshape(depth, 1, LANE)
    return dict(
        g_pre_mix=row(p["norm_pre_mix"]), g_post_mix=row(p["norm_post_mix"]),
        g_pre_ffn=row(p["norm_pre_ffn"]), g_post_ffn=row(p["norm_post_ffn"]),
        w_main=w_main, w_ab=w_ab, w_gates=w_gates,
        rg_cw=p["rg_conv_w"].astype(f32), rg_cb=row(p["rg_conv_b"]), rg_wg=wg,
        rg_ba=row(p["rg_ba"]), rg_bx=row(p["rg_bx"]), rg_lam=row(p["rg_lambda"]),
        rg_wo=p["rg_w_out"].astype(BF16),
        s5_lr=lr.reshape(depth, 1, S5_FLAT), s5_li=li.reshape(depth, 1, S5_FLAT),
        s5_bre=to_in(bb_re).astype(BF16), s5_bim=to_in(bb_im).astype(BF16),
        s5_cre=to_out(p["s5_c_re"]).astype(BF16), s5_cim=to_out(p["s5_c_im"]).astype(BF16),
        s5_d=row(p["s5_d"]), s5_gw=p["s5_glu_w"].astype(BF16), s5_gb=row(p["s5_glu_b"]),
        s5_wo=p["s5_w_out"].astype(BF16),
        gd_cw=p["gdn_conv_w"].astype(f32), gd_alog=pad4(p["gdn_a_log"]), gd_dtb=pad4(p["gdn_dt_bias"]),
        gd_nw=row(p["gdn_norm_w"]), gd_wo=p["gdn_w_out"].astype(BF16),
        w_out=p["w_out"].astype(BF16),
        ffn_wg=p["ffn_w_gate"].astype(BF16), ffn_wu=p["ffn_w_up"].astype(BF16),
        ffn_wd=p["ffn_w_down"].astype(BF16),
    )


def _pad_conv_state(s):
    return jnp.pad(s, ((0, 0), (0, 0), (SUBLANE - (CONV_W - 1), 0), (0, 0)))


def _run_groups(xs, states, w, cfgs):
    keep = CONV_W - 1
    depth = states[0][1].shape[0]
    prepared = []
    for x, st, cfg in zip(xs, states, cfgs):
        nb, nt, _ = x.shape
        assert nt >= keep and nt % SUBLANE == 0 and nb % SUBLANE == 0
        st_rg_conv, st_rg_h, st_s5_re, st_s5_im, st_gd_conv, st_gd_s = st
        ydtype = BF16 if (cfg["tc"] % BF16_ROWS == 0 and cfg["ck"] % BF16_ROWS == 0) else F32
        prepared.append(dict(
            nb=nb, nt=nt, ydtype=ydtype, rg_conv0=_pad_conv_state(st_rg_conv), rg_h0=st_rg_h,
            s5_re0=st_s5_re.reshape(depth, nb, S5_FLAT), s5_im0=st_s5_im.reshape(depth, nb, S5_FLAT),
            gd_conv0=_pad_conv_state(st_gd_conv), gd_s0=st_gd_s, s_gd=None, outs=[[] for _ in range(5)]))
    xfs = [x.reshape(-1, D_MODEL) for x in xs]
    for l in range(depth):
        projs = _in_proj(xfs, w, l)
        merge_in = []
        for xf, (proj, ab), g, cfg in zip(xfs, projs, prepared, cfgs):
            nb, nt = g["nb"], g["nt"]
            flat = lambda a: a.reshape(nb * nt, a.shape[-1])
            proj3 = proj.reshape(nb, nt, N_MAIN)
            y_rg, h_last = _rglru(proj3, g["rg_conv0"], g["rg_h0"], w, l, tc=cfg["tc"], ydtype=g["ydtype"])
            y_s5, s_re, s_im = _s5(proj3, g["s5_re0"], g["s5_im0"], w, l, tc=cfg["tc"],
                                   tiles_per_pass=cfg["tiles_per_pass"], ydtype=g["ydtype"])
            y_gd, g["s_gd"] = _gdn(proj3, ab.reshape(nb, nt, N_AB), g["gd_conv0"], g["gd_s0"], g["s_gd"], w, l,
                                   ck=cfg["ck"], nseq=cfg["gdn_nseq"], stack_rows=cfg["gdn_stack_rows"],
                                   odtype=g["ydtype"])
            merge_in.append([xf, flat(y_rg), flat(y_s5), flat(y_gd)])
            g["outs"][0].append(proj3[:, nt - keep:, COL_RG:COL_RG + RG_WIDTH])
            g["outs"][1].append(h_last)
            g["outs"][2].append(s_re.reshape(nb, S5_GROUPS, S5_STATE))
            g["outs"][3].append(s_im.reshape(nb, S5_GROUPS, S5_STATE))
            g["outs"][4].append(proj3[:, nt - keep:, COL_QKV:COL_QKV + QKV_WIDTH])
        xfs = _ffn(_merge(merge_in, w, l), w, l)
    return [(xf.reshape(g["nb"], g["nt"], D_MODEL), [jnp.stack(o) for o in g["outs"]] + [g["s_gd"]])
            for xf, g in zip(xfs, prepared)]


def _zero_states(depth, nb):
    return (jnp.zeros((depth, nb, CONV_W - 1, RG_WIDTH), F32),
            jnp.zeros((depth, nb, RG_WIDTH), F32),
            jnp.zeros((depth, nb, S5_GROUPS, S5_STATE), F32),
            jnp.zeros((depth, nb, S5_GROUPS, S5_STATE), F32),
            jnp.zeros((depth, nb, CONV_W - 1, QKV_WIDTH), F32),
            jnp.zeros((depth, nb, GDN_HEADS, GDN_DK, GDN_DV), F32))


def kernel(x_prompt, x_sample, state_rg_conv, state_rg_h, state_s5_re, state_s5_im, state_gdn_conv, state_gdn_s, norm_pre_mix, norm_post_mix, norm_pre_ffn, norm_post_ffn, w_in, rg_conv_w, rg_conv_b, rg_wa, rg_ba, rg_wx, rg_bx, rg_lambda, rg_w_out, s5_a_re, s5_a_im, s5_b_re, s5_b_im, s5_c_re, s5_c_im, s5_d, s5_log_dt, s5_glu_w, s5_glu_b, s5_w_out, gdn_conv_w, gdn_a_log, gdn_dt_bias, gdn_norm_w, gdn_w_out, w_out, ffn_w_gate, ffn_w_up, ffn_w_down):
    p = dict(norm_pre_mix=norm_pre_mix, norm_post_mix=norm_post_mix, norm_pre_ffn=norm_pre_ffn,
             norm_post_ffn=norm_post_ffn, w_in=w_in, rg_conv_w=rg_conv_w, rg_conv_b=rg_conv_b, rg_wa=rg_wa,
             rg_ba=rg_ba, rg_wx=rg_wx, rg_bx=rg_bx, rg_lambda=rg_lambda, rg_w_out=rg_w_out, s5_a_re=s5_a_re,
             s5_a_im=s5_a_im, s5_b_re=s5_b_re, s5_b_im=s5_b_im, s5_c_re=s5_c_re, s5_c_im=s5_c_im, s5_d=s5_d,
             s5_log_dt=s5_log_dt, s5_glu_w=s5_glu_w, s5_glu_b=s5_glu_b, s5_w_out=s5_w_out,
             gdn_conv_w=gdn_conv_w, gdn_a_log=gdn_a_log, gdn_dt_bias=gdn_dt_bias, gdn_norm_w=gdn_norm_w,
             gdn_w_out=gdn_w_out, w_out=w_out, ffn_w_gate=ffn_w_gate, ffn_w_up=ffn_w_up, ffn_w_down=ffn_w_down)
    depth = w_in.shape[0]
    w = _prep_params(p)
    sample_t = x_sample.shape[1]
    (y_p, st_p), (y_s, st_s) = _run_groups(
        [x_prompt, x_sample],
        [_zero_states(depth, x_prompt.shape[0]),
         (state_rg_conv, state_rg_h, state_s5_re, state_s5_im, state_gdn_conv, state_gdn_s)],
        w,
        [dict(tc=128, ck=64, gdn_nseq=8, gdn_stack_rows=MXU_EDGE, tiles_per_pass=8),
         dict(tc=sample_t, ck=sample_t, gdn_nseq=16, gdn_stack_rows=MXU_EDGE, tiles_per_pass=1)])
    return (y_p, y_s, *st_p, *st_s)
```

```python
import functools

import jax
import jax.numpy as jnp
from jax import lax
from jax.experimental import pallas as pl
from jax.experimental.pallas import tpu as pltpu

F32 = jnp.float32
BF16 = jnp.bfloat16

D_MODEL = 1024
CONV_W = 4
EPS = 1e-6
RG_WIDTH = 512
RG_BLOCK = 64
RG_C = 8.0
S5_WIDTH = 512
S5_GROUP = 16
S5_GROUPS = 32
S5_STATE = 64
S5_FLAT = S5_GROUPS * S5_STATE
GDN_HEADS = 4
GDN_DK = 128
GDN_DV = 128
GDN_QK = 512
GDN_V = 512
QKV_WIDTH = 1536
D_FF = 2816

LANE = 128
SUBLANE = 8
BF16_ROWS = 16
MXU_EDGE = 256
MASKED_LOG = -1e30
GDN_INV_BASE = 8

COL_RG = 0
COL_S5 = 1024
COL_QKV = 1536
COL_Z = 3072
N_MAIN = 3584
N_BRANCH = 3
N_AB = LANE

VMEM_LIMIT = 52 * 1024 * 1024

IN_PROJ_TM = 512
MERGE_TM = 512
FFN_TM, FFN_TF = 512, 256


def _cparams(sem):
    return pltpu.CompilerParams(dimension_semantics=sem, vmem_limit_bytes=VMEM_LIMIT)


def _lspec(layer, block, index_map):
    return pl.BlockSpec((None,) + tuple(block), lambda *g: (layer,) + tuple(index_map(*g)))


def _zeros_map(n):
    return lambda *g: (0,) * n


def _rms(x, g):
    return x * lax.rsqrt(jnp.mean(x * x, axis=-1, keepdims=True) + EPS) * g


def _dot(a, b):
    return jnp.dot(a, b, preferred_element_type=F32)


def _dot_nt(a, b):
    return lax.dot_general(a, b, (((1,), (1,)), ((), ())), preferred_element_type=F32)


def _dot_tn(a, b):
    return lax.dot_general(a, b, (((0,), (0,)), ((), ())), preferred_element_type=F32)


def _step_pitch(nt_chunk):
    return nt_chunk if nt_chunk <= SUBLANE else nt_chunk + SUBLANE // 2


def _put_rows(ref, tile, val, nb, tc, pitch):
    if pitch == tc:
        ref[tile, 0:nb * tc, :] = val
    else:
        for b in range(nb):
            ref[tile, b * pitch:b * pitch + tc, :] = val[b * tc:(b + 1) * tc, :]


def _get_rows(ref, tile, nb, tc, pitch):
    if pitch == tc:
        return ref[tile, 0:nb * tc, :]
    return jnp.concatenate([ref[tile, b * pitch:b * pitch + tc, :] for b in range(nb)], axis=0)


def _row_tiled_call(body, groups, consts, const_specs, out_cols, tm, name):
    n_tiles = [g[0].shape[0] // tm for g in groups]
    assert all(a.shape[0] == n * tm for g, n in zip(groups, n_tiles) for a in g)
    starts = [sum(n_tiles[:g]) for g in range(len(groups))]
    n_in, n_out, n_const = len(groups[0]), len(out_cols), len(consts)

    def spec(width, g):
        return pl.BlockSpec((tm, width), lambda i, lo=starts[g], n=n_tiles[g]: (jnp.clip(i - lo, 0, n - 1), 0))

    def kernel(*refs):
        ins, cs, outs = refs[:len(groups) * n_in], refs[len(groups) * n_in:][:n_const], refs[-len(groups) * n_out:]
        i = pl.program_id(0)
        for g in range(len(groups)):
            @pl.when((i >= starts[g]) & (i < starts[g] + n_tiles[g]))
            def _(g=g):
                body(ins[g * n_in:(g + 1) * n_in], cs, outs[g * n_out:(g + 1) * n_out])

    outs = pl.pallas_call(
        kernel,
        grid=(sum(n_tiles),),
        in_specs=[spec(a.shape[1], g) for g, arrs in enumerate(groups) for a in arrs] + list(const_specs),
        out_specs=[spec(wd, g) for g in range(len(groups)) for wd, _ in out_cols],
        out_shape=[jax.ShapeDtypeStruct((groups[g][0].shape[0], wd), dt)
                   for g in range(len(groups)) for wd, dt in out_cols],
        compiler_params=_cparams(("arbitrary",)),
        name=name,
    )(*[a for arrs in groups for a in arrs], *consts)
    return [outs[g * n_out:(g + 1) * n_out] for g in range(len(groups))]


def _resident(layer, shape):
    return pl.BlockSpec((None,) + tuple(shape), lambda i: (layer,) + (0,) * len(shape), pipeline_mode=pl.Buffered(1))


def _in_proj_body(ins, cs, outs):
    (x_ref,), (g_ref, w_ref, wab_ref), (o_ref, oab_ref) = ins, cs, outs
    h = _rms(x_ref[...], g_ref[...]).astype(BF16)
    oab_ref[...] = _dot(h, wab_ref[...])
    o_ref[...] = _dot(h, w_ref[...])


def _in_proj(xs, w, layer):
    tm = min([IN_PROJ_TM] + [x.shape[0] for x in xs])
    return _row_tiled_call(
        _in_proj_body, [[x] for x in xs], [w["g_pre_mix"], w["w_main"], w["w_ab"]],
        [_lspec(layer, (1, D_MODEL), _zeros_map(2)), _resident(layer, (D_MODEL, N_MAIN)),
         _resident(layer, (D_MODEL, N_AB))],
        [(N_MAIN, F32), (N_AB, F32)], tm, "in_proj")


def _rglru_kernel(p_ref, conv0_ref, h0_ref, cw_ref, cb_ref, wg_ref, ba_ref, bx_ref, lam_ref,
                  y_ref, hl_ref, xp_ref, a_ref, h_ref, hc_ref, *, nb, tc, pitch):
    ntile = RG_WIDTH // LANE
    first = SUBLANE - (CONV_W - 1)

    @pl.when(pl.program_id(0) == 0)
    def _():
        for p in range(ntile):
            xp_ref[:, p, 0:SUBLANE, :] = conv0_ref[:, :, p * LANE:(p + 1) * LANE]
        hc_ref[...] = h0_ref[...]

    cw = cw_ref[...]
    for p in range(ntile):
        cols = slice(p * LANE, (p + 1) * LANE)
        xp_ref[:, p, SUBLANE:SUBLANE + tc, :] = p_ref[:, :, cols]
        xc = cb_ref[:, cols] + xp_ref[:, p, first:first + tc, :] * cw[0:1, cols]
        for k in range(1, CONV_W):
            xc = xc + xp_ref[:, p, first + k:first + k + tc, :] * cw[k:k + 1, cols]
        xp_ref[:, p, 0:SUBLANE, :] = xp_ref[:, p, tc:tc + SUBLANE, :]
        xc = xc.reshape(nb * tc, LANE)
        gm = _dot(xc.astype(BF16), wg_ref[p])
        r = jax.nn.sigmoid(gm[:, 0:LANE] + ba_ref[:, cols])
        i = jax.nn.sigmoid(gm[:, LANE:2 * LANE] + bx_ref[:, cols])
        log_a = -RG_C * r * jax.nn.softplus(-lam_ref[:, cols])
        a = jnp.exp(log_a)
        _put_rows(a_ref, p, a, nb, tc, pitch)
        _put_rows(h_ref, p, jnp.sqrt(1.0 - a * a) * (i * xc), nb, tc, pitch)

    def step(t, hs):
        rs = pl.ds(t, nb, stride=pitch)
        out = []
        for p in range(ntile):
            h = a_ref[p, rs, :] * hs[p] + h_ref[p, rs, :]
            h_ref[p, rs, :] = h
            out.append(h)
        return tuple(out)

    hc = hc_ref[...]
    hs = lax.fori_loop(0, tc, step, tuple(hc[:, p * LANE:(p + 1) * LANE] for p in range(ntile)), unroll=2)
    for p in range(ntile):
        cols = slice(p * LANE, (p + 1) * LANE)
        hc_ref[:, cols] = hs[p]
        hl_ref[:, cols] = hs[p]
        h = _get_rows(h_ref, p, nb, tc, pitch).reshape(nb, tc, LANE)
        gate = p_ref[:, :, RG_WIDTH + p * LANE:RG_WIDTH + (p + 1) * LANE]
        y_ref[:, :, cols] = (h * jax.nn.gelu(gate)).astype(y_ref.dtype)


def _rglru(proj3, conv0, h0, w, layer, *, tc, ydtype):
    nb, nt, _ = proj3.shape
    pitch = _step_pitch(tc)
    ntile = RG_WIDTH // LANE
    row = _lspec(layer, (1, RG_WIDTH), _zeros_map(2))
    return pl.pallas_call(
        functools.partial(_rglru_kernel, nb=nb, tc=tc, pitch=pitch),
        grid=(nt // tc,),
        in_specs=[
            pl.BlockSpec((nb, tc, 2 * RG_WIDTH), lambda c: (0, c, COL_RG // (2 * RG_WIDTH))),
            _lspec(layer, (nb, SUBLANE, RG_WIDTH), _zeros_map(3)),
            _lspec(layer, (nb, RG_WIDTH), _zeros_map(2)),
            _lspec(layer, (CONV_W, RG_WIDTH), _zeros_map(2)),
            row,
            _lspec(layer, (ntile, LANE, 2 * LANE), _zeros_map(3)),
            row,
            row,
            row,
        ],
        out_specs=[
            pl.BlockSpec((nb, tc, RG_WIDTH), lambda c: (0, c, 0)),
            pl.BlockSpec((nb, RG_WIDTH), _zeros_map(2)),
        ],
        out_shape=[jax.ShapeDtypeStruct((nb, nt, RG_WIDTH), ydtype), jax.ShapeDtypeStruct((nb, RG_WIDTH), F32)],
        scratch_shapes=[
            pltpu.VMEM((nb, ntile, tc + SUBLANE, LANE), F32),
            pltpu.VMEM((ntile, nb * pitch, LANE), F32),
            pltpu.VMEM((ntile, nb * pitch, LANE), F32),
            pltpu.VMEM((nb, RG_WIDTH), F32),
        ],
        compiler_params=_cparams(("arbitrary",)),
        name="rglru",
    )(proj3, conv0, h0, w["rg_cw"], w["rg_cb"], w["rg_wg"], w["rg_ba"], w["rg_bx"], w["rg_lam"])


def _s5_kernel(u_ref, s0r_ref, s0i_ref, lr_ref, li_ref, bre_ref, bim_ref, cre_ref, cim_ref, d_ref, gw_ref, gb_ref,
               y_ref, sr_out, si_out, sre_ref, sim_ref, src_ref, sic_ref, *, nb, tc, pitch, tiles_per_pass):
    nq = S5_WIDTH // LANE
    qt = S5_FLAT // nq // LANE
    ntile = S5_FLAT // LANE

    @pl.when(pl.program_id(0) == 0)
    def _():
        src_ref[...] = s0r_ref[...]
        sic_ref[...] = s0i_ref[...]

    u = u_ref[...].reshape(nb * tc, S5_WIDTH)
    ub = u.astype(BF16)
    for q in range(nq):
        uq = ub[:, q * LANE:(q + 1) * LANE]
        bur = _dot(uq, bre_ref[q])
        bui = _dot(uq, bim_ref[q])
        for j in range(qt):
            _put_rows(sre_ref, q * qt + j, bur[:, j * LANE:(j + 1) * LANE], nb, tc, pitch)
            _put_rows(sim_ref, q * qt + j, bui[:, j * LANE:(j + 1) * LANE], nb, tc, pitch)

    for t0 in range(0, ntile, tiles_per_pass):
        tiles = list(range(t0, t0 + tiles_per_pass))
        lrs = [jnp.broadcast_to(lr_ref[:, tl * LANE:(tl + 1) * LANE], (nb, LANE)) for tl in tiles]
        lis = [jnp.broadcast_to(li_ref[:, tl * LANE:(tl + 1) * LANE], (nb, LANE)) for tl in tiles]

        def step(t, carry, tiles=tiles, lrs=lrs, lis=lis):
            rs = pl.ds(t, nb, stride=pitch)
            out = []
            for n, tl in enumerate(tiles):
                sr, si = carry[2 * n], carry[2 * n + 1]
                nr = lrs[n] * sr - lis[n] * si + sre_ref[tl, rs, :]
                ni = lrs[n] * si + lis[n] * sr + sim_ref[tl, rs, :]
                sre_ref[tl, rs, :] = nr
                sim_ref[tl, rs, :] = ni
                out += [nr, ni]
            return tuple(out)

        init = []
        for tl in tiles:
            init += [src_ref[:, tl * LANE:(tl + 1) * LANE], sic_ref[:, tl * LANE:(tl + 1) * LANE]]
        fin = lax.fori_loop(0, tc, step, tuple(init), unroll=2)
        for n, tl in enumerate(tiles):
            src_ref[:, tl * LANE:(tl + 1) * LANE] = fin[2 * n]
            sic_ref[:, tl * LANE:(tl + 1) * LANE] = fin[2 * n + 1]

    sr_out[...] = src_ref[...]
    si_out[...] = sic_ref[...]

    ys = []
    for q in range(nq):
        sr_q = jnp.concatenate([_get_rows(sre_ref, q * qt + j, nb, tc, pitch) for j in range(qt)], axis=1)
        si_q = jnp.concatenate([_get_rows(sim_ref, q * qt + j, nb, tc, pitch) for j in range(qt)], axis=1)
        ys.append(_dot(sr_q.astype(BF16), cre_ref[q]) - _dot(si_q.astype(BF16), cim_ref[q]))
    y = jnp.concatenate(ys, axis=1) + d_ref[...] * u
    y = jax.nn.gelu(y)
    y = y * jax.nn.sigmoid(_dot(y.astype(BF16), gw_ref[...]) + gb_ref[...])
    y_ref[...] = y.reshape(nb, tc, S5_WIDTH).astype(y_ref.dtype)


def _s5(proj3, s0r, s0i, w, layer, *, tc, tiles_per_pass, ydtype):
    nb, nt, _ = proj3.shape
    pitch = _step_pitch(tc)
    nq = S5_WIDTH // LANE
    qs = S5_FLAT // nq
    ntile = S5_FLAT // LANE
    state = _lspec(layer, (nb, S5_FLAT), _zeros_map(2))
    lam = _lspec(layer, (1, S5_FLAT), _zeros_map(2))
    b_in = _lspec(layer, (nq, LANE, qs), _zeros_map(3))
    c_out = _lspec(layer, (nq, qs, LANE), _zeros_map(3))
    row = _lspec(layer, (1, S5_WIDTH), _zeros_map(2))
    return pl.pallas_call(
        functools.partial(_s5_kernel, nb=nb, tc=tc, pitch=pitch, tiles_per_pass=tiles_per_pass),
        grid=(nt // tc,),
        in_specs=[
            pl.BlockSpec((nb, tc, S5_WIDTH), lambda c: (0, c, COL_S5 // S5_WIDTH)),
            state, state, lam, lam, b_in, b_in, c_out, c_out, row,
            _lspec(layer, (S5_WIDTH, S5_WIDTH), _zeros_map(2)),
            row,
        ],
        out_specs=[
            pl.BlockSpec((nb, tc, S5_WIDTH), lambda c: (0, c, 0)),
            pl.BlockSpec((nb, S5_FLAT), _zeros_map(2)),
            pl.BlockSpec((nb, S5_FLAT), _zeros_map(2)),
        ],
        out_shape=[
            jax.ShapeDtypeStruct((nb, nt, S5_WIDTH), ydtype),
            jax.ShapeDtypeStruct((nb, S5_FLAT), F32),
            jax.ShapeDtypeStruct((nb, S5_FLAT), F32),
        ],
        scratch_shapes=[
            pltpu.VMEM((ntile, nb * pitch, LANE), F32),
            pltpu.VMEM((ntile, nb * pitch, LANE), F32),
            pltpu.VMEM((nb, S5_FLAT), F32),
            pltpu.VMEM((nb, S5_FLAT), F32),
        ],
        compiler_params=_cparams(("arbitrary",)),
        name="s5",
    )(proj3, s0r, s0i, w["s5_lr"], w["s5_li"], w["s5_bre"], w["s5_bim"], w["s5_cre"], w["s5_cim"], w["s5_d"],
      w["s5_gw"], w["s5_gb"])


def _gdn_masks(sr, ck):
    shift = ck.bit_length() - 1
    row = lax.broadcasted_iota(jnp.int32, (sr, sr), 0)
    col = lax.broadcasted_iota(jnp.int32, (sr, sr), 1)
    same = (row >> shift) == (col >> shift)
    causal = same & (row >= col)
    strict = same & (row > col)
    eye = jnp.where(row == col, 1.0, 0.0).astype(F32)
    within = lambda m: (row >> (m.bit_length() - 1)) == (col >> (m.bit_length() - 1))
    base = min(ck, GDN_INV_BASE)
    inner = within(base)
    joins = []
    m = base
    while m < ck:
        joins.append(within(2 * m) & jnp.logical_not(within(m)))
        m *= 2
    return causal, strict, eye, inner, joins


def _gdn_stack(blocks, ck, masks, xc, gc_all, beta_all, z, nw, s_ref, o_ref):
    nblk = len(blocks)
    sr = nblk * ck
    causal, strict, eye, inner, joins = masks

    def stack(fn):
        return jnp.concatenate([fn(g, h) for g, h in blocks], axis=0)

    q = stack(lambda g, h: xc[h][g])
    k = stack(lambda g, h: xc[GDN_HEADS + h][g])
    v = stack(lambda g, h: xc[2 * GDN_HEADS + h][g])
    zs = stack(lambda g, h: z[g, :, h * GDN_DV:(h + 1) * GDN_DV])
    gc = stack(lambda g, h: jnp.broadcast_to(gc_all[g, :, h:h + 1], (ck, LANE)))
    beta = stack(lambda g, h: jnp.broadcast_to(beta_all[g, :, GDN_HEADS + h:GDN_HEADS + h + 1], (ck, LANE)))

    q = q * lax.rsqrt(jnp.sum(q * q, axis=-1, keepdims=True) + EPS) * (GDN_DK ** -0.5)
    k = k * lax.rsqrt(jnp.sum(k * k, axis=-1, keepdims=True) + EPS)
    kb = k * beta
    vb = v * beta
    yield

    kbf = k.astype(BF16)
    kk = _dot_nt(kb.astype(BF16), kbf)
    qk = _dot_nt(q.astype(BF16), kbf)
    yield
    gc_lanes = jnp.transpose(gc)[0:SUBLANE, :]
    gc_rows = jnp.concatenate([gc] * (sr // LANE), axis=1) if sr > LANE else gc[:, 0:sr]
    seg = gc_rows - jnp.concatenate([gc_lanes] * (sr // SUBLANE), axis=0)
    decay = jnp.exp(jnp.where(causal, seg, MASKED_LOG))
    lmat = jnp.where(strict, kk * decay, 0.0)
    qk = (qk * decay).astype(BF16)

    base = min(ck, GDN_INV_BASE)
    ld = jnp.where(inner, lmat, 0.0)
    x = -ld
    ldb = ld.astype(BF16)
    p = _dot(ldb, ldb)
    yield
    x = x + p + _dot(x.astype(BF16), p.astype(BF16))
    n = 4
    while n < base:
        pb = p.astype(BF16)
        p = _dot(pb, pb)
        yield
        x = x + p + _dot(x.astype(BF16), p.astype(BF16))
        yield
        n *= 2
    for join in joins:
        c = jnp.where(join, lmat, 0.0)
        xb = x.astype(BF16)
        ca = c + _dot(c.astype(BF16), xb)
        yield
        x = x - (ca + _dot(xb, ca.astype(BF16)))
        yield
    tmat = (eye + x).astype(BF16)
    egc = jnp.exp(gc)
    uw = _dot(tmat, jnp.concatenate([vb, kb * egc], axis=1).astype(BF16))
    yield
    u = uw[:, 0:GDN_DV]
    w = uw[:, GDN_DV:GDN_DV + GDN_DK]
    qg = q * egc
    g_last = jnp.concatenate(
        [jnp.broadcast_to(gc[(n + 1) * ck - 1:(n + 1) * ck, :], (ck, LANE)) for n in range(nblk)], axis=0)
    k_dec = (k * jnp.exp(g_last - gc)).astype(BF16)

    v_new, qs = [], []
    for n, (g, h) in enumerate(blocks):
        rs = slice(n * ck, (n + 1) * ck)
        sb = s_ref[g, h].astype(BF16)
        r = _dot(jnp.concatenate([w[rs], qg[rs]], axis=0).astype(BF16), sb)
        v_new.append(u[rs] - r[0:ck])
        qs.append(r[ck:2 * ck])
    yield
    v_new = jnp.concatenate(v_new, axis=0)
    vnb = v_new.astype(BF16)
    o = jnp.concatenate(qs, axis=0) + _dot(qk, vnb)
    for n, (g, h) in enumerate(blocks):
        rs = slice(n * ck, (n + 1) * ck)
        s_ref[g, h] = s_ref[g, h] * jnp.exp(gc[(n + 1) * ck - 1:(n + 1) * ck, :]) + _dot_tn(k_dec[rs], vnb[rs])
    yield

    o = _rms(o, nw) * (zs * jax.nn.sigmoid(zs))
    for n, (g, h) in enumerate(blocks):
        o_ref[g, :, h * GDN_DV:(h + 1) * GDN_DV] = o[n * ck:(n + 1) * ck].astype(o_ref.dtype)


def _gdn_kernel(q_ref, k_ref, v_ref, z_ref, ab_ref, conv0_ref, s0_ref, cw_ref, alog_ref, dtb_ref, nw_ref,
                *rest, ck, nseq, stack_rows, own_layer):
    o_ref, sout_ref, xb_ref, s_ref, gsum_ref = rest[-5:]
    c = pl.program_id(1)
    ntile = QKV_WIDTH // LANE
    per_src = GDN_QK // LANE

    @pl.when(c == 0)
    def _():
        for lt in range(ntile):
            xb_ref[:, lt, 0:SUBLANE, :] = conv0_ref[:, :, lt * LANE:(lt + 1) * LANE]
        s_ref[...] = s0_ref[...]

    cw = cw_ref[...]
    first = SUBLANE - (CONV_W - 1)
    xc = []
    for lt in range(ntile):
        src = (q_ref, k_ref, v_ref)[lt // per_src]
        xb_ref[:, lt, SUBLANE:SUBLANE + ck, :] = src[:, :, (lt % per_src) * LANE:(lt % per_src + 1) * LANE]
        acc = xb_ref[:, lt, first:first + ck, :] * cw[0:1, lt * LANE:(lt + 1) * LANE]
        for j in range(1, CONV_W):
            acc = acc + xb_ref[:, lt, first + j:first + j + ck, :] * cw[j:j + 1, lt * LANE:(lt + 1) * LANE]
        xb_ref[:, lt, 0:SUBLANE, :] = xb_ref[:, lt, ck:ck + SUBLANE, :]
        xc.append(acc * jax.nn.sigmoid(acc))

    ab = ab_ref[...]
    g_all = -jnp.exp(alog_ref[...]) * jax.nn.softplus(ab + dtb_ref[...])
    gsum_ref[:, 0:ck, :] = jnp.zeros_like(g_all)
    gc_all = g_all
    d = 1
    while d < ck:
        gsum_ref[:, ck:2 * ck, :] = gc_all
        gc_all = gc_all + gsum_ref[:, ck - d:2 * ck - d, :]
        d *= 2
    beta_all = jax.nn.sigmoid(ab)
    z = z_ref[...]
    nw = nw_ref[...]
    masks = _gdn_masks(stack_rows, ck)
    blocks = [(g, h) for g in range(nseq) for h in range(GDN_HEADS)]
    per = stack_rows // ck
    stacks = [_gdn_stack(blocks[b0:b0 + per], ck, masks, xc, gc_all, beta_all, z, nw, s_ref, o_ref)
              for b0 in range(0, len(blocks), per)]
    while stacks:
        stacks = [s for s in stacks if next(s, True) is None]

    @pl.when(c == pl.num_programs(1) - 1)
    def _():
        if own_layer is None:
            sout_ref[...] = s_ref[...]
        else:
            sout_ref[...] = jnp.zeros_like(sout_ref)
            sout_ref[own_layer] = s_ref[...]


def _gdn(proj3, ab3, conv0, s_all, s_done, w, layer, *, ck, nseq, stack_rows, odtype):
    nb, nt, _ = proj3.shape
    assert (nseq * GDN_HEADS * ck) % stack_rows == 0 and stack_rows % ck == 0
    qblk = COL_QKV // GDN_QK
    state_block = (nseq, GDN_HEADS, GDN_DK, GDN_DV)
    state_spec = _lspec(layer, state_block, lambda i, c: (i, 0, 0, 0))
    head_row = _lspec(layer, (1, LANE), _zeros_map(2))
    operands = [proj3, proj3, proj3, proj3, ab3, conv0, s_all, w["gd_cw"], w["gd_alog"], w["gd_dtb"], w["gd_nw"]]
    extra_specs, aliases = [], {}
    if s_done is not None:
        aliases = {len(operands): 1}
        operands.append(s_done)
        extra_specs = [pl.BlockSpec(memory_space=pl.ANY)]
        out_state_spec, own_layer = state_spec, None
    else:
        out_state_spec = pl.BlockSpec((s_all.shape[0],) + state_block, lambda i, c: (0, i, 0, 0, 0))
        own_layer = layer
    return pl.pallas_call(
        functools.partial(_gdn_kernel, ck=ck, nseq=nseq, stack_rows=stack_rows, own_layer=own_layer),
        grid=(nb // nseq, nt // ck),
        input_output_aliases=aliases,
        in_specs=[
            pl.BlockSpec((nseq, ck, GDN_QK), lambda i, c: (i, c, qblk)),
            pl.BlockSpec((nseq, ck, GDN_QK), lambda i, c: (i, c, qblk + 1)),
            pl.BlockSpec((nseq, ck, GDN_V), lambda i, c: (i, c, qblk + 2)),
            pl.BlockSpec((nseq, ck, GDN_V), lambda i, c: (i, c, COL_Z // GDN_V)),
            pl.BlockSpec((nseq, ck, N_AB), lambda i, c: (i, c, 0)),
            _lspec(layer, (nseq, SUBLANE, QKV_WIDTH), lambda i, c: (i, 0, 0)),
            state_spec,
            _lspec(layer, (CONV_W, QKV_WIDTH), _zeros_map(2)),
            head_row,
            head_row,
            head_row,
        ] + extra_specs,
        out_specs=[
            pl.BlockSpec((nseq, ck, GDN_V), lambda i, c: (i, c, 0)),
            out_state_spec,
        ],
        out_shape=[
            jax.ShapeDtypeStruct((nb, nt, GDN_V), odtype),
            jax.ShapeDtypeStruct(s_all.shape, F32),
        ],
        scratch_shapes=[
            pltpu.VMEM((nseq, QKV_WIDTH // LANE, ck + SUBLANE, LANE), F32),
            pltpu.VMEM((nseq, GDN_HEADS, GDN_DK, GDN_DV), F32),
            pltpu.VMEM((nseq, 2 * ck, N_AB), F32),
        ],
        compiler_params=_cparams(("parallel", "arbitrary")),
        name="gdn",
    )(*operands)


def _merge_body(ins, cs, outs):
    x_ref, yrg_ref, ys5_ref, ygd_ref = ins
    gpre_ref, wgate_ref, wrg_ref, ws5_ref, wgd_ref, wo_ref, gpost_ref = cs
    (o_ref,) = outs
    x = x_ref[...]
    h = _rms(x, gpre_ref[...]).astype(BF16)
    mixed = None
    for n, (y_ref, w_ref) in enumerate(((yrg_ref, wrg_ref), (ys5_ref, ws5_ref), (ygd_ref, wgd_ref))):
        gate = jax.nn.sigmoid(_dot(h, wgate_ref[:, n * D_MODEL:(n + 1) * D_MODEL]))
        term = gate * _dot(y_ref[...].astype(BF16), w_ref[...])
        mixed = term if mixed is None else mixed + term
    out = _dot(mixed.astype(BF16), wo_ref[...])
    o_ref[...] = x + _rms(out, gpost_ref[...])


def _merge(groups, w, layer):
    half = D_MODEL // 2
    row = _lspec(layer, (1, D_MODEL), _zeros_map(2))
    branch_out = _resident(layer, (half, D_MODEL))
    tm = min([MERGE_TM] + [g[0].shape[0] for g in groups])
    outs = _row_tiled_call(
        _merge_body, groups,
        [w["g_pre_mix"], w["w_gates"], w["rg_wo"], w["s5_wo"], w["gd_wo"], w["w_out"], w["g_post_mix"]],
        [row, _resident(layer, (D_MODEL, N_BRANCH * D_MODEL)), branch_out, branch_out, branch_out,
         _resident(layer, (D_MODEL, D_MODEL)), row],
        [(D_MODEL, F32)], tm, "merge")
    return [o[0] for o in outs]


def _ffn_body(ins, cs, outs):
    (x_ref,), (gpre_ref, wg_ref, wu_ref, wd_ref, gpost_ref), (o_ref,) = ins, cs, outs
    x = x_ref[...]
    h = _rms(x, gpre_ref[...]).astype(BF16)
    acc = None
    for f0 in range(0, D_FF, FFN_TF):
        gate = _dot(h, wg_ref[:, f0:f0 + FFN_TF])
        act = (gate * jax.nn.sigmoid(gate)) * _dot(h, wu_ref[:, f0:f0 + FFN_TF])
        part = _dot(act.astype(BF16), wd_ref[f0:f0 + FFN_TF, :])
        acc = part if acc is None else acc + part
    o_ref[...] = x + _rms(acc, gpost_ref[...])


def _ffn(xs, w, layer):
    row = _lspec(layer, (1, D_MODEL), _zeros_map(2))
    tm = min([FFN_TM] + [x.shape[0] for x in xs])
    outs = _row_tiled_call(
        _ffn_body, [[x] for x in xs],
        [w["g_pre_ffn"], w["ffn_wg"], w["ffn_wu"], w["ffn_wd"], w["g_post_ffn"]],
        [row, _resident(layer, (D_MODEL, D_FF)), _resident(layer, (D_MODEL, D_FF)),
         _resident(layer, (D_FF, D_MODEL)), row],
        [(D_MODEL, F32)], tm, "ffn")
    return [o[0] for o in outs]


def _block_diag(blocks):
    n, r, c = blocks.shape[-3:]
    eye = jnp.eye(n, dtype=blocks.dtype)
    out = jnp.einsum("...nrc,nm->...nrmc", blocks, eye)
    return out.reshape(blocks.shape[:-3] + (n * r, n * c))


def _prep_params(p):
    f32 = F32
    depth = p["w_in"].shape[0]
    w_in = p["w_in"]
    sizes = (RG_WIDTH, RG_WIDTH, S5_WIDTH, QKV_WIDTH, GDN_V, GDN_HEADS, GDN_HEADS, 3 * D_MODEL)
    offs = [0]
    for s in sizes:
        offs.append(offs[-1] + s)
    assert offs[5] == N_MAIN
    w_main = w_in.astype(BF16)
    w_gates = w_main[:, :, offs[7]:offs[8]]
    w_ab = jnp.concatenate(
        [w_main[:, :, offs[5]:offs[7]], jnp.zeros((depth, D_MODEL, N_AB - 2 * GDN_HEADS), BF16)], axis=2)

    pair = LANE // RG_BLOCK
    pairs = lambda a: _block_diag(a.reshape(depth, RG_WIDTH // LANE, pair, RG_BLOCK, RG_BLOCK))
    wg = jnp.concatenate([pairs(p["rg_wa"]), pairs(p["rg_wx"])], axis=-1).astype(BF16)

    a_re = p["s5_a_re"].astype(f32)
    a_im = p["s5_a_im"].astype(f32)
    dt = jnp.exp(p["s5_log_dt"].astype(f32))[..., None]
    mag = jnp.exp(a_re * dt)
    lr = mag * jnp.cos(a_im * dt)
    li = mag * jnp.sin(a_im * dt)
    den = a_re * a_re + a_im * a_im
    cr = (((lr - 1.0) * a_re + li * a_im) / den)[..., None]
    ci = ((li * a_re - (lr - 1.0) * a_im) / den)[..., None]
    b_re = p["s5_b_re"].astype(f32)
    b_im = p["s5_b_im"].astype(f32)
    bb_re = cr * b_re - ci * b_im
    bb_im = cr * b_im + ci * b_re
    nq = S5_WIDTH // LANE
    gq = S5_GROUPS // nq
    to_in = lambda a: _block_diag(jnp.swapaxes(a, 2, 3).reshape(depth, nq, gq, S5_GROUP, S5_STATE))
    to_out = lambda a: _block_diag(jnp.swapaxes(a.astype(f32), 2, 3).reshape(depth, nq, gq, S5_STATE, S5_GROUP))

    row = lambda a: a.astype(f32).reshape(depth, 1, -1)
    pad4 = lambda a: jnp.pad(a.astype(f32), ((0, 0), (0, LANE - GDN_HEADS))).reshape(depth, 1, LANE)
    return dict(
        g_pre_mix=row(p["norm_pre_mix"]), g_post_mix=row(p["norm_post_mix"]),
        g_pre_ffn=row(p["norm_pre_ffn"]), g_post_ffn=row(p["norm_post_ffn"]),
        w_main=w_main, w_ab=w_ab, w_gates=w_gates,
        rg_cw=p["rg_conv_w"].astype(f32), rg_cb=row(p["rg_conv_b"]), rg_wg=wg,
        rg_ba=row(p["rg_ba"]), rg_bx=row(p["rg_bx"]), rg_lam=row(p["rg_lambda"]),
        rg_wo=p["rg_w_out"].astype(BF16),
        s5_lr=lr.reshape(depth, 1, S5_FLAT), s5_li=li.reshape(depth, 1, S5_FLAT),
        s5_bre=to_in(bb_re).astype(BF16), s5_bim=to_in(bb_im).astype(BF16),
        s5_cre=to_out(p["s5_c_re"]).astype(BF16), s5_cim=to_out(p["s5_c_im"]).astype(BF16),
        s5_d=row(p["s5_d"]), s5_gw=p["s5_glu_w"].astype(BF16), s5_gb=row(p["s5_glu_b"]),
        s5_wo=p["s5_w_out"].astype(BF16),
        gd_cw=p["gdn_conv_w"].astype(f32), gd_alog=pad4(p["gdn_a_log"]), gd_dtb=pad4(p["gdn_dt_bias"]),
        gd_nw=row(p["gdn_norm_w"]), gd_wo=p["gdn_w_out"].astype(BF16),
        w_out=p["w_out"].astype(BF16),
        ffn_wg=p["ffn_w_gate"].astype(BF16), ffn_wu=p["ffn_w_up"].astype(BF16),
        ffn_wd=p["ffn_w_down"].astype(BF16),
    )


def _pad_conv_state(s):
    return jnp.pad(s, ((0, 0), (0, 0), (SUBLANE - (CONV_W - 1), 0), (0, 0)))


def _run_groups(xs, states, w, cfgs):
    keep = CONV_W - 1
    depth = states[0][1].shape[0]
    prepared = []
    for x, st, cfg in zip(xs, states, cfgs):
        nb, nt, _ = x.shape
        assert nt >= keep and nt % SUBLANE == 0 and nb % SUBLANE == 0
        st_rg_conv, st_rg_h, st_s5_re, st_s5_im, st_gd_conv, st_gd_s = st
        ydtype = BF16 if (cfg["tc"] % BF16_ROWS == 0 and cfg["ck"] % BF16_ROWS == 0) else F32
        prepared.append(dict(
            nb=nb, nt=nt, ydtype=ydtype, rg_conv0=_pad_conv_state(st_rg_conv), rg_h0=st_rg_h,
            s5_re0=st_s5_re.reshape(depth, nb, S5_FLAT), s5_im0=st_s5_im.reshape(depth, nb, S5_FLAT),
            gd_conv0=_pad_conv_state(st_gd_conv), gd_s0=st_gd_s, s_gd=None, outs=[[] for _ in range(5)]))
    xfs = [x.reshape(-1, D_MODEL) for x in xs]
    for l in range(depth):
        projs = _in_proj(xfs, w, l)
        merge_in = []
        for xf, (proj, ab), g, cfg in zip(xfs, projs, prepared, cfgs):
            nb, nt = g["nb"], g["nt"]
            flat = lambda a: a.reshape(nb * nt, a.shape[-1])
            proj3 = proj.reshape(nb, nt, N_MAIN)
            y_rg, h_last = _rglru(proj3, g["rg_conv0"], g["rg_h0"], w, l, tc=cfg["tc"], ydtype=g["ydtype"])
            y_s5, s_re, s_im = _s5(proj3, g["s5_re0"], g["s5_im0"], w, l, tc=cfg["tc"],
                                   tiles_per_pass=cfg["tiles_per_pass"], ydtype=g["ydtype"])
            y_gd, g["s_gd"] = _gdn(proj3, ab.reshape(nb, nt, N_AB), g["gd_conv0"], g["gd_s0"], g["s_gd"], w, l,
                                   ck=cfg["ck"], nseq=cfg["gdn_nseq"], stack_rows=cfg["gdn_stack_rows"],
                                   odtype=g["ydtype"])
            merge_in.append([xf, flat(y_rg), flat(y_s5), flat(y_gd)])
            g["outs"][0].append(proj3[:, nt - keep:, COL_RG:COL_RG + RG_WIDTH])
            g["outs"][1].append(h_last)
            g["outs"][2].append(s_re.reshape(nb, S5_GROUPS, S5_STATE))
            g["outs"][3].append(s_im.reshape(nb, S5_GROUPS, S5_STATE))
            g["outs"][4].append(proj3[:, nt - keep:, COL_QKV:COL_QKV + QKV_WIDTH])
        xfs = _ffn(_merge(merge_in, w, l), w, l)
    return [(xf.reshape(g["nb"], g["nt"], D_MODEL), [jnp.stack(o) for o in g["outs"]] + [g["s_gd"]])
            for xf, g in zip(xfs, prepared)]


def _zero_states(depth, nb):
    return (jnp.zeros((depth, nb, CONV_W - 1, RG_WIDTH), F32),
            jnp.zeros((depth, nb, RG_WIDTH), F32),
            jnp.zeros((depth, nb, S5_GROUPS, S5_STATE), F32),
            jnp.zeros((depth, nb, S5_GROUPS, S5_STATE), F32),
            jnp.zeros((depth, nb, CONV_W - 1, QKV_WIDTH), F32),
            jnp.zeros((depth, nb, GDN_HEADS, GDN_DK, GDN_DV), F32))


def kernel(x_prompt, x_sample, state_rg_conv, state_rg_h, state_s5_re, state_s5_im, state_gdn_conv, state_gdn_s, norm_pre_mix, norm_post_mix, norm_pre_ffn, norm_post_ffn, w_in, rg_conv_w, rg_conv_b, rg_wa, rg_ba, rg_wx, rg_bx, rg_lambda, rg_w_out, s5_a_re, s5_a_im, s5_b_re, s5_b_im, s5_c_re, s5_c_im, s5_d, s5_log_dt, s5_glu_w, s5_glu_b, s5_w_out, gdn_conv_w, gdn_a_log, gdn_dt_bias, gdn_norm_w, gdn_w_out, w_out, ffn_w_gate, ffn_w_up, ffn_w_down):
    p = dict(norm_pre_mix=norm_pre_mix, norm_post_mix=norm_post_mix, norm_pre_ffn=norm_pre_ffn,
             norm_post_ffn=norm_post_ffn, w_in=w_in, rg_conv_w=rg_conv_w, rg_conv_b=rg_conv_b, rg_wa=rg_wa,
             rg_ba=rg_ba, rg_wx=rg_wx, rg_bx=rg_bx, rg_lambda=rg_lambda, rg_w_out=rg_w_out, s5_a_re=s5_a_re,
             s5_a_im=s5_a_im, s5_b_re=s5_b_re, s5_b_im=s5_b_im, s5_c_re=s5_c_re, s5_c_im=s5_c_im, s5_d=s5_d,
             s5_log_dt=s5_log_dt, s5_glu_w=s5_glu_w, s5_glu_b=s5_glu_b, s5_w_out=s5_w_out,
             gdn_conv_w=gdn_conv_w, gdn_a_log=gdn_a_log, gdn_dt_bias=gdn_dt_bias, gdn_norm_w=gdn_norm_w,
             gdn_w_out=gdn_w_out, w_out=w_out, ffn_w_gate=ffn_w_gate, ffn_w_up=ffn_w_up, ffn_w_down=ffn_w_down)
    depth = w_in.shape[0]
    w = _prep_params(p)
    sample_t = x_sample.shape[1]
    (y_p, st_p), (y_s, st_s) = _run_groups(
        [x_prompt, x_sample],
        [_zero_states(depth, x_prompt.shape[0]),
         (state_rg_conv, state_rg_h, state_s5_re, state_s5_im, state_gdn_conv, state_gdn_s)],
        w,
        [dict(tc=128, ck=64, gdn_nseq=8, gdn_stack_rows=MXU_EDGE, tiles_per_pass=8),
         dict(tc=sample_t, ck=sample_t, gdn_nseq=16, gdn_stack_rows=MXU_EDGE, tiles_per_pass=1)])
    return (y_p, y_s, *st_p, *st_s)
```

```python
import functools

import jax
import jax.numpy as jnp
from jax import lax
from jax.experimental import pallas as pl
from jax.experimental.pallas import tpu as pltpu

F32 = jnp.float32
BF16 = jnp.bfloat16

D_MODEL = 1024
CONV_W = 4
EPS = 1e-6
RG_WIDTH = 512
RG_BLOCK = 64
RG_C = 8.0
S5_WIDTH = 512
S5_GROUP = 16
S5_GROUPS = 32
S5_STATE = 64
S5_FLAT = S5_GROUPS * S5_STATE
GDN_HEADS = 4
GDN_DK = 128
GDN_DV = 128
GDN_QK = 512
GDN_V = 512
QKV_WIDTH = 1536
D_FF = 2816

LANE = 128
SUBLANE = 8
BF16_ROWS = 16
MXU_EDGE = 256
MASKED_LOG = -1e30
GDN_INV_BASE = 8

COL_RG = 0
COL_S5 = 1024
COL_QKV = 1536
COL_Z = 3072
N_MAIN = 3584
CONV_TILES = (RG_WIDTH + QKV_WIDTH) // LANE
N_BRANCH = 3
N_AB = LANE
N_PROJ = N_MAIN + N_AB

VMEM_LIMIT = 52 * 1024 * 1024

IN_PROJ_TM, IN_PROJ_SUBTILES = 512, 4
MERGE_TM = 512
FFN_TM, FFN_TF = 512, 256


def _cparams(sem):
    return pltpu.CompilerParams(dimension_semantics=sem, vmem_limit_bytes=VMEM_LIMIT)


def _lspec(layer, block, index_map):
    return pl.BlockSpec((None,) + tuple(block), lambda *g: (layer,) + tuple(index_map(*g)))


def _zeros_map(n):
    return lambda *g: (0,) * n


def _rms(x, g):
    return x * lax.rsqrt(jnp.mean(x * x, axis=-1, keepdims=True) + EPS) * g


def _dot(a, b):
    return jnp.dot(a, b, preferred_element_type=F32)


def _dot_nt(a, b):
    return lax.dot_general(a, b, (((1,), (1,)), ((), ())), preferred_element_type=F32)


def _dot_tn(a, b):
    return lax.dot_general(a, b, (((0,), (0,)), ((), ())), preferred_element_type=F32)


def _step_pitch(nt_chunk):
    return nt_chunk if nt_chunk <= SUBLANE else nt_chunk + SUBLANE // 2


def _put_rows(ref, tile, val, nb, tc, pitch):
    if pitch == tc:
        ref[tile, 0:nb * tc, :] = val
    else:
        for b in range(nb):
            ref[tile, b * pitch:b * pitch + tc, :] = val[b * tc:(b + 1) * tc, :]


def _get_rows(ref, tile, nb, tc, pitch):
    if pitch == tc:
        return ref[tile, 0:nb * tc, :]
    return jnp.concatenate([ref[tile, b * pitch:b * pitch + tc, :] for b in range(nb)], axis=0)


def _row_tiled_call(body, groups, consts, const_specs, out_cols, tm, name):
    n_tiles = [g[0].shape[0] // tm for g in groups]
    assert all(a.shape[0] == n * tm for g, n in zip(groups, n_tiles) for a in g)
    starts = [sum(n_tiles[:g]) for g in range(len(groups))]
    n_in, n_out, n_const = len(groups[0]), len(out_cols), len(consts)

    def spec(width, g):
        return pl.BlockSpec((tm, width), lambda i, lo=starts[g], n=n_tiles[g]: (jnp.clip(i - lo, 0, n - 1), 0))

    def kernel(*refs):
        ins, cs, outs = refs[:len(groups) * n_in], refs[len(groups) * n_in:][:n_const], refs[-len(groups) * n_out:]
        i = pl.program_id(0)
        for g in range(len(groups)):
            @pl.when((i >= starts[g]) & (i < starts[g] + n_tiles[g]))
            def _(g=g):
                body(ins[g * n_in:(g + 1) * n_in], cs, outs[g * n_out:(g + 1) * n_out])

    outs = pl.pallas_call(
        kernel,
        grid=(sum(n_tiles),),
        in_specs=[spec(a.shape[1], g) for g, arrs in enumerate(groups) for a in arrs] + list(const_specs),
        out_specs=[spec(wd, g) for g in range(len(groups)) for wd, _ in out_cols],
        out_shape=[jax.ShapeDtypeStruct((groups[g][0].shape[0], wd), dt)
                   for g in range(len(groups)) for wd, dt in out_cols],
        compiler_params=_cparams(("arbitrary",)),
        name=name,
    )(*[a for arrs in groups for a in arrs], *consts)
    return [outs[g * n_out:(g + 1) * n_out] for g in range(len(groups))]


def _resident(layer, shape):
    return pl.BlockSpec((None,) + tuple(shape), lambda i: (layer,) + (0,) * len(shape), pipeline_mode=pl.Buffered(1))


def _in_proj_kernel(*refs, starts, n_tiles, conv_tiles, tm):
    ng = len(starts)
    xs = refs[0:ng]
    g_ref, w_ref, rg_cw_ref, rg_cb_ref, gd_cw_ref = refs[ng:ng + 5]
    outs = refs[ng + 5:ng + 5 + 2 * ng]
    cbuf_ref = refs[-1]
    i = pl.program_id(0)
    first = SUBLANE - (CONV_W - 1)
    rg_t = RG_WIDTH // LANE

    def plain(x_ref, o_ref, tail_ref):
        h = _rms(x_ref[...], g_ref[...]).astype(BF16)
        o_ref[...] = _dot(h, w_ref[...])
        tail_ref[...] = jnp.zeros_like(tail_ref)

    def fused(x_ref, o_ref, tail_ref, lo, per_seq):
        @pl.when((i - lo) % per_seq == 0)
        def _():
            cbuf_ref[:, 0:SUBLANE, :] = jnp.zeros((CONV_TILES, SUBLANE, LANE), F32)

        rg_cw, gd_cw = rg_cw_ref[...], gd_cw_ref[...]
        sub = tm // IN_PROJ_SUBTILES
        for r0 in range(0, tm, sub):
            rows = slice(r0, r0 + sub)
            h = _rms(x_ref[rows, :], g_ref[...]).astype(BF16)
            raw = _dot(h, w_ref[...])
            o_ref[rows, RG_WIDTH:COL_QKV] = raw[:, RG_WIDTH:COL_QKV]
            o_ref[rows, COL_Z:N_PROJ] = raw[:, COL_Z:N_PROJ]
            for lt in range(CONV_TILES):
                is_rg = lt < rg_t
                col0 = COL_RG + lt * LANE if is_rg else COL_QKV + (lt - rg_t) * LANE
                wcols = slice(lt * LANE, (lt + 1) * LANE) if is_rg else slice((lt - rg_t) * LANE, (lt - rg_t + 1) * LANE)
                cw = rg_cw if is_rg else gd_cw
                cbuf_ref[lt, SUBLANE + r0:SUBLANE + r0 + sub, :] = raw[:, col0:col0 + LANE]
                acc = cbuf_ref[lt, first + r0:first + r0 + sub, :] * cw[0:1, wcols]
                for j in range(1, CONV_W):
                    acc = acc + cbuf_ref[lt, first + r0 + j:first + r0 + j + sub, :] * cw[j:j + 1, wcols]
                if is_rg:
                    o_ref[rows, col0:col0 + LANE] = acc + rg_cb_ref[:, wcols]
                else:
                    o_ref[rows, col0:col0 + LANE] = acc * jax.nn.sigmoid(acc)
        for lt in range(CONV_TILES):
            keep = cbuf_ref[lt, tm:tm + SUBLANE, :]
            tail_ref[:, lt * LANE:(lt + 1) * LANE] = keep
            cbuf_ref[lt, 0:SUBLANE, :] = keep

    for g in range(ng):
        @pl.when((i >= starts[g]) & (i < starts[g] + n_tiles[g]))
        def _(g=g):
            o_ref, tail_ref = outs[2 * g:2 * g + 2]
            if conv_tiles[g]:
                fused(xs[g], o_ref, tail_ref, starts[g], conv_tiles[g])
            else:
                plain(xs[g], o_ref, tail_ref)


def _in_proj(xs, seq_lens, w, layer):
    tm = min([IN_PROJ_TM] + [x.shape[0] for x in xs])
    n_tiles = [x.shape[0] // tm for x in xs]
    assert all(x.shape[0] == n * tm for x, n in zip(xs, n_tiles))
    starts = [sum(n_tiles[:g]) for g in range(len(xs))]
    conv_tiles = [nt // tm if (zero_state and nt % tm == 0) else 0 for nt, zero_state in seq_lens]

    def spec(rows, width, g):
        return pl.BlockSpec((rows, width), lambda i, lo=starts[g], n=n_tiles[g]: (jnp.clip(i - lo, 0, n - 1), 0))

    ng = len(xs)
    outs = pl.pallas_call(
        functools.partial(_in_proj_kernel, starts=starts, n_tiles=n_tiles, conv_tiles=conv_tiles, tm=tm),
        grid=(sum(n_tiles),),
        in_specs=[spec(tm, D_MODEL, g) for g in range(ng)] + [
            _lspec(layer, (1, D_MODEL), _zeros_map(2)), _resident(layer, (D_MODEL, N_PROJ)),
            _lspec(layer, (CONV_W, RG_WIDTH), _zeros_map(2)),
            _lspec(layer, (1, RG_WIDTH), _zeros_map(2)), _lspec(layer, (CONV_W, QKV_WIDTH), _zeros_map(2))],
        out_specs=[s for g in range(ng) for s in
                   (spec(tm, N_PROJ, g), spec(SUBLANE, CONV_TILES * LANE, g))],
        out_shape=[s for g in range(ng) for s in
                   (jax.ShapeDtypeStruct((xs[g].shape[0], N_PROJ), F32),
                    jax.ShapeDtypeStruct((n_tiles[g] * SUBLANE, CONV_TILES * LANE), F32))],
        scratch_shapes=[pltpu.VMEM((CONV_TILES, tm + SUBLANE, LANE), F32)],
        compiler_params=_cparams(("arbitrary",)),
        name="in_proj",
    )(*xs, w["g_pre_mix"], w["w_main"], w["rg_cw"], w["rg_cb"], w["gd_cw"])
    return [(outs[2 * g], outs[2 * g + 1], conv_tiles[g]) for g in range(ng)]


def _rglru_kernel(p_ref, conv0_ref, h0_ref, cw_ref, cb_ref, wg_ref, ba_ref, bx_ref, lam_ref,
                  y_ref, hl_ref, xp_ref, a_ref, h_ref, hc_ref, *, nb, tc, pitch, conv_done):
    ntile = RG_WIDTH // LANE
    first = SUBLANE - (CONV_W - 1)

    @pl.when(pl.program_id(0) == 0)
    def _():
        for p in range(ntile):
            xp_ref[:, p, 0:SUBLANE, :] = conv0_ref[:, :, p * LANE:(p + 1) * LANE]
        hc_ref[...] = h0_ref[...]

    cw = cw_ref[...]
    for p in range(ntile):
        cols = slice(p * LANE, (p + 1) * LANE)
        if conv_done:
            xc = p_ref[:, :, cols]
        else:
            xp_ref[:, p, SUBLANE:SUBLANE + tc, :] = p_ref[:, :, cols]
            xc = cb_ref[:, cols] + xp_ref[:, p, first:first + tc, :] * cw[0:1, cols]
            for k in range(1, CONV_W):
                xc = xc + xp_ref[:, p, first + k:first + k + tc, :] * cw[k:k + 1, cols]
            xp_ref[:, p, 0:SUBLANE, :] = xp_ref[:, p, tc:tc + SUBLANE, :]
        xc = xc.reshape(nb * tc, LANE)
        gm = _dot(xc.astype(BF16), wg_ref[p])
        r = jax.nn.sigmoid(gm[:, 0:LANE] + ba_ref[:, cols])
        i = jax.nn.sigmoid(gm[:, LANE:2 * LANE] + bx_ref[:, cols])
        log_a = -RG_C * r * jax.nn.softplus(-lam_ref[:, cols])
        a = jnp.exp(log_a)
        _put_rows(a_ref, p, a, nb, tc, pitch)
        _put_rows(h_ref, p, jnp.sqrt(1.0 - a * a) * (i * xc), nb, tc, pitch)

    def step(t, hs):
        rs = pl.ds(t, nb, stride=pitch)
        out = []
        for p in range(ntile):
            h = a_ref[p, rs, :] * hs[p] + h_ref[p, rs, :]
            h_ref[p, rs, :] = h
            out.append(h)
        return tuple(out)

    hc = hc_ref[...]
    hs = lax.fori_loop(0, tc, step, tuple(hc[:, p * LANE:(p + 1) * LANE] for p in range(ntile)), unroll=2)
    for p in range(ntile):
        cols = slice(p * LANE, (p + 1) * LANE)
        hc_ref[:, cols] = hs[p]
        hl_ref[:, cols] = hs[p]
        h = _get_rows(h_ref, p, nb, tc, pitch).reshape(nb, tc, LANE)
        gate = p_ref[:, :, RG_WIDTH + p * LANE:RG_WIDTH + (p + 1) * LANE]
        y_ref[:, :, cols] = (h * jax.nn.gelu(gate)).astype(y_ref.dtype)


def _rglru(proj3, conv0, h0, w, layer, *, tc, ydtype, conv_done):
    nb, nt, _ = proj3.shape
    pitch = _step_pitch(tc)
    ntile = RG_WIDTH // LANE
    row = _lspec(layer, (1, RG_WIDTH), _zeros_map(2))
    return pl.pallas_call(
        functools.partial(_rglru_kernel, nb=nb, tc=tc, pitch=pitch, conv_done=conv_done),
        grid=(nt // tc,),
        in_specs=[
            pl.BlockSpec((nb, tc, 2 * RG_WIDTH), lambda c: (0, c, COL_RG // (2 * RG_WIDTH))),
            _lspec(layer, (nb, SUBLANE, RG_WIDTH), _zeros_map(3)),
            _lspec(layer, (nb, RG_WIDTH), _zeros_map(2)),
            _lspec(layer, (CONV_W, RG_WIDTH), _zeros_map(2)),
            row,
            _lspec(layer, (ntile, LANE, 2 * LANE), _zeros_map(3)),
            row,
            row,
            row,
        ],
        out_specs=[
            pl.BlockSpec((nb, tc, RG_WIDTH), lambda c: (0, c, 0)),
            pl.BlockSpec((nb, RG_WIDTH), _zeros_map(2)),
        ],
        out_shape=[jax.ShapeDtypeStruct((nb, nt, RG_WIDTH), ydtype), jax.ShapeDtypeStruct((nb, RG_WIDTH), F32)],
        scratch_shapes=[
            pltpu.VMEM((nb, ntile, tc + SUBLANE, LANE), F32),
            pltpu.VMEM((ntile, nb * pitch, LANE), F32),
            pltpu.VMEM((ntile, nb * pitch, LANE), F32),
            pltpu.VMEM((nb, RG_WIDTH), F32),
        ],
        compiler_params=_cparams(("arbitrary",)),
        name="rglru",
    )(proj3, conv0, h0, w["rg_cw"], w["rg_cb"], w["rg_wg"], w["rg_ba"], w["rg_bx"], w["rg_lam"])


def _s5_kernel(u_ref, s0r_ref, s0i_ref, lr_ref, li_ref, bre_ref, bim_ref, cre_ref, cim_ref, d_ref, gw_ref, gb_ref,
               y_ref, sr_out, si_out, sre_ref, sim_ref, src_ref, sic_ref, *, nb, tc, pitch, tiles_per_pass):
    nq = S5_WIDTH // LANE
    qt = S5_FLAT // nq // LANE
    ntile = S5_FLAT // LANE

    @pl.when(pl.program_id(0) == 0)
    def _():
        src_ref[...] = s0r_ref[...]
        sic_ref[...] = s0i_ref[...]

    u = u_ref[...].reshape(nb * tc, S5_WIDTH)
    ub = u.astype(BF16)
    for q in range(nq):
        uq = ub[:, q * LANE:(q + 1) * LANE]
        bur = _dot(uq, bre_ref[q])
        bui = _dot(uq, bim_ref[q])
        for j in range(qt):
            _put_rows(sre_ref, q * qt + j, bur[:, j * LANE:(j + 1) * LANE], nb, tc, pitch)
            _put_rows(sim_ref, q * qt + j, bui[:, j * LANE:(j + 1) * LANE], nb, tc, pitch)

    for t0 in range(0, ntile, tiles_per_pass):
        tiles = list(range(t0, t0 + tiles_per_pass))
        lrs = [jnp.broadcast_to(lr_ref[:, tl * LANE:(tl + 1) * LANE], (nb, LANE)) for tl in tiles]
        lis = [jnp.broadcast_to(li_ref[:, tl * LANE:(tl + 1) * LANE], (nb, LANE)) for tl in tiles]

        def step(t, carry, tiles=tiles, lrs=lrs, lis=lis):
            rs = pl.ds(t, nb, stride=pitch)
            out = []
            for n, tl in enumerate(tiles):
                sr, si = carry[2 * n], carry[2 * n + 1]
                nr = lrs[n] * sr - lis[n] * si + sre_ref[tl, rs, :]
                ni = lrs[n] * si + lis[n] * sr + sim_ref[tl, rs, :]
                sre_ref[tl, rs, :] = nr
                sim_ref[tl, rs, :] = ni
                out += [nr, ni]
            return tuple(out)

        init = []
        for tl in tiles:
            init += [src_ref[:, tl * LANE:(tl + 1) * LANE], sic_ref[:, tl * LANE:(tl + 1) * LANE]]
        fin = lax.fori_loop(0, tc, step, tuple(init), unroll=2)
        for n, tl in enumerate(tiles):
            src_ref[:, tl * LANE:(tl + 1) * LANE] = fin[2 * n]
            sic_ref[:, tl * LANE:(tl + 1) * LANE] = fin[2 * n + 1]

    sr_out[...] = src_ref[...]
    si_out[...] = sic_ref[...]

    ys = []
    for q in range(nq):
        sr_q = jnp.concatenate([_get_rows(sre_ref, q * qt + j, nb, tc, pitch) for j in range(qt)], axis=1)
        si_q = jnp.concatenate([_get_rows(sim_ref, q * qt + j, nb, tc, pitch) for j in range(qt)], axis=1)
        ys.append(_dot(sr_q.astype(BF16), cre_ref[q]) - _dot(si_q.astype(BF16), cim_ref[q]))
    y = jnp.concatenate(ys, axis=1) + d_ref[...] * u
    y = jax.nn.gelu(y)
    y = y * jax.nn.sigmoid(_dot(y.astype(BF16), gw_ref[...]) + gb_ref[...])
    y_ref[...] = y.reshape(nb, tc, S5_WIDTH).astype(y_ref.dtype)


def _s5(proj3, s0r, s0i, w, layer, *, tc, tiles_per_pass, ydtype):
    nb, nt, _ = proj3.shape
    pitch = _step_pitch(tc)
    nq = S5_WIDTH // LANE
    qs = S5_FLAT // nq
    ntile = S5_FLAT // LANE
    state = _lspec(layer, (nb, S5_FLAT), _zeros_map(2))
    lam = _lspec(layer, (1, S5_FLAT), _zeros_map(2))
    b_in = _lspec(layer, (nq, LANE, qs), _zeros_map(3))
    c_out = _lspec(layer, (nq, qs, LANE), _zeros_map(3))
    row = _lspec(layer, (1, S5_WIDTH), _zeros_map(2))
    return pl.pallas_call(
        functools.partial(_s5_kernel, nb=nb, tc=tc, pitch=pitch, tiles_per_pass=tiles_per_pass),
        grid=(nt // tc,),
        in_specs=[
            pl.BlockSpec((nb, tc, S5_WIDTH), lambda c: (0, c, COL_S5 // S5_WIDTH)),
            state, state, lam, lam, b_in, b_in, c_out, c_out, row,
            _lspec(layer, (S5_WIDTH, S5_WIDTH), _zeros_map(2)),
            row,
        ],
        out_specs=[
            pl.BlockSpec((nb, tc, S5_WIDTH), lambda c: (0, c, 0)),
            pl.BlockSpec((nb, S5_FLAT), _zeros_map(2)),
            pl.BlockSpec((nb, S5_FLAT), _zeros_map(2)),
        ],
        out_shape=[
            jax.ShapeDtypeStruct((nb, nt, S5_WIDTH), ydtype),
            jax.ShapeDtypeStruct((nb, S5_FLAT), F32),
            jax.ShapeDtypeStruct((nb, S5_FLAT), F32),
        ],
        scratch_shapes=[
            pltpu.VMEM((ntile, nb * pitch, LANE), F32),
            pltpu.VMEM((ntile, nb * pitch, LANE), F32),
            pltpu.VMEM((nb, S5_FLAT), F32),
            pltpu.VMEM((nb, S5_FLAT), F32),
        ],
        compiler_params=_cparams(("arbitrary",)),
        name="s5",
    )(proj3, s0r, s0i, w["s5_lr"], w["s5_li"], w["s5_bre"], w["s5_bim"], w["s5_cre"], w["s5_cim"], w["s5_d"],
      w["s5_gw"], w["s5_gb"])


def _gdn_masks(sr, ck):
    shift = ck.bit_length() - 1
    row = lax.broadcasted_iota(jnp.int32, (sr, sr), 0)
    col = lax.broadcasted_iota(jnp.int32, (sr, sr), 1)
    same = (row >> shift) == (col >> shift)
    causal = same & (row >= col)
    strict = same & (row > col)
    eye = jnp.where(row == col, 1.0, 0.0).astype(F32)
    within = lambda m: (row >> (m.bit_length() - 1)) == (col >> (m.bit_length() - 1))
    base = min(ck, GDN_INV_BASE)
    inner = within(base)
    joins = []
    m = base
    while m < ck:
        joins.append(within(2 * m) & jnp.logical_not(within(m)))
        m *= 2
    return causal, strict, eye, inner, joins


def _gdn_stack(blocks, ck, masks, xc, gc_all, beta_all, z, nw, s_ref, o_ref):
    nblk = len(blocks)
    sr = nblk * ck
    causal, strict, eye, inner, joins = masks

    def stack(fn):
        return jnp.concatenate([fn(g, h) for g, h in blocks], axis=0)

    q = stack(lambda g, h: xc[h][g])
    k = stack(lambda g, h: xc[GDN_HEADS + h][g])
    v = stack(lambda g, h: xc[2 * GDN_HEADS + h][g])
    zs = stack(lambda g, h: z[g, :, h * GDN_DV:(h + 1) * GDN_DV])
    gc = stack(lambda g, h: jnp.broadcast_to(gc_all[g, :, h:h + 1], (ck, LANE)))
    beta = stack(lambda g, h: jnp.broadcast_to(beta_all[g, :, GDN_HEADS + h:GDN_HEADS + h + 1], (ck, LANE)))

    q = q * lax.rsqrt(jnp.sum(q * q, axis=-1, keepdims=True) + EPS) * (GDN_DK ** -0.5)
    k = k * lax.rsqrt(jnp.sum(k * k, axis=-1, keepdims=True) + EPS)
    kb = k * beta
    vb = v * beta
    yield

    kbf = k.astype(BF16)
    kk = _dot_nt(kb.astype(BF16), kbf)
    qk = _dot_nt(q.astype(BF16), kbf)
    yield
    gc_lanes = jnp.transpose(gc)[0:SUBLANE, :]
    gc_rows = jnp.concatenate([gc] * (sr // LANE), axis=1) if sr > LANE else gc[:, 0:sr]
    seg = gc_rows - jnp.concatenate([gc_lanes] * (sr // SUBLANE), axis=0)
    decay = jnp.exp(jnp.where(causal, seg, MASKED_LOG))
    lmat = jnp.where(strict, kk * decay, 0.0)
    qk = (qk * decay).astype(BF16)

    base = min(ck, GDN_INV_BASE)
    ld = jnp.where(inner, lmat, 0.0)
    x = -ld
    ldb = ld.astype(BF16)
    p = _dot(ldb, ldb)
    yield
    x = x + p + _dot(x.astype(BF16), p.astype(BF16))
    n = 4
    while n < base:
        pb = p.astype(BF16)
        p = _dot(pb, pb)
        yield
        x = x + p + _dot(x.astype(BF16), p.astype(BF16))
        yield
        n *= 2
    for join in joins:
        c = jnp.where(join, lmat, 0.0)
        xb = x.astype(BF16)
        ca = c + _dot(c.astype(BF16), xb)
        yield
        x = x - (ca + _dot(xb, ca.astype(BF16)))
        yield
    tmat = (eye + x).astype(BF16)
    egc = jnp.exp(gc)
    uw = _dot(tmat, jnp.concatenate([vb, kb * egc], axis=1).astype(BF16))
    yield
    u = uw[:, 0:GDN_DV]
    w = uw[:, GDN_DV:GDN_DV + GDN_DK]
    qg = q * egc
    g_last = jnp.concatenate(
        [jnp.broadcast_to(gc[(n + 1) * ck - 1:(n + 1) * ck, :], (ck, LANE)) for n in range(nblk)], axis=0)
    k_dec = (k * jnp.exp(g_last - gc)).astype(BF16)

    v_new, qs = [], []
    for n, (g, h) in enumerate(blocks):
        rs = slice(n * ck, (n + 1) * ck)
        sb = s_ref[g, h].astype(BF16)
        r = _dot(jnp.concatenate([w[rs], qg[rs]], axis=0).astype(BF16), sb)
        v_new.append(u[rs] - r[0:ck])
        qs.append(r[ck:2 * ck])
    yield
    v_new = jnp.concatenate(v_new, axis=0)
    vnb = v_new.astype(BF16)
    o = jnp.concatenate(qs, axis=0) + _dot(qk, vnb)
    for n, (g, h) in enumerate(blocks):
        rs = slice(n * ck, (n + 1) * ck)
        s_ref[g, h] = s_ref[g, h] * jnp.exp(gc[(n + 1) * ck - 1:(n + 1) * ck, :]) + _dot_tn(k_dec[rs], vnb[rs])
    yield

    o = _rms(o, nw) * (zs * jax.nn.sigmoid(zs))
    for n, (g, h) in enumerate(blocks):
        o_ref[g, :, h * GDN_DV:(h + 1) * GDN_DV] = o[n * ck:(n + 1) * ck].astype(o_ref.dtype)


def _gdn_kernel(q_ref, k_ref, v_ref, z_ref, ab_ref, conv0_ref, s0_ref, cw_ref, alog_ref, dtb_ref, nw_ref,
                *rest, ck, nseq, stack_rows, own_layer, conv_done):
    o_ref, sout_ref, xb_ref, s_ref, gsum_ref = rest[-5:]
    c = pl.program_id(1)
    ntile = QKV_WIDTH // LANE
    per_src = GDN_QK // LANE

    @pl.when(c == 0)
    def _():
        for lt in range(ntile):
            xb_ref[:, lt, 0:SUBLANE, :] = conv0_ref[:, :, lt * LANE:(lt + 1) * LANE]
        s_ref[...] = s0_ref[...]

    cw = cw_ref[...]
    first = SUBLANE - (CONV_W - 1)
    xc = []
    for lt in range(ntile):
        src = (q_ref, k_ref, v_ref)[lt // per_src]
        piece = src[:, :, (lt % per_src) * LANE:(lt % per_src + 1) * LANE]
        if conv_done:
            xc.append(piece)
            continue
        xb_ref[:, lt, SUBLANE:SUBLANE + ck, :] = piece
        acc = xb_ref[:, lt, first:first + ck, :] * cw[0:1, lt * LANE:(lt + 1) * LANE]
        for j in range(1, CONV_W):
            acc = acc + xb_ref[:, lt, first + j:first + j + ck, :] * cw[j:j + 1, lt * LANE:(lt + 1) * LANE]
        xb_ref[:, lt, 0:SUBLANE, :] = xb_ref[:, lt, ck:ck + SUBLANE, :]
        xc.append(acc * jax.nn.sigmoid(acc))

    ab = ab_ref[...]
    g_all = -jnp.exp(alog_ref[...]) * jax.nn.softplus(ab + dtb_ref[...])
    gsum_ref[:, 0:ck, :] = jnp.zeros_like(g_all)
    gc_all = g_all
    d = 1
    while d < ck:
        gsum_ref[:, ck:2 * ck, :] = gc_all
        gc_all = gc_all + gsum_ref[:, ck - d:2 * ck - d, :]
        d *= 2
    beta_all = jax.nn.sigmoid(ab)
    z = z_ref[...]
    nw = nw_ref[...]
    masks = _gdn_masks(stack_rows, ck)
    blocks = [(g, h) for g in range(nseq) for h in range(GDN_HEADS)]
    per = stack_rows // ck
    stacks = [_gdn_stack(blocks[b0:b0 + per], ck, masks, xc, gc_all, beta_all, z, nw, s_ref, o_ref)
              for b0 in range(0, len(blocks), per)]
    while stacks:
        stacks = [s for s in stacks if next(s, True) is None]

    @pl.when(c == pl.num_programs(1) - 1)
    def _():
        if own_layer is None:
            sout_ref[...] = s_ref[...]
        else:
            sout_ref[...] = jnp.zeros_like(sout_ref)
            sout_ref[own_layer] = s_ref[...]


def _gdn(proj3, conv0, s_all, s_done, w, layer, *, ck, nseq, stack_rows, odtype, conv_done):
    nb, nt, _ = proj3.shape
    assert (nseq * GDN_HEADS * ck) % stack_rows == 0 and stack_rows % ck == 0
    qblk = COL_QKV // GDN_QK
    state_block = (nseq, GDN_HEADS, GDN_DK, GDN_DV)
    state_spec = _lspec(layer, state_block, lambda i, c: (i, 0, 0, 0))
    head_row = _lspec(layer, (1, LANE), _zeros_map(2))
    operands = [proj3, proj3, proj3, proj3, proj3, conv0, s_all, w["gd_cw"], w["gd_alog"], w["gd_dtb"], w["gd_nw"]]
    extra_specs, aliases = [], {}
    if s_done is not None:
        aliases = {len(operands): 1}
        operands.append(s_done)
        extra_specs = [pl.BlockSpec(memory_space=pl.ANY)]
        out_state_spec, own_layer = state_spec, None
    else:
        out_state_spec = pl.BlockSpec((s_all.shape[0],) + state_block, lambda i, c: (0, i, 0, 0, 0))
        own_layer = layer
    return pl.pallas_call(
        functools.partial(_gdn_kernel, ck=ck, nseq=nseq, stack_rows=stack_rows, own_layer=own_layer,
                          conv_done=conv_done),
        grid=(nb // nseq, nt // ck),
        input_output_aliases=aliases,
        in_specs=[
            pl.BlockSpec((nseq, ck, GDN_QK), lambda i, c: (i, c, qblk)),
            pl.BlockSpec((nseq, ck, GDN_QK), lambda i, c: (i, c, qblk + 1)),
            pl.BlockSpec((nseq, ck, GDN_V), lambda i, c: (i, c, qblk + 2)),
            pl.BlockSpec((nseq, ck, GDN_V), lambda i, c: (i, c, COL_Z // GDN_V)),
            pl.BlockSpec((nseq, ck, N_AB), lambda i, c: (i, c, N_MAIN // N_AB)),
            _lspec(layer, (nseq, SUBLANE, QKV_WIDTH), lambda i, c: (i, 0, 0)),
            state_spec,
            _lspec(layer, (CONV_W, QKV_WIDTH), _zeros_map(2)),
            head_row,
            head_row,
            head_row,
        ] + extra_specs,
        out_specs=[
            pl.BlockSpec((nseq, ck, GDN_V), lambda i, c: (i, c, 0)),
            out_state_spec,
        ],
        out_shape=[
            jax.ShapeDtypeStruct((nb, nt, GDN_V), odtype),
            jax.ShapeDtypeStruct(s_all.shape, F32),
        ],
        scratch_shapes=[
            pltpu.VMEM((nseq, QKV_WIDTH // LANE, ck + SUBLANE, LANE), F32),
            pltpu.VMEM((nseq, GDN_HEADS, GDN_DK, GDN_DV), F32),
            pltpu.VMEM((nseq, 2 * ck, N_AB), F32),
        ],
        compiler_params=_cparams(("parallel", "arbitrary")),
        name="gdn",
    )(*operands)


def _merge_body(ins, cs, outs):
    x_ref, yrg_ref, ys5_ref, ygd_ref = ins
    gpre_ref, wgate_ref, wrg_ref, ws5_ref, wgd_ref, wo_ref, gpost_ref = cs
    (o_ref,) = outs
    x = x_ref[...]
    h = _rms(x, gpre_ref[...]).astype(BF16)
    mixed = None
    for n, (y_ref, w_ref) in enumerate(((yrg_ref, wrg_ref), (ys5_ref, ws5_ref), (ygd_ref, wgd_ref))):
        gate = jax.nn.sigmoid(_dot(h, wgate_ref[:, n * D_MODEL:(n + 1) * D_MODEL]))
        term = gate * _dot(y_ref[...].astype(BF16), w_ref[...])
        mixed = term if mixed is None else mixed + term
    out = _dot(mixed.astype(BF16), wo_ref[...])
    o_ref[...] = x + _rms(out, gpost_ref[...])


def _merge(groups, w, layer):
    half = D_MODEL // 2
    row = _lspec(layer, (1, D_MODEL), _zeros_map(2))
    branch_out = _resident(layer, (half, D_MODEL))
    tm = min([MERGE_TM] + [g[0].shape[0] for g in groups])
    outs = _row_tiled_call(
        _merge_body, groups,
        [w["g_pre_mix"], w["w_gates"], w["rg_wo"], w["s5_wo"], w["gd_wo"], w["w_out"], w["g_post_mix"]],
        [row, _resident(layer, (D_MODEL, N_BRANCH * D_MODEL)), branch_out, branch_out, branch_out,
         _resident(layer, (D_MODEL, D_MODEL)), row],
        [(D_MODEL, F32)], tm, "merge")
    return [o[0] for o in outs]


def _ffn_body(ins, cs, outs):
    (x_ref,), (gpre_ref, wg_ref, wu_ref, wd_ref, gpost_ref), (o_ref,) = ins, cs, outs
    x = x_ref[...]
    h = _rms(x, gpre_ref[...]).astype(BF16)
    acc = None
    for f0 in range(0, D_FF, FFN_TF):
        gate = _dot(h, wg_ref[:, f0:f0 + FFN_TF])
        act = (gate * jax.nn.sigmoid(gate)) * _dot(h, wu_ref[:, f0:f0 + FFN_TF])
        part = _dot(act.astype(BF16), wd_ref[f0:f0 + FFN_TF, :])
        acc = part if acc is None else acc + part
    o_ref[...] = x + _rms(acc, gpost_ref[...])


def _ffn(xs, w, layer):
    row = _lspec(layer, (1, D_MODEL), _zeros_map(2))
    tm = min([FFN_TM] + [x.shape[0] for x in xs])
    outs = _row_tiled_call(
        _ffn_body, [[x] for x in xs],
        [w["g_pre_ffn"], w["ffn_wg"], w["ffn_wu"], w["ffn_wd"], w["g_post_ffn"]],
        [row, _resident(layer, (D_MODEL, D_FF)), _resident(layer, (D_MODEL, D_FF)),
         _resident(layer, (D_FF, D_MODEL)), row],
        [(D_MODEL, F32)], tm, "ffn")
    return [o[0] for o in outs]


def _block_diag(blocks):
    n, r, c = blocks.shape[-3:]
    eye = jnp.eye(n, dtype=blocks.dtype)
    out = jnp.einsum("...nrc,nm->...nrmc", blocks, eye)
    return out.reshape(blocks.shape[:-3] + (n * r, n * c))


def _prep_params(p):
    f32 = F32
    depth = p["w_in"].shape[0]
    w_in = p["w_in"]
    sizes = (RG_WIDTH, RG_WIDTH, S5_WIDTH, QKV_WIDTH, GDN_V, GDN_HEADS, GDN_HEADS, 3 * D_MODEL)
    offs = [0]
    for s in sizes:
        offs.append(offs[-1] + s)
    assert offs[5] == N_MAIN
    w_main = w_in.astype(BF16)
    w_gates = w_main[:, :, offs[7]:offs[8]]

    pair = LANE // RG_BLOCK
    pairs = lambda a: _block_diag(a.reshape(depth, RG_WIDTH // LANE, pair, RG_BLOCK, RG_BLOCK))
    wg = jnp.concatenate([pairs(p["rg_wa"]), pairs(p["rg_wx"])], axis=-1).astype(BF16)

    a_re = p["s5_a_re"].astype(f32)
    a_im = p["s5_a_im"].astype(f32)
    dt = jnp.exp(p["s5_log_dt"].astype(f32))[..., None]
    mag = jnp.exp(a_re * dt)
    lr = mag * jnp.cos(a_im * dt)
    li = mag * jnp.sin(a_im * dt)
    den = a_re * a_re + a_im * a_im
    cr = (((lr - 1.0) * a_re + li * a_im) / den)[..., None]
    ci = ((li * a_re - (lr - 1.0) * a_im) / den)[..., None]
    b_re = p["s5_b_re"].astype(f32)
    b_im = p["s5_b_im"].astype(f32)
    bb_re = cr * b_re - ci * b_im
    bb_im = cr * b_im + ci * b_re
    nq = S5_WIDTH // LANE
    gq = S5_GROUPS // nq
    to_in = lambda a: _block_diag(jnp.swapaxes(a, 2, 3).reshape(depth, nq, gq, S5_GROUP, S5_STATE))
    to_out = lambda a: _block_diag(jnp.swapaxes(a.astype(f32), 2, 3).reshape(depth, nq, gq, S5_STATE, S5_GROUP))

    row = lambda a: a.astype(f32).reshape(depth, 1, -1)
    pad4 = lambda a: jnp.pad(a.astype(f32), ((0, 0), (0, LANE - GDN_HEADS))).reshape(depth, 1, LANE)
    return dict(
        g_pre_mix=row(p["norm_pre_mix"]), g_post_mix=row(p["norm_post_mix"]),
        g_pre_ffn=row(p["norm_pre_ffn"]), g_post_ffn=row(p["norm_post_ffn"]),
        w_main=w_main, w_gates=w_gates,
        rg_cw=p["rg_conv_w"].astype(f32), rg_cb=row(p["rg_conv_b"]), rg_wg=wg,
        rg_ba=row(p["rg_ba"]), rg_bx=row(p["rg_bx"]), rg_lam=row(p["rg_lambda"]),
        rg_wo=p["rg_w_out"].astype(BF16),
        s5_lr=lr.reshape(depth, 1, S5_FLAT), s5_li=li.reshape(depth, 1, S5_FLAT),
        s5_bre=to_in(bb_re).astype(BF16), s5_bim=to_in(bb_im).astype(BF16),
        s5_cre=to_out(p["s5_c_re"]).astype(BF16), s5_cim=to_out(p["s5_c_im"]).astype(BF16),
        s5_d=row(p["s5_d"]), s5_gw=p["s5_glu_w"].astype(BF16), s5_gb=row(p["s5_glu_b"]),
        s5_wo=p["s5_w_out"].astype(BF16),
        gd_cw=p["gdn_conv_w"].astype(f32), gd_alog=pad4(p["gdn_a_log"]), gd_dtb=pad4(p["gdn_dt_bias"]),
        gd_nw=row(p["gdn_norm_w"]), gd_wo=p["gdn_w_out"].astype(BF16),
        w_out=p["w_out"].astype(BF16),
        ffn_wg=p["ffn_w_gate"].astype(BF16), ffn_wu=p["ffn_w_up"].astype(BF16),
        ffn_wd=p["ffn_w_down"].astype(BF16),
    )


def _pad_conv_state(s):
    return jnp.pad(s, ((0, 0), (0, 0), (SUBLANE - (CONV_W - 1), 0), (0, 0)))


def _run_groups(xs, states, w, cfgs):
    keep = CONV_W - 1
    depth = states[0][1].shape[0]
    prepared = []
    for x, st, cfg in zip(xs, states, cfgs):
        nb, nt, _ = x.shape
        assert nt >= keep and nt % SUBLANE == 0 and nb % SUBLANE == 0
        st_rg_conv, st_rg_h, st_s5_re, st_s5_im, st_gd_conv, st_gd_s = st
        ydtype = BF16 if (cfg["tc"] % BF16_ROWS == 0 and cfg["ck"] % BF16_ROWS == 0) else F32
        prepared.append(dict(
            nb=nb, nt=nt, ydtype=ydtype, rg_conv0=_pad_conv_state(st_rg_conv), rg_h0=st_rg_h,
            s5_re0=st_s5_re.reshape(depth, nb, S5_FLAT), s5_im0=st_s5_im.reshape(depth, nb, S5_FLAT),
            gd_conv0=_pad_conv_state(st_gd_conv), gd_s0=st_gd_s, s_gd=None, outs=[[] for _ in range(5)]))
    xfs = [x.reshape(-1, D_MODEL) for x in xs]
    for l in range(depth):
        projs = _in_proj(xfs, [(g["nt"], cfg["zero_state"]) for g, cfg in zip(prepared, cfgs)], w, l)
        merge_in = []
        for xf, (proj, tails, conv_tiles), g, cfg in zip(xfs, projs, prepared, cfgs):
            nb, nt = g["nb"], g["nt"]
            conv_done = conv_tiles > 0
            flat = lambda a: a.reshape(nb * nt, a.shape[-1])
            proj3 = proj.reshape(nb, nt, N_PROJ)
            y_rg, h_last = _rglru(proj3, g["rg_conv0"], g["rg_h0"], w, l, tc=cfg["tc"], ydtype=g["ydtype"],
                                  conv_done=conv_done)
            y_s5, s_re, s_im = _s5(proj3, g["s5_re0"], g["s5_im0"], w, l, tc=cfg["tc"],
                                   tiles_per_pass=cfg["tiles_per_pass"], ydtype=g["ydtype"])
            y_gd, g["s_gd"] = _gdn(proj3, g["gd_conv0"], g["gd_s0"], g["s_gd"], w, l,
                                   ck=cfg["ck"], nseq=cfg["gdn_nseq"], stack_rows=cfg["gdn_stack_rows"],
                                   odtype=g["ydtype"], conv_done=conv_done)
            merge_in.append([xf, flat(y_rg), flat(y_s5), flat(y_gd)])
            if conv_done:
                last = tails.reshape(nb, conv_tiles, SUBLANE, CONV_TILES * LANE)[:, -1, SUBLANE - keep:, :]
                new_rg_conv, new_gd_conv = last[:, :, 0:RG_WIDTH], last[:, :, RG_WIDTH:]
            else:
                new_rg_conv = proj3[:, nt - keep:, COL_RG:COL_RG + RG_WIDTH]
                new_gd_conv = proj3[:, nt - keep:, COL_QKV:COL_QKV + QKV_WIDTH]
            g["outs"][0].append(new_rg_conv)
            g["outs"][1].append(h_last)
            g["outs"][2].append(s_re.reshape(nb, S5_GROUPS, S5_STATE))
            g["outs"][3].append(s_im.reshape(nb, S5_GROUPS, S5_STATE))
            g["outs"][4].append(new_gd_conv)
        xfs = _ffn(_merge(merge_in, w, l), w, l)
    return [(xf.reshape(g["nb"], g["nt"], D_MODEL), [jnp.stack(o) for o in g["outs"]] + [g["s_gd"]])
            for xf, g in zip(xfs, prepared)]


def _zero_states(depth, nb):
    return (jnp.zeros((depth, nb, CONV_W - 1, RG_WIDTH), F32),
            jnp.zeros((depth, nb, RG_WIDTH), F32),
            jnp.zeros((depth, nb, S5_GROUPS, S5_STATE), F32),
            jnp.zeros((depth, nb, S5_GROUPS, S5_STATE), F32),
            jnp.zeros((depth, nb, CONV_W - 1, QKV_WIDTH), F32),
            jnp.zeros((depth, nb, GDN_HEADS, GDN_DK, GDN_DV), F32))


def kernel(x_prompt, x_sample, state_rg_conv, state_rg_h, state_s5_re, state_s5_im, state_gdn_conv, state_gdn_s, norm_pre_mix, norm_post_mix, norm_pre_ffn, norm_post_ffn, w_in, rg_conv_w, rg_conv_b, rg_wa, rg_ba, rg_wx, rg_bx, rg_lambda, rg_w_out, s5_a_re, s5_a_im, s5_b_re, s5_b_im, s5_c_re, s5_c_im, s5_d, s5_log_dt, s5_glu_w, s5_glu_b, s5_w_out, gdn_conv_w, gdn_a_log, gdn_dt_bias, gdn_norm_w, gdn_w_out, w_out, ffn_w_gate, ffn_w_up, ffn_w_down):
    p = dict(norm_pre_mix=norm_pre_mix, norm_post_mix=norm_post_mix, norm_pre_ffn=norm_pre_ffn,
             norm_post_ffn=norm_post_ffn, w_in=w_in, rg_conv_w=rg_conv_w, rg_conv_b=rg_conv_b, rg_wa=rg_wa,
             rg_ba=rg_ba, rg_wx=rg_wx, rg_bx=rg_bx, rg_lambda=rg_lambda, rg_w_out=rg_w_out, s5_a_re=s5_a_re,
             s5_a_im=s5_a_im, s5_b_re=s5_b_re, s5_b_im=s5_b_im, s5_c_re=s5_c_re, s5_c_im=s5_c_im, s5_d=s5_d,
             s5_log_dt=s5_log_dt, s5_glu_w=s5_glu_w, s5_glu_b=s5_glu_b, s5_w_out=s5_w_out,
             gdn_conv_w=gdn_conv_w, gdn_a_log=gdn_a_log, gdn_dt_bias=gdn_dt_bias, gdn_norm_w=gdn_norm_w,
             gdn_w_out=gdn_w_out, w_out=w_out, ffn_w_gate=ffn_w_gate, ffn_w_up=ffn_w_up, ffn_w_down=ffn_w_down)
    depth = w_in.shape[0]
    w = _prep_params(p)
    sample_t = x_sample.shape[1]
    (y_p, st_p), (y_s, st_s) = _run_groups(
        [x_prompt, x_sample],
        [_zero_states(depth, x_prompt.shape[0]),
         (state_rg_conv, state_rg_h, state_s5_re, state_s5_im, state_gdn_conv, state_gdn_s)],
        w,
        [dict(tc=128, ck=64, gdn_nseq=8, gdn_stack_rows=MXU_EDGE, tiles_per_pass=8, zero_state=True),
         dict(tc=sample_t, ck=sample_t, gdn_nseq=16, gdn_stack_rows=MXU_EDGE, tiles_per_pass=1, zero_state=False)])
    return (y_p, y_s, *st_p, *st_s)
```
